```python
import math
import jax, jax.numpy as jnp
from jax import lax
import numpy as np

D_MODEL = 1024
BATCH = 4
SEQ = 4096
DEPTH = 2
DEC_BATCH = 128
DEC_SEQ = 8
PAST_LEN = 2048
PAGE_SIZE = 128

N_PAGES = PAST_LEN // PAGE_SIZE
N_PHYS_PAGES = DEC_BATCH * N_PAGES + max(1, (DEC_BATCH * N_PAGES) // 4)

D_MIX = D_MODEL
DIFF_W = D_MIX // 2
DIFF_H = 4
DIFF_DV = DIFF_W // DIFF_H
DIFF_DH = DIFF_DV // 2
GLA_W = D_MIX - DIFF_W
GLA_H = 4
GLA_DV = GLA_W // GLA_H
GLA_DK = GLA_DV // 2
GLA_LR = 16
GATE_NORM = 16.0
GLA_CHUNK = 64
N_BUCKETS = 32
MAX_DISTANCE = 128
QBLOCK = 128
N_MEM = 256
MEM_H = 4
MEM_DH = D_MODEL // MEM_H
MEM_W = MEM_H * MEM_DH
D_FF = ((8 * D_MODEL // 3 + 127) // 128) * 128
N_EXPERTS = 8
TOP_K = 2
N_DENSE = (DEPTH + 1) // 2
N_MOE = DEPTH // 2
IN_SIZES = (DIFF_H * 2 * DIFF_DH, DIFF_H * 2 * DIFF_DH, DIFF_W,
            GLA_H * GLA_DK, GLA_H * GLA_DK, GLA_W, GLA_W, GLA_LR)
IN_SPLIT = tuple(int(s) for s in np.cumsum(IN_SIZES)[:-1])
N_IN = int(sum(IN_SIZES))

kernel_name = 'hymba_diffattn_gla_memxattn_moe_step'


def rmsnorm(x, g, eps=1e-6):
    xf = x.astype(jnp.float32)
    y = xf * lax.rsqrt(jnp.mean(xf * xf, axis=-1, keepdims=True) + eps)
    return (y * g.astype(jnp.float32)).astype(x.dtype)


def t5_bucket(rel):
    n = jnp.maximum(rel, 0)
    max_exact = N_BUCKETS // 2
    nf = jnp.maximum(n, max_exact).astype(jnp.float32)
    large = max_exact + (jnp.log(nf / max_exact) / math.log(MAX_DISTANCE / max_exact)
                         * (N_BUCKETS - max_exact)).astype(jnp.int32)
    return jnp.where(n < max_exact, n, jnp.minimum(large, N_BUCKETS - 1))


def diff_attention(q, k, v, q_pos, k_pos, rel_bias, lam):
    B, Lq = q.shape[0], q.shape[1]
    Lk = k.shape[1]
    qf = q.astype(jnp.float32).reshape(B, Lq, DIFF_H, 2, DIFF_DH)
    kf = k.astype(jnp.float32).reshape(B, Lk, DIFF_H, 2, DIFF_DH)
    s = jnp.einsum('bqhcd,bkhcd->cbhqk', qf, kf) * (DIFF_DH ** -0.5)
    rel = q_pos[:, None] - k_pos[None, :]
    bias = jnp.transpose(rel_bias.astype(jnp.float32)[t5_bucket(rel)], (2, 0, 1))
    s = jnp.where(rel >= 0, s + bias, -jnp.inf)
    p = jax.nn.softmax(s, axis=-1)
    return jnp.einsum('bhqk,bkhv->bqhv', p[0] - lam * p[1], v.astype(jnp.float32))


def gla_chunked(q, k, v, logg, S0):
    B, L, H, DK = q.shape
    DV = v.shape[-1]
    C = GLA_CHUNK if L % GLA_CHUNK == 0 else L
    n = L // C

    def to_chunks(a):
        return a.astype(jnp.float32).reshape(B, n, C, H, a.shape[-1]).transpose(1, 0, 2, 3, 4)

    tri = jnp.tril(jnp.ones((C, C), dtype=bool))[None, :, :, None, None]

    def step(S, inp):
        qc, kc, vc, gc = inp
        b = jnp.cumsum(gc, axis=1)
        o_inter = jnp.einsum('bthk,bhkv->bthv', qc * jnp.exp(b), S)
        decay = jnp.exp(jnp.where(tri, b[:, :, None] - b[:, None, :], -jnp.inf))
        A = jnp.einsum('bthk,bshk,btshk->bhts', qc, kc, decay)
        o_intra = jnp.einsum('bhts,bshv->bthv', A, vc)
        b_last = b[:, -1]
        S_new = jnp.exp(b_last)[..., None] * S + jnp.einsum(
            'bshk,bshv->bhkv', kc * jnp.exp(b_last[:, None] - b), vc)
        return S_new, o_inter + o_intra

    S, o = lax.scan(step, S0.astype(jnp.float32),
                    (to_chunks(q), to_chunks(k), to_chunks(v), to_chunks(logg)))
    o = o.transpose(1, 0, 2, 3, 4).reshape(B, L, H, DV)
    return o, S


def mixer_block(hn, k_past, v_past, past_len, S0, rel_bias, w_in, w_a2, b_a, gla_g, subln,
                lq1, lk1, lq2, lk2, w_out, lambda_init):
    B, L, _ = hn.shape
    proj = hn @ w_in
    qa, ka, va, qg, kg, vg, gg, alr = jnp.split(proj, IN_SPLIT, axis=-1)
    qa = qa.reshape(B, L, DIFF_H, 2 * DIFF_DH)
    ka = ka.reshape(B, L, DIFF_H, 2 * DIFF_DH)
    va = va.reshape(B, L, DIFF_H, DIFF_DV)
    if k_past is None:
        k_all, v_all = ka, va
    else:
        k_all = jnp.concatenate([k_past.astype(ka.dtype), ka], axis=1)
        v_all = jnp.concatenate([v_past.astype(va.dtype), va], axis=1)
    q_pos = past_len + jnp.arange(L, dtype=jnp.int32)
    k_pos = jnp.arange(past_len + L, dtype=jnp.int32)
    lam = (jnp.exp(jnp.sum(lq1.astype(jnp.float32) * lk1.astype(jnp.float32)))
           - jnp.exp(jnp.sum(lq2.astype(jnp.float32) * lk2.astype(jnp.float32))) + lambda_init)
    if L > QBLOCK and L % QBLOCK == 0:
        nb = L // QBLOCK
        qb = qa.reshape(B, nb, QBLOCK, DIFF_H, 2 * DIFF_DH).transpose(1, 0, 2, 3, 4)
        pb = q_pos.reshape(nb, QBLOCK)
        o = lax.map(lambda qp: diff_attention(qp[0], k_all, v_all, qp[1], k_pos, rel_bias, lam), (qb, pb))
        o = o.transpose(1, 0, 2, 3, 4).reshape(B, L, DIFF_H, DIFF_DV)
    else:
        o = diff_attention(qa, k_all, v_all, q_pos, k_pos, rel_bias, lam)
    diff_out = (rmsnorm(o, subln, 1e-5) * (1.0 - lambda_init)).reshape(B, L, DIFF_W)
    qg = qg.reshape(B, L, GLA_H, GLA_DK) * (GLA_DK ** -0.5)
    kg = kg.reshape(B, L, GLA_H, GLA_DK)
    vg = vg.reshape(B, L, GLA_H, GLA_DV)
    logg = jax.nn.log_sigmoid((alr @ w_a2 + b_a).astype(jnp.float32)) / GATE_NORM
    og, S = gla_chunked(qg, kg, vg, logg.reshape(B, L, GLA_H, GLA_DK), S0)
    gla_out = rmsnorm(og, gla_g, 1e-5).reshape(B, L, GLA_W) * jax.nn.silu(gg.astype(jnp.float32))
    mix = jnp.concatenate([diff_out.astype(hn.dtype), gla_out.astype(hn.dtype)], axis=-1) @ w_out
    return mix, ka, va, S


def memory_kv(mem, g, w_mkv):
    B = mem.shape[0]
    mk, mv = jnp.split(rmsnorm(mem, g) @ w_mkv, 2, axis=-1)
    return mk.reshape(B, N_MEM, MEM_H, MEM_DH), mv.reshape(B, N_MEM, MEM_H, MEM_DH)


def cross_attention(hn, mk, mv, w_mq, w_mo):
    B, L, _ = hn.shape
    q = (hn @ w_mq).reshape(B, L, MEM_H, MEM_DH).astype(jnp.float32)
    s = jnp.einsum('blhd,bmhd->bhlm', q, mk.astype(jnp.float32)) * (MEM_DH ** -0.5)
    p = jax.nn.softmax(s, axis=-1)
    o = jnp.einsum('bhlm,bmhd->blhd', p, mv.astype(jnp.float32)).reshape(B, L, MEM_W)
    return o.astype(hn.dtype) @ w_mo


def swiglu(h, w_gu, w_d):
    g, u = jnp.split(h @ w_gu, 2, axis=-1)
    return (jax.nn.silu(g) * u) @ w_d


def moe_ffn(h, w_router, w_gu, w_d):
    probs = jax.nn.softmax((h @ w_router).astype(jnp.float32), axis=-1)
    topv, topi = lax.top_k(probs, TOP_K)
    topv = topv / jnp.sum(topv, axis=-1, keepdims=True)
    gates = jnp.sum(jax.nn.one_hot(topi, N_EXPERTS, dtype=jnp.float32) * topv[..., None], axis=-2)
    out = jnp.zeros(h.shape, jnp.float32)
    for e in range(N_EXPERTS):
        out = out + gates[..., e:e + 1] * swiglu(h, w_gu[e], w_d[e]).astype(jnp.float32)
    return out.astype(h.dtype)


def setup_inputs(seed: int = 0) -> dict:
    key = jax.random.key(seed)
    ks = iter(jax.random.split(key, 48))
    f32 = jnp.float32

    def nrm(shape, scale):
        return jax.random.normal(next(ks), shape, f32) * scale

    def gain(shape):
        return 1.0 + nrm(shape, 0.02)

    page_table = jax.random.permutation(next(ks), N_PHYS_PAGES)[:DEC_BATCH * N_PAGES]
    page_table = page_table.reshape(DEC_BATCH, N_PAGES).astype(jnp.int32)
    return {
        'x_prompt': nrm((BATCH, SEQ, D_MODEL), 1.0),
        'x_sample': nrm((DEC_BATCH, DEC_SEQ, D_MODEL), 1.0),
        'mem_prompt': nrm((BATCH, N_MEM, D_MODEL), 1.0),
        'cache_attn_k': nrm((DEPTH, N_PHYS_PAGES, PAGE_SIZE, DIFF_H, 2 * DIFF_DH), 1.0),
        'cache_attn_v': nrm((DEPTH, N_PHYS_PAGES, PAGE_SIZE, DIFF_H, DIFF_DV), 1.0),
        'cache_mem_k': nrm((DEPTH, DEC_BATCH, N_MEM, MEM_H, MEM_DH), 1.0),
        'cache_mem_v': nrm((DEPTH, DEC_BATCH, N_MEM, MEM_H, MEM_DH), 1.0),
        'state_gla': nrm((DEPTH, DEC_BATCH, GLA_H, GLA_DK, GLA_DV), 0.1),
        'page_table': page_table,
        'rel_bias': nrm((N_BUCKETS, DIFF_H), 0.5),
        'norm_mix': gain((DEPTH, D_MODEL)),
        'w_in': nrm((DEPTH, D_MODEL, N_IN), D_MODEL ** -0.5),
        'w_gla_a2': nrm((DEPTH, GLA_LR, GLA_H * GLA_DK), GLA_LR ** -0.5),
        'b_gla_a': nrm((DEPTH, GLA_H * GLA_DK), 0.1),
        'gla_norm': gain((DEPTH, GLA_DV)),
        'diff_subln': gain((DEPTH, DIFF_DV)),
        'lambda_q1': nrm((DEPTH, DIFF_DH), 0.1),
        'lambda_k1': nrm((DEPTH, DIFF_DH), 0.1),
        'lambda_q2': nrm((DEPTH, DIFF_DH), 0.1),
        'lambda_k2': nrm((DEPTH, DIFF_DH), 0.1),
        'w_out': nrm((DEPTH, D_MIX, D_MODEL), D_MIX ** -0.5),
        'norm_mem': gain((DEPTH, D_MODEL)),
        'norm_memkv': gain((DEPTH, D_MODEL)),
        'w_mq': nrm((DEPTH, D_MODEL, MEM_W), D_MODEL ** -0.5),
        'w_mkv': nrm((DEPTH, D_MODEL, 2 * MEM_W), D_MODEL ** -0.5),
        'w_mo': nrm((DEPTH, MEM_W, D_MODEL), MEM_W ** -0.5),
        'norm_ffn': gain((DEPTH, D_MODEL)),
        'w_ffn_gu': nrm((N_DENSE, D_MODEL, 2 * D_FF), D_MODEL ** -0.5),
        'w_ffn_down': nrm((N_DENSE, D_FF, D_MODEL), D_FF ** -0.5),
        'w_router': nrm((N_MOE, D_MODEL, N_EXPERTS), D_MODEL ** -0.5),
        'w_exp_gu': nrm((N_MOE, N_EXPERTS, D_MODEL, 2 * D_FF), D_MODEL ** -0.5),
        'w_exp_down': nrm((N_MOE, N_EXPERTS, D_FF, D_MODEL), D_FF ** -0.5),
        'norm_final': gain((D_MODEL,)),
    }


def reference(x_prompt, x_sample, mem_prompt, cache_attn_k, cache_attn_v, cache_mem_k, cache_mem_v,
              state_gla, page_table, rel_bias, norm_mix, w_in, w_gla_a2, b_gla_a, gla_norm, diff_subln,
              lambda_q1, lambda_k1, lambda_q2, lambda_k2, w_out, norm_mem, norm_memkv, w_mq, w_mkv, w_mo,
              norm_ffn, w_ffn_gu, w_ffn_down, w_router, w_exp_gu, w_exp_down, norm_final):

    def run_layer(l, h, k_past, v_past, past_len, S0, mk, mv):
        lambda_init = 0.8 - 0.6 * math.exp(-0.3 * l)
        mix, k_new, v_new, S = mixer_block(
            rmsnorm(h, norm_mix[l]), k_past, v_past, past_len, S0, rel_bias, w_in[l], w_gla_a2[l],
            b_gla_a[l], gla_norm[l], diff_subln[l], lambda_q1[l], lambda_k1[l], lambda_q2[l],
            lambda_k2[l], w_out[l], lambda_init)
        h = h + mix
        h = h + cross_attention(rmsnorm(h, norm_mem[l]), mk, mv, w_mq[l], w_mo[l])
        hn = rmsnorm(h, norm_ffn[l])
        if l % 2 == 0:
            f = swiglu(hn, w_ffn_gu[l // 2], w_ffn_down[l // 2])
        else:
            f = moe_ffn(hn, w_router[l // 2], w_exp_gu[l // 2], w_exp_down[l // 2])
        return h + f, k_new, v_new, S

    B, L = x_prompt.shape[0], x_prompt.shape[1]
    h = x_prompt
    pk, pv, ps, pmk, pmv = [], [], [], [], []
    for l in range(DEPTH):
        mk, mv = memory_kv(mem_prompt, norm_memkv[l], w_mkv[l])
        S0 = jnp.zeros((B, GLA_H, GLA_DK, GLA_DV), jnp.float32)
        h, k_new, v_new, S = run_layer(l, h, None, None, 0, S0, mk, mv)
        pk.append(k_new); pv.append(v_new); ps.append(S); pmk.append(mk); pmv.append(mv)
    y_prompt = rmsnorm(h, norm_final)

    DB = x_sample.shape[0]
    past_len = page_table.shape[1] * cache_attn_k.shape[2]
    h = x_sample
    sk, sv, ss = [], [], []
    for l in range(DEPTH):
        k_past = cache_attn_k[l][page_table].reshape(DB, past_len, DIFF_H, 2 * DIFF_DH)
        v_past = cache_attn_v[l][page_table].reshape(DB, past_len, DIFF_H, DIFF_DV)
        h, k_new, v_new, S = run_layer(l, h, k_past, v_past, past_len, state_gla[l],
                                       cache_mem_k[l], cache_mem_v[l])
        sk.append(k_new); sv.append(v_new); ss.append(S)
    y_sample = rmsnorm(h, norm_final)

    return (y_prompt, y_sample, jnp.stack(pk), jnp.stack(pv), jnp.stack(ps), jnp.stack(pmk),
            jnp.stack(pmv), jnp.stack(sk), jnp.stack(sv), jnp.stack(ss))
```

```python
import functools
import math

import numpy as np
import jax
import jax.numpy as jnp
from jax import lax
from jax.experimental import pallas as pl
from jax.experimental.pallas import tpu as pltpu

F32 = jnp.float32
BF16 = jnp.bfloat16
I32 = jnp.int32

LANES = 128
SUBLANES = 8
VMEM_LIMIT_BYTES = 56 * 1024 * 1024

D_MODEL = 1024
DIFF_H = 4
DIFF_DV = 128
DIFF_DH = 64
DIFF_W = DIFF_H * DIFF_DV
GLA_H = 4
GLA_DK = 64
GLA_DV = 128
GLA_W = GLA_H * GLA_DV
GLA_LR = 16
GATE_NORM = 16.0
N_BUCKETS = 32
MAX_DISTANCE = 128
MEM_H = 4
MEM_DH = 256
N_EXPERTS = 8
TOP_K = 2
N_IN = 3 * DIFF_W + 2 * GLA_H * GLA_DK + 2 * GLA_W + GLA_LR
N_IN_PAD = 3200
NEG_BIG = -1e30

COL_Q = 0
COL_K = 4
COL_V = 8
COL_QK_GLA = 3
COL_V_GLA = 4
COL_G_GLA = 5
COL_ALR = 24

NT_DIMS = (((1,), (1,)), ((), ()))
TN_DIMS = (((0,), (0,)), ((), ()))


def _params(*sem):
    return pltpu.CompilerParams(dimension_semantics=sem, vmem_limit_bytes=VMEM_LIMIT_BYTES)


def _rms(x, g, eps):
    ms = jnp.mean(x * x, axis=-1, keepdims=True)
    return (x * lax.rsqrt(ms + eps)) * g


def _silu(x):
    return x / (1.0 + jnp.exp(-x))


def _rms_matmul_kernel(x_ref, g_ref, w_ref, o_ref, *, eps, scale):
    xn = _rms(x_ref[...], g_ref[...], eps).astype(BF16)
    y = jnp.dot(xn, w_ref[...], preferred_element_type=F32)
    if scale != 1.0:
        y = y * scale
    o_ref[...] = y.astype(o_ref.dtype)


def rms_matmul(x, g, w, *, tm, out_dtype=F32, scale=1.0, eps=1e-6):
    M, K = x.shape
    N = w.shape[1]
    return pl.pallas_call(
        functools.partial(_rms_matmul_kernel, eps=eps, scale=scale),
        grid=(M // tm,),
        in_specs=[pl.BlockSpec((tm, K), lambda i: (i, 0)),
                  pl.BlockSpec((1, K), lambda i: (0, 0)),
                  pl.BlockSpec((K, N), lambda i: (0, 0))],
        out_specs=pl.BlockSpec((tm, N), lambda i: (i, 0)),
        out_shape=jax.ShapeDtypeStruct((M, N), out_dtype),
        compiler_params=_params("parallel"),
        name="rms_matmul",
    )(x, g.reshape(1, K), w)


def _mm_res_kernel(*refs, n_lhs):
    a_refs, w_refs = refs[:n_lhs], refs[n_lhs:2 * n_lhs]
    res_ref, o_ref = refs[2 * n_lhs], refs[2 * n_lhs + 1]
    acc = res_ref[...]
    for a_ref, w_ref in zip(a_refs, w_refs):
        acc = acc + jnp.dot(a_ref[...], w_ref[...], preferred_element_type=F32)
    o_ref[...] = acc


def mm_res(lhs, ws, res, *, tm):
    M, N = res.shape
    n = len(lhs)
    in_specs = ([pl.BlockSpec((tm, a.shape[1]), lambda i: (i, 0)) for a in lhs]
                + [pl.BlockSpec(w.shape, lambda i: (0, 0)) for w in ws]
                + [pl.BlockSpec((tm, N), lambda i: (i, 0))])
    return pl.pallas_call(
        functools.partial(_mm_res_kernel, n_lhs=n),
        grid=(M // tm,),
        in_specs=in_specs,
        out_specs=pl.BlockSpec((tm, N), lambda i: (i, 0)),
        out_shape=jax.ShapeDtypeStruct((M, N), F32),
        compiler_params=_params("parallel"),
        name="mm_res",
    )(*lhs, *ws, res)


def _bias_kernel(tab_ref, o_ref, *, offsets, rows_per_head):
    h = pl.program_id(0)
    R, C = o_ref.shape[-2], o_ref.shape[-1]
    r = lax.broadcasted_iota(I32, (R, C), 0)
    if rows_per_head != R:
        r = r % rows_per_head
    c = lax.broadcasted_iota(I32, (R, C), 1)
    max_exact = N_BUCKETS // 2
    far = tab_ref[N_BUCKETS - 1, h]
    for kind, off in enumerate(offsets):
        rel = off + r - c
        n = jnp.maximum(rel, 0)
        nf = jnp.maximum(n, max_exact).astype(F32)
        large = max_exact + (jnp.log(nf / max_exact) / math.log(MAX_DISTANCE / max_exact)
                             * (N_BUCKETS - max_exact)).astype(I32)
        bucket = jnp.where(n < max_exact, n, jnp.minimum(large, N_BUCKETS - 1))
        acc = jnp.zeros((R, C), F32)
        for b in range(N_BUCKETS - 1):
            acc = jnp.where(bucket == b, tab_ref[b, h] - far, acc)
        o_ref[kind] = jnp.where(rel >= 0, acc, NEG_BIG)


def bias_tiles(rel_bias, *, R, C, offsets, rows_per_head=None):
    rows_per_head = R if rows_per_head is None else rows_per_head
    return pl.pallas_call(
        functools.partial(_bias_kernel, offsets=tuple(offsets), rows_per_head=rows_per_head),
        grid=(DIFF_H,),
        in_specs=[pl.BlockSpec(memory_space=pltpu.SMEM)],
        out_specs=pl.BlockSpec((None, len(offsets), R, C), lambda h: (h, 0, 0, 0)),
        out_shape=jax.ShapeDtypeStruct((DIFF_H, len(offsets), R, C), F32),
        compiler_params=_params("arbitrary"),
        name="bias_tiles",
    )(rel_bias)


def _lambda_value(lamp, lambda_init):
    s1 = jnp.sum(lamp[0:1, :] * lamp[1:2, :], axis=-1, keepdims=True)
    s2 = jnp.sum(lamp[2:3, :] * lamp[3:4, :], axis=-1, keepdims=True)
    return jnp.exp(s1) - jnp.exp(s2) + lambda_init


def _diff_finish(o1, o2, lam, subln, lambda_init):
    o = o1 - lam * o2
    return _rms(o, subln, 1e-5) * (1.0 - lambda_init)


def _diff_attn_kernel(qi_tab, ki_tab, q_ref, k_ref, v_ref, bias_ref, lamp_ref, subln_ref, o_ref,
                      q2_sc, m_sc, l_sc, acc_sc, *, T, lambda_init):
    t = pl.program_id(2)
    qi = qi_tab[t]
    ki = ki_tab[t]

    @pl.when(ki == 0)
    def _init():
        q = q_ref[...] * (DIFF_DH ** -0.5)
        lane = lax.broadcasted_iota(I32, q.shape, 1)
        q2_sc[0:T, :] = jnp.where(lane < DIFF_DH, q, 0.0).astype(BF16)
        q2_sc[T:2 * T, :] = jnp.where(lane >= DIFF_DH, q, 0.0).astype(BF16)
        m_sc[...] = jnp.full(m_sc.shape, NEG_BIG, F32)
        l_sc[...] = jnp.zeros(l_sc.shape, F32)
        acc_sc[...] = jnp.zeros(acc_sc.shape, F32)

    def scores():
        return lax.dot_general(q2_sc[...], k_ref[...].astype(BF16), NT_DIMS, preferred_element_type=F32)

    def update(s):
        m_old = m_sc[...]
        m_new = jnp.maximum(m_old, jnp.max(s, axis=-1, keepdims=True))
        alpha = jnp.exp(m_old - m_new)
        p = jnp.exp(s - m_new)
        l_sc[...] = alpha * l_sc[...] + jnp.sum(p, axis=-1, keepdims=True)
        acc_sc[...] = alpha * acc_sc[...] + jnp.dot(p.astype(BF16), v_ref[...].astype(BF16),
                                                     preferred_element_type=F32)
        m_sc[...] = m_new

    def biased(kind):
        b = bias_ref[kind]
        return (scores().reshape(2, T, T) + b[None]).reshape(2 * T, T)

    @pl.when(ki < qi - 1)
    def _far():
        update(scores())

    @pl.when(ki == qi - 1)
    def _near():
        update(biased(1))

    @pl.when(ki == qi)
    def _diag():
        update(biased(0))
        on = acc_sc[...] / l_sc[...]
        lam = _lambda_value(lamp_ref[...], lambda_init)
        o_ref[...] = _diff_finish(on[0:T], on[T:2 * T], lam, subln_ref[...], lambda_init).astype(o_ref.dtype)


def diff_attention_prompt(proj, bias, lamp, subln, *, T, lambda_init):
    B, L, _ = proj.shape
    nq = L // T
    pairs = [(qi, ki) for qi in range(nq) for ki in range(qi + 1)]
    qi_tab = jnp.asarray(np.array([p[0] for p in pairs], np.int32))
    ki_tab = jnp.asarray(np.array([p[1] for p in pairs], np.int32))
    grid_spec = pltpu.PrefetchScalarGridSpec(
        num_scalar_prefetch=2,
        grid=(B, DIFF_H, len(pairs)),
        in_specs=[
            pl.BlockSpec((None, T, DIFF_DV), lambda b, h, t, qt, kt: (b, qt[t], COL_Q + h)),
            pl.BlockSpec((None, T, DIFF_DV), lambda b, h, t, qt, kt: (b, kt[t], COL_K + h)),
            pl.BlockSpec((None, T, DIFF_DV), lambda b, h, t, qt, kt: (b, kt[t], COL_V + h)),
            pl.BlockSpec((None, 2, T, T), lambda b, h, t, qt, kt: (h, 0, 0, 0)),
            pl.BlockSpec((4, DIFF_DH), lambda b, h, t, qt, kt: (0, 0)),
            pl.BlockSpec((1, DIFF_DV), lambda b, h, t, qt, kt: (0, 0)),
        ],
        out_specs=pl.BlockSpec((None, T, DIFF_DV), lambda b, h, t, qt, kt: (b, qt[t], h)),
        scratch_shapes=[pltpu.VMEM((2 * T, DIFF_DV), BF16),
                        pltpu.VMEM((2 * T, 1), F32),
                        pltpu.VMEM((2 * T, 1), F32),
                        pltpu.VMEM((2 * T, DIFF_DV), F32)],
    )
    return pl.pallas_call(
        functools.partial(_diff_attn_kernel, T=T, lambda_init=lambda_init),
        grid_spec=grid_spec,
        out_shape=jax.ShapeDtypeStruct((B, L, DIFF_W), BF16),
        compiler_params=_params("parallel", "parallel", "arbitrary"),
        name="diff_attn_prompt",
    )(qi_tab, ki_tab, proj, proj, proj, bias, lamp, subln.reshape(1, DIFF_DV))


def _diff_decode_kernel(pt_ref, proj_ref, bias_ref, lamp_ref, subln_ref, *rest, n_pages, page, lq, lambda_init):
    k_refs, v_refs = rest[:n_pages], rest[n_pages:2 * n_pages]
    o_ref = rest[2 * n_pages]
    q = proj_ref[:, 0:DIFF_W] * (DIFF_DH ** -0.5)
    k_new = proj_ref[:, DIFF_W:2 * DIFF_W]
    v_new = proj_ref[:, 2 * DIFF_W:3 * DIFF_W]
    lane = lax.broadcasted_iota(I32, q.shape, 1)
    blocks = []
    for h in range(DIFF_H):
        for c in range(2):
            lo = h * DIFF_DV + c * DIFF_DH
            blocks.append(jnp.where(lane < lo, 0.0, jnp.where(lane < lo + DIFF_DH, q, 0.0)))
    qbd = jnp.concatenate(blocks, axis=0).astype(BF16)
    pad = jnp.zeros((page - lq, DIFF_W), F32)
    k_tail = jnp.concatenate([k_new, pad], axis=0).astype(BF16)
    v_tail = jnp.concatenate([v_new, pad], axis=0).astype(BF16)
    parts = [lax.dot_general(qbd, k_refs[j][...].astype(BF16), NT_DIMS, preferred_element_type=F32)
             for j in range(n_pages)]
    parts.append(lax.dot_general(qbd, k_tail, NT_DIMS, preferred_element_type=F32))
    s = jnp.concatenate(parts, axis=1) + bias_ref[...]
    m = jnp.max(s, axis=-1, keepdims=True)
    p = jnp.exp(s - m)
    l = jnp.sum(p, axis=-1, keepdims=True)
    pb = p.astype(BF16)
    acc = jnp.dot(pb[:, n_pages * page:], v_tail, preferred_element_type=F32)
    for j in range(n_pages):
        acc = acc + jnp.dot(pb[:, j * page:(j + 1) * page], v_refs[j][...].astype(BF16),
                            preferred_element_type=F32)
    on = acc / l
    lam = _lambda_value(lamp_ref[...], lambda_init)
    for h in range(DIFF_H):
        r0 = 2 * lq * h
        cols = slice(h * DIFF_DV, (h + 1) * DIFF_DV)
        o = _diff_finish(on[r0:r0 + lq, cols], on[r0 + lq:r0 + 2 * lq, cols], lam, subln_ref[...], lambda_init)
        o_ref[:, cols] = o.astype(o_ref.dtype)


def diff_attention_decode(proj, cache_k, cache_v, page_table, bias, lamp, subln, *, layer, lambda_init):
    DB, lq, _ = proj.shape
    n_pages = page_table.shape[1]
    page = cache_k.shape[2]

    def page_spec(j):
        return pl.BlockSpec((None, None, page, DIFF_W), lambda b, pt: (layer, pt[b, j], 0, 0))

    grid_spec = pltpu.PrefetchScalarGridSpec(
        num_scalar_prefetch=1,
        grid=(DB,),
        in_specs=([pl.BlockSpec((None, lq, 3 * DIFF_W), lambda b, pt: (b, 0, 0)),
                   pl.BlockSpec(bias.shape, lambda b, pt: (0, 0)),
                   pl.BlockSpec((4, DIFF_DH), lambda b, pt: (0, 0)),
                   pl.BlockSpec((1, DIFF_DV), lambda b, pt: (0, 0))]
                  + [page_spec(j) for j in range(n_pages)]
                  + [page_spec(j) for j in range(n_pages)]),
        out_specs=pl.BlockSpec((None, lq, DIFF_W), lambda b, pt: (b, 0, 0)),
    )
    return pl.pallas_call(
        functools.partial(_diff_decode_kernel, n_pages=n_pages, page=page, lq=lq, lambda_init=lambda_init),
        grid_spec=grid_spec,
        out_shape=jax.ShapeDtypeStruct((DB, lq, DIFF_W), BF16),
        compiler_params=_params("parallel"),
        name="diff_attn_decode",
    )(page_table, proj, bias, lamp, subln.reshape(1, DIFF_DV),
      *([cache_k] * n_pages), *([cache_v] * n_pages))


def _gla_kernel(qk_ref, v_ref, gg_ref, alr_ref, wa_ref, ba_ref, gn_ref, s0_ref, o_ref, sout_ref, st_sc,
                *, C, SB, n_chunks):
    c = pl.program_id(1)

    @pl.when(c == 0)
    def _init():
        st_sc[...] = s0_ref[...]

    z = jnp.dot(alr_ref[...], wa_ref[...], precision=lax.Precision.HIGHEST,
                preferred_element_type=F32) + ba_ref[...]
    logg = (jnp.minimum(z, 0.0) - jnp.log1p(jnp.exp(-jnp.abs(z)))) * (1.0 / GATE_NORM)
    row = lax.broadcasted_iota(I32, logg.shape, 0)
    b = logg
    d = 1
    while d < C:
        b = b + jnp.where(row >= d, pltpu.roll(b, d, 0), 0.0)
        d *= 2
    q_all = qk_ref[:, 0:GLA_H * GLA_DK] * (GLA_DK ** -0.5)
    k_all = qk_ref[:, GLA_H * GLA_DK:2 * GLA_H * GLA_DK]
    v_all = v_ref[...]
    n_sub = C // SB
    hk = lambda a, h: a[:, h * GLA_DK:(h + 1) * GLA_DK]
    hv = lambda a, h: a[:, h * GLA_DV:(h + 1) * GLA_DV]

    outs = []
    for h in range(GLA_H):
        q, k, bh = hk(q_all, h), hk(k_all, h), hk(b, h)
        vb = hv(v_all, h).astype(BF16)
        st = st_sc[h]
        o = lax.dot_general((q * jnp.exp(bh)).astype(BF16), st.astype(BF16), NT_DIMS,
                            preferred_element_type=F32)
        if n_sub > 1:
            col = lax.broadcasted_iota(I32, (SB, C), 1)
            rows = [jnp.zeros((SB, C), F32)]
            for i in range(1, n_sub):
                ref = bh[i * SB:i * SB + 1, :]
                qt = q[i * SB:(i + 1) * SB] * jnp.exp(bh[i * SB:(i + 1) * SB] - ref)
                kt = k * jnp.exp(jnp.minimum(ref - bh, 0.0))
                a = lax.dot_general(qt.astype(BF16), kt.astype(BF16), NT_DIMS, preferred_element_type=F32)
                rows.append(jnp.where(col < i * SB, a, 0.0))
            a_off = jnp.concatenate(rows, axis=0)
            o = o + jnp.dot(a_off.astype(BF16), vb, preferred_element_type=F32)
        b_last = bh[C - 1:C, :]
        kdec = k * jnp.exp(b_last - bh)
        st_sc[h] = st * jnp.exp(b_last) + lax.dot_general(vb, kdec.astype(BF16), TN_DIMS,
                                                          preferred_element_type=F32)
        outs.append(o)

    rmod = lax.broadcasted_iota(I32, (C, 1), 0) % SB
    for dlt in range(SB):
        if dlt == 0:
            kd, bd, vd = k_all, b, v_all
        else:
            kd, bd, vd = pltpu.roll(k_all, dlt, 0), pltpu.roll(b, dlt, 0), pltpu.roll(v_all, dlt, 0)
        x = q_all * kd * jnp.exp(jnp.minimum(b - bd, 0.0))
        keep = rmod >= dlt
        for h in range(GLA_H):
            w = jnp.where(keep, jnp.sum(hk(x, h), axis=-1, keepdims=True), 0.0)
            outs[h] = outs[h] + w * hv(vd, h)

    for h in range(GLA_H):
        gate = _silu(hv(gg_ref[...], h))
        o_ref[:, h * GLA_DV:(h + 1) * GLA_DV] = (_rms(outs[h], gn_ref[...], 1e-5) * gate).astype(o_ref.dtype)

    @pl.when(c == n_chunks - 1)
    def _fin():
        sout_ref[...] = st_sc[...]


def gla(proj, w_a2p, b_a, gla_g, s0t, *, C, SB):
    B, L, _ = proj.shape
    n_chunks = L // C
    W = GLA_W
    return pl.pallas_call(
        functools.partial(_gla_kernel, C=C, SB=SB, n_chunks=n_chunks),
        grid=(B, n_chunks),
        in_specs=[
            pl.BlockSpec((None, C, W), lambda b, c: (b, c, COL_QK_GLA)),
            pl.BlockSpec((None, C, W), lambda b, c: (b, c, COL_V_GLA)),
            pl.BlockSpec((None, C, W), lambda b, c: (b, c, COL_G_GLA)),
            pl.BlockSpec((None, C, LANES), lambda b, c: (b, c, COL_ALR)),
            pl.BlockSpec((LANES, GLA_H * GLA_DK), lambda b, c: (0, 0)),
            pl.BlockSpec((1, GLA_H * GLA_DK), lambda b, c: (0, 0)),
            pl.BlockSpec((1, GLA_DV), lambda b, c: (0, 0)),
            pl.BlockSpec((None, GLA_H, GLA_DV, GLA_DK), lambda b, c: (b, 0, 0, 0)),
        ],
        out_specs=[pl.BlockSpec((None, C, W), lambda b, c: (b, c, 0)),
                   pl.BlockSpec((None, GLA_H, GLA_DV, GLA_DK), lambda b, c: (b, 0, 0, 0))],
        out_shape=[jax.ShapeDtypeStruct((B, L, W), BF16),
                   jax.ShapeDtypeStruct((B, GLA_H, GLA_DV, GLA_DK), F32)],
        scratch_shapes=[pltpu.VMEM((GLA_H, GLA_DV, GLA_DK), F32)],
        compiler_params=_params("parallel", "arbitrary"),
        name="gla",
    )(proj, proj, proj, proj, w_a2p, b_a.reshape(1, -1), gla_g.reshape(1, -1), s0t)


def _xattn_kernel(q_ref, mk_ref, mv_ref, o_ref):
    for h in range(MEM_H):
        cols = slice(h * MEM_DH, (h + 1) * MEM_DH)
        s = lax.dot_general(q_ref[:, cols], mk_ref[:, cols].astype(BF16), NT_DIMS, preferred_element_type=F32)
        m = jnp.max(s, axis=-1, keepdims=True)
        p = jnp.exp(s - m)
        l = jnp.sum(p, axis=-1, keepdims=True)
        o = jnp.dot(p.astype(BF16), mv_ref[:, cols].astype(BF16), preferred_element_type=F32) / l
        o_ref[:, cols] = o.astype(o_ref.dtype)


def xattn_core(q, mk, mv, *, tm):
    Bx, Lx, W = q.shape
    n_mem = mk.shape[1]
    return pl.pallas_call(
        _xattn_kernel,
        grid=(Bx, Lx // tm),
        in_specs=[pl.BlockSpec((None, tm, W), lambda b, i: (b, i, 0)),
                  pl.BlockSpec((None, n_mem, W), lambda b, i: (b, 0, 0)),
                  pl.BlockSpec((None, n_mem, W), lambda b, i: (b, 0, 0))],
        out_specs=pl.BlockSpec((None, tm, W), lambda b, i: (b, i, 0)),
        out_shape=jax.ShapeDtypeStruct((Bx, Lx, W), BF16),
        compiler_params=_params("parallel", "arbitrary"),
        name="xattn_core",
    )(q, mk, mv)


def _ffn_kernel(te_ref, tv_ref, x_ref, g_ref, wg_ref, wu_ref, wd_ref, sc_ref, o_ref, xn_sc, acc_sc,
                *, n_f, dense):
    i = pl.program_id(0)
    f = pl.program_id(1)

    @pl.when(f == 0)
    def _init():
        xn_sc[...] = _rms(x_ref[...], g_ref[...], 1e-6).astype(BF16)
        acc_sc[...] = jnp.zeros(acc_sc.shape, F32)

    @pl.when(tv_ref[i] != 0)
    def _compute():
        xn = xn_sc[...]
        g = jnp.dot(xn, wg_ref[...], preferred_element_type=F32)
        u = jnp.dot(xn, wu_ref[...], preferred_element_type=F32)
        a = (_silu(g) * u).astype(BF16)
        acc_sc[...] += jnp.dot(a, wd_ref[...], preferred_element_type=F32)

    @pl.when(f == n_f - 1)
    def _fin():
        if dense:
            o_ref[...] = x_ref[...] + acc_sc[...]
        else:
            o_ref[...] = acc_sc[...] * sc_ref[...]


def ffn(x, g, w_gu, w_d, tile_expert, tile_valid, row_scale, *, tm, tf, dense):
    Mp, D = x.shape
    F = w_d.shape[1]
    n_f = F // tf
    last = n_f - 1

    def fblk(i, f, tv):
        return f * tv[i] + last * (1 - tv[i])

    grid_spec = pltpu.PrefetchScalarGridSpec(
        num_scalar_prefetch=2,
        grid=(Mp // tm, n_f),
        in_specs=[
            pl.BlockSpec((tm, D), lambda i, f, te, tv: (i, 0)),
            pl.BlockSpec((1, D), lambda i, f, te, tv: (0, 0)),
            pl.BlockSpec((None, D, tf), lambda i, f, te, tv: (te[i], 0, fblk(i, f, tv))),
            pl.BlockSpec((None, D, tf), lambda i, f, te, tv: (te[i], 0, n_f + fblk(i, f, tv))),
            pl.BlockSpec((None, tf, D), lambda i, f, te, tv: (te[i], fblk(i, f, tv), 0)),
            pl.BlockSpec((tm, 1), lambda i, f, te, tv: (i, 0)),
        ],
        out_specs=pl.BlockSpec((tm, D), lambda i, f, te, tv: (i, 0)),
        scratch_shapes=[pltpu.VMEM((tm, D), BF16), pltpu.VMEM((tm, D), F32)],
    )
    return pl.pallas_call(
        functools.partial(_ffn_kernel, n_f=n_f, dense=dense),
        grid_spec=grid_spec,
        out_shape=jax.ShapeDtypeStruct((Mp, D), F32),
        compiler_params=_params("parallel", "arbitrary"),
        name="ffn_dense" if dense else "ffn_grouped",
    )(tile_expert, tile_valid, x, g.reshape(1, D), w_gu, w_gu, w_d, row_scale)


def _router_kernel(x_ref, g_ref, wr_ref, idx_ref, gate_ref):
    xn = _rms(x_ref[...], g_ref[...], 1e-6)
    logits = jnp.dot(xn, wr_ref[...], precision=lax.Precision.HIGHEST, preferred_element_type=F32)
    lane = lax.broadcasted_iota(I32, logits.shape, 1)
    real = lane < N_EXPERTS
    logits = jnp.where(real, logits, NEG_BIG)
    e = jnp.exp(logits - jnp.max(logits, axis=-1, keepdims=True))
    probs = jnp.where(real, e / jnp.sum(e, axis=-1, keepdims=True), -1.0)
    v1 = jnp.max(probs, axis=-1, keepdims=True)
    i1 = jnp.min(jnp.where(probs == v1, lane, LANES), axis=-1, keepdims=True)
    rest = jnp.where(lane == i1, -1.0, probs)
    v2 = jnp.max(rest, axis=-1, keepdims=True)
    i2 = jnp.min(jnp.where(rest == v2, lane, LANES), axis=-1, keepdims=True)
    den = v1 + v2
    idx_ref[...] = jnp.where(lane == 0, i1, jnp.where(lane == 1, i2, 0))
    gate_ref[...] = jnp.where(lane == 0, v1 / den, jnp.where(lane == 1, v2 / den, 0.0))


def router(x, g, w_router_pad, *, tm):
    M, D = x.shape
    return pl.pallas_call(
        _router_kernel,
        grid=(M // tm,),
        in_specs=[pl.BlockSpec((tm, D), lambda i: (i, 0)),
                  pl.BlockSpec((1, D), lambda i: (0, 0)),
                  pl.BlockSpec((D, LANES), lambda i: (0, 0))],
        out_specs=[pl.BlockSpec((tm, LANES), lambda i: (i, 0)),
                   pl.BlockSpec((tm, LANES), lambda i: (i, 0))],
        out_shape=[jax.ShapeDtypeStruct((M, LANES), I32), jax.ShapeDtypeStruct((M, LANES), F32)],
        compiler_params=_params("parallel"),
        name="router",
    )(x, g.reshape(1, D), w_router_pad)


def _row_copy(src_hbm, dst_ref, src_row, dst_row, sem):
    return pltpu.make_async_copy(src_hbm.at[pl.ds(src_row, 1), :], dst_ref.at[pl.ds(dst_row, 1), :], sem)


def _gather_kernel(idx_ref, src_hbm, o_ref, sem, *, R):
    def start(r, carry):
        _row_copy(src_hbm, o_ref, idx_ref[0, r], r, sem).start()
        return carry

    def wait(r, carry):
        _row_copy(src_hbm, o_ref, 0, r, sem).wait()
        return carry

    lax.fori_loop(0, R, start, 0)
    lax.fori_loop(0, R, wait, 0)


def gather_rows(src, idx, *, R):
    Mp = idx.shape[0]
    D = src.shape[1]
    return pl.pallas_call(
        functools.partial(_gather_kernel, R=R),
        grid=(Mp // R,),
        in_specs=[pl.BlockSpec((None, 1, R), lambda i: (i, 0, 0), memory_space=pltpu.SMEM),
                  pl.BlockSpec(memory_space=pl.ANY)],
        out_specs=pl.BlockSpec((R, D), lambda i: (i, 0)),
        out_shape=jax.ShapeDtypeStruct((Mp, D), src.dtype),
        scratch_shapes=[pltpu.SemaphoreType.DMA(())],
        compiler_params=_params("arbitrary"),
        name="gather_rows",
    )(idx.reshape(Mp // R, 1, R), src)


def _combine_kernel(pos_ref, h_ref, y_hbm, g_ref, o_ref, a_sc, b_sc, sem, *, R, final_norm):
    def start(r, carry):
        _row_copy(y_hbm, a_sc, pos_ref[0, 2 * r], r, sem.at[0]).start()
        _row_copy(y_hbm, b_sc, pos_ref[0, 2 * r + 1], r, sem.at[1]).start()
        return carry

    def wait(r, carry):
        _row_copy(y_hbm, a_sc, 0, r, sem.at[0]).wait()
        _row_copy(y_hbm, b_sc, 0, r, sem.at[1]).wait()
        return carry

    lax.fori_loop(0, R, start, 0)
    lax.fori_loop(0, R, wait, 0)
    out = h_ref[...] + (a_sc[...] + b_sc[...])
    if final_norm:
        out = _rms(out, g_ref[...], 1e-6)
    o_ref[...] = out


def combine(h, y, pos, g, *, R, final_norm):
    M, D = h.shape
    return pl.pallas_call(
        functools.partial(_combine_kernel, R=R, final_norm=final_norm),
        grid=(M // R,),
        in_specs=[pl.BlockSpec((None, 1, 2 * R), lambda i: (i, 0, 0), memory_space=pltpu.SMEM),
                  pl.BlockSpec((R, D), lambda i: (i, 0)),
                  pl.BlockSpec(memory_space=pl.ANY),
                  pl.BlockSpec((1, D), lambda i: (0, 0))],
        out_specs=pl.BlockSpec((R, D), lambda i: (i, 0)),
        out_shape=jax.ShapeDtypeStruct((M, D), F32),
        scratch_shapes=[pltpu.VMEM((R, D), F32), pltpu.VMEM((R, D), F32), pltpu.SemaphoreType.DMA((2,))],
        compiler_params=_params("arbitrary"),
        name="combine",
    )(pos.reshape(M // R, 1, 2 * R), h, y, g.reshape(1, D))


def _rms_only_kernel(x_ref, g_ref, o_ref):
    o_ref[...] = _rms(x_ref[...], g_ref[...], 1e-6)


def rms_only(x, g, *, tm):
    M, D = x.shape
    return pl.pallas_call(
        _rms_only_kernel,
        grid=(M // tm,),
        in_specs=[pl.BlockSpec((tm, D), lambda i: (i, 0)), pl.BlockSpec((1, D), lambda i: (0, 0))],
        out_specs=pl.BlockSpec((tm, D), lambda i: (i, 0)),
        out_shape=jax.ShapeDtypeStruct((M, D), F32),
        compiler_params=_params("parallel"),
        name="rms_only",
    )(x, g.reshape(1, D))


def moe_ffn(h, g_norm, w_router_pad, w_gu, w_d, g_final, *, tm, R, final_norm):
    M, D = h.shape
    idx_p, gate_p = router(h, g_norm, w_router_pad, tm=min(512, M))
    expert = idx_p[:, :TOP_K].reshape(-1)
    gate = gate_p[:, :TOP_K].reshape(-1)
    n_pairs = TOP_K * M
    n_tiles = n_pairs // tm + N_EXPERTS
    Mp = n_tiles * tm
    order = jnp.argsort(expert, stable=True).astype(I32)
    counts = jnp.sum((expert[:, None] == jnp.arange(N_EXPERTS, dtype=I32)[None, :]).astype(I32), axis=0)
    tiles_per = (counts + tm - 1) // tm
    tile_end = jnp.cumsum(tiles_per)
    tile_start = tile_end - tiles_per
    group_start = jnp.cumsum(counts) - counts
    sorted_expert = expert[order]
    rank = jnp.arange(n_pairs, dtype=I32) - group_start[sorted_expert]
    dest = tile_start[sorted_expert] * tm + rank
    token_of_slot = jnp.zeros((Mp,), I32).at[dest].set(order // TOP_K)
    scale_of_slot = jnp.zeros((Mp,), F32).at[dest].set(gate[order])
    slot_of_pair = jnp.zeros((n_pairs,), I32).at[order].set(dest)
    tile_ids = jnp.arange(n_tiles, dtype=I32)
    tile_valid = (tile_ids < tile_end[-1]).astype(I32)
    tile_expert = jnp.minimum(jnp.sum((tile_ids[:, None] >= tile_end[None, :]).astype(I32), axis=1),
                              N_EXPERTS - 1)
    last_expert = tile_expert[jnp.maximum(tile_end[-1] - 1, 0)]
    tile_expert = jnp.where(tile_valid != 0, tile_expert, last_expert)

    x_sorted = gather_rows(h, token_of_slot, R=R)
    y_sorted = ffn(x_sorted, g_norm, w_gu, w_d, tile_expert, tile_valid, scale_of_slot.reshape(Mp, 1),
                   tm=tm, tf=w_d.shape[1] // 2, dense=False)
    return combine(h, y_sorted, slot_of_pair, g_final, R=R, final_norm=final_norm)


def kernel(x_prompt, x_sample, mem_prompt, cache_attn_k, cache_attn_v, cache_mem_k, cache_mem_v, state_gla,
           page_table, rel_bias, norm_mix, w_in, w_gla_a2, b_gla_a, gla_norm, diff_subln, lambda_q1, lambda_k1,
           lambda_q2, lambda_k2, w_out, norm_mem, norm_memkv, w_mq, w_mkv, w_mo, norm_ffn, w_ffn_gu,
           w_ffn_down, w_router, w_exp_gu, w_exp_down, norm_final):
    B, L, D = x_prompt.shape
    DB, LS, _ = x_sample.shape
    depth = w_in.shape[0]
    n_mem = mem_prompt.shape[1]
    n_pages, page = page_table.shape[1], cache_attn_k.shape[2]
    past_len = n_pages * page
    M, MS = B * L, DB * LS
    T_ATT = 512
    GLA_C = 64

    bias_p = bias_tiles(rel_bias, R=T_ATT, C=T_ATT, offsets=(0, T_ATT))
    bias_s = bias_tiles(rel_bias, R=2 * LS, C=past_len + page, offsets=(past_len,),
                        rows_per_head=LS).reshape(DIFF_H * 2 * LS, past_len + page)
    cache_k = cache_attn_k.reshape(depth, -1, page, DIFF_W)
    cache_v = cache_attn_v.reshape(depth, -1, page, DIFF_W)
    w_router_pad = jnp.pad(w_router, ((0, 0), (0, 0), (0, LANES - N_EXPERTS)))

    hp = x_prompt.reshape(M, D)
    hs = x_sample.reshape(MS, D)
    pk, pv, ps, pmk, pmv, sk, sv, ss = [], [], [], [], [], [], [], []
    for l in range(depth):
        lambda_init = 0.8 - 0.6 * math.exp(-0.3 * l)
        w_in_p = jnp.pad(w_in[l], ((0, 0), (0, N_IN_PAD - N_IN))).astype(BF16)
        w_a2p = jnp.pad(w_gla_a2[l], ((0, LANES - GLA_LR), (0, 0)))
        lamp = jnp.stack([lambda_q1[l], lambda_k1[l], lambda_q2[l], lambda_k2[l]])
        w_out_a = w_out[l, :DIFF_W].astype(BF16)
        w_out_g = w_out[l, DIFF_W:].astype(BF16)
        w_mq_b, w_mo_b, w_mkv_b = w_mq[l].astype(BF16), w_mo[l].astype(BF16), w_mkv[l].astype(BF16)
        last = l == depth - 1

        proj_p = rms_matmul(hp, norm_mix[l], w_in_p, tm=512).reshape(B, L, N_IN_PAD)
        proj_s = rms_matmul(hs, norm_mix[l], w_in_p, tm=512).reshape(DB, LS, N_IN_PAD)
        att_p = diff_attention_prompt(proj_p, bias_p, lamp, diff_subln[l], T=T_ATT, lambda_init=lambda_init)
        att_s = diff_attention_decode(proj_s, cache_k, cache_v, page_table, bias_s, lamp, diff_subln[l],
                                      layer=l, lambda_init=lambda_init)
        zero_state = jnp.zeros((B, GLA_H, GLA_DV, GLA_DK), F32)
        gla_p, st_p = gla(proj_p, w_a2p, b_gla_a[l], gla_norm[l], zero_state, C=GLA_C, SB=16)
        gla_s, st_s = gla(proj_s, w_a2p, b_gla_a[l], gla_norm[l], jnp.swapaxes(state_gla[l], -1, -2),
                          C=LS, SB=LS)
        hp = mm_res([att_p.reshape(M, DIFF_W), gla_p.reshape(M, GLA_W)], [w_out_a, w_out_g], hp, tm=512)
        hs = mm_res([att_s.reshape(MS, DIFF_W), gla_s.reshape(MS, GLA_W)], [w_out_a, w_out_g], hs, tm=512)

        mkv = rms_matmul(mem_prompt.reshape(B * n_mem, D), norm_memkv[l], w_mkv_b, tm=512)
        mk_p = mkv[:, :D].reshape(B, n_mem, D)
        mv_p = mkv[:, D:].reshape(B, n_mem, D)
        q_p = rms_matmul(hp, norm_mem[l], w_mq_b, tm=512, out_dtype=BF16, scale=MEM_DH ** -0.5)
        q_s = rms_matmul(hs, norm_mem[l], w_mq_b, tm=512, out_dtype=BF16, scale=MEM_DH ** -0.5)
        xo_p = xattn_core(q_p.reshape(B, L, D), mk_p, mv_p, tm=512)
        xo_s = xattn_core(q_s.reshape(DB, LS, D), cache_mem_k[l].reshape(DB, n_mem, D),
                          cache_mem_v[l].reshape(DB, n_mem, D), tm=LS)
        hp = mm_res([xo_p.reshape(M, D)], [w_mo_b], hp, tm=512)
        hs = mm_res([xo_s.reshape(MS, D)], [w_mo_b], hs, tm=512)

        if l % 2 == 0:
            w_gu = w_ffn_gu[l // 2].astype(BF16)[None]
            w_d = w_ffn_down[l // 2].astype(BF16)[None]
            for_dense = lambda h, tm: ffn(
                h, norm_ffn[l], w_gu, w_d, jnp.zeros((h.shape[0] // tm,), I32),
                jnp.ones((h.shape[0] // tm,), I32), jnp.ones((h.shape[0], 1), F32),
                tm=tm, tf=w_d.shape[1] // 2, dense=True)
            hp, hs = for_dense(hp, 512), for_dense(hs, 512)
            if last:
                hp, hs = rms_only(hp, norm_final, tm=512), rms_only(hs, norm_final, tm=512)
        else:
            w_gu = w_exp_gu[l // 2].astype(BF16)
            w_d = w_exp_down[l // 2].astype(BF16)
            hp = moe_ffn(hp, norm_ffn[l], w_router_pad[l // 2], w_gu, w_d, norm_final,
                         tm=512, R=256, final_norm=last)
            hs = moe_ffn(hs, norm_ffn[l], w_router_pad[l // 2], w_gu, w_d, norm_final,
                         tm=256, R=256, final_norm=last)

        pk.append(proj_p[:, :, DIFF_W:2 * DIFF_W].reshape(B, L, DIFF_H, DIFF_DV))
        pv.append(proj_p[:, :, 2 * DIFF_W:3 * DIFF_W].reshape(B, L, DIFF_H, DIFF_DV))
        ps.append(jnp.swapaxes(st_p, -1, -2))
        pmk.append(mk_p.reshape(B, n_mem, MEM_H, MEM_DH))
        pmv.append(mv_p.reshape(B, n_mem, MEM_H, MEM_DH))
        sk.append(proj_s[:, :, DIFF_W:2 * DIFF_W].reshape(DB, LS, DIFF_H, DIFF_DV))
        sv.append(proj_s[:, :, 2 * DIFF_W:3 * DIFF_W].reshape(DB, LS, DIFF_H, DIFF_DV))
        ss.append(jnp.swapaxes(st_s, -1, -2))

    return (hp.reshape(B, L, D), hs.reshape(DB, LS, D), jnp.stack(pk), jnp.stack(pv), jnp.stack(ps),
            jnp.stack(pmk), jnp.stack(pmv), jnp.stack(sk), jnp.stack(sv), jnp.stack(ss))
```

```python
import functools
import math

import numpy as np
import jax
import jax.numpy as jnp
from jax import lax
from jax.experimental import pallas as pl
from jax.experimental.pallas import tpu as pltpu

F32 = jnp.float32
BF16 = jnp.bfloat16
I32 = jnp.int32

LANES = 128
SUBLANES = 8
VMEM_LIMIT_BYTES = 56 * 1024 * 1024

D_MODEL = 1024
DIFF_H = 4
DIFF_DV = 128
DIFF_DH = 64
DIFF_W = DIFF_H * DIFF_DV
GLA_H = 4
GLA_DK = 64
GLA_DV = 128
GLA_W = GLA_H * GLA_DV
GLA_LR = 16
GATE_NORM = 16.0
N_BUCKETS = 32
MAX_DISTANCE = 128
MEM_H = 4
MEM_DH = 256
N_EXPERTS = 8
TOP_K = 2
N_IN = 3 * DIFF_W + 2 * GLA_H * GLA_DK + 2 * GLA_W + GLA_LR
N_IN_PAD = 3200
NEG_BIG = -1e30
LOG2E = math.log2(math.e)

COL_Q = 0
COL_K = 4
COL_V = 8
COL_QK_GLA = 3
COL_V_GLA = 4
COL_G_GLA = 5
COL_ALR = 24

NT_DIMS = (((1,), (1,)), ((), ()))
TN_DIMS = (((0,), (0,)), ((), ()))


def _params(*sem):
    return pltpu.CompilerParams(dimension_semantics=sem, vmem_limit_bytes=VMEM_LIMIT_BYTES)


def _rms(x, g, eps):
    ms = jnp.mean(x * x, axis=-1, keepdims=True)
    return (x * lax.rsqrt(ms + eps)) * g


def _silu(x):
    return x / (1.0 + jnp.exp(-x))


def _rms_matmul_kernel(x_ref, g_ref, w_ref, o_ref, *, eps, scale):
    xn = _rms(x_ref[...], g_ref[...], eps).astype(BF16)
    y = jnp.dot(xn, w_ref[...], preferred_element_type=F32)
    if scale != 1.0:
        y = y * scale
    o_ref[...] = y.astype(o_ref.dtype)


def rms_matmul(x, g, w, *, tm, out_dtype=F32, scale=1.0, eps=1e-6):
    M, K = x.shape
    N = w.shape[1]
    return pl.pallas_call(
        functools.partial(_rms_matmul_kernel, eps=eps, scale=scale),
        grid=(M // tm,),
        in_specs=[pl.BlockSpec((tm, K), lambda i: (i, 0)),
                  pl.BlockSpec((1, K), lambda i: (0, 0)),
                  pl.BlockSpec((K, N), lambda i: (0, 0))],
        out_specs=pl.BlockSpec((tm, N), lambda i: (i, 0)),
        out_shape=jax.ShapeDtypeStruct((M, N), out_dtype),
        compiler_params=_params("parallel"),
        name="rms_matmul",
    )(x, g.reshape(1, K), w)


def _rms_matmul_heads_kernel(x_ref, g_ref, w_ref, *o_refs, eps, keep_full, head_outs):
    xn = _rms(x_ref[...], g_ref[...], eps).astype(BF16)
    y = jnp.dot(xn, w_ref[...], preferred_element_type=F32)
    if keep_full:
        o_refs[0][...] = y
    for o_ref, (col0, n_heads, width) in zip(o_refs[1 if keep_full else 0:], head_outs):
        for h in range(n_heads):
            o_ref[:, h, :] = y[:, col0 + h * width:col0 + (h + 1) * width]


def rms_matmul_heads(x, g, w, *, tm, keep_full, head_outs, eps=1e-6):
    M, K = x.shape
    N = w.shape[1]
    out_specs = [pl.BlockSpec((tm, nh, wd), lambda i: (i, 0, 0)) for _, nh, wd in head_outs]
    out_shape = [jax.ShapeDtypeStruct((M, nh, wd), F32) for _, nh, wd in head_outs]
    if keep_full:
        out_specs.insert(0, pl.BlockSpec((tm, N), lambda i: (i, 0)))
        out_shape.insert(0, jax.ShapeDtypeStruct((M, N), F32))
    return pl.pallas_call(
        functools.partial(_rms_matmul_heads_kernel, eps=eps, keep_full=keep_full, head_outs=tuple(head_outs)),
        grid=(M // tm,),
        in_specs=[pl.BlockSpec((tm, K), lambda i: (i, 0)),
                  pl.BlockSpec((1, K), lambda i: (0, 0)),
                  pl.BlockSpec((K, N), lambda i: (0, 0))],
        out_specs=out_specs,
        out_shape=out_shape,
        compiler_params=_params("parallel"),
        name="rms_matmul_heads",
    )(x, g.reshape(1, K), w)


def _mm_res_kernel(*refs, n_lhs):
    a_refs, w_refs = refs[:n_lhs], refs[n_lhs:2 * n_lhs]
    res_ref, o_ref = refs[2 * n_lhs], refs[2 * n_lhs + 1]
    acc = res_ref[...]
    for a_ref, w_ref in zip(a_refs, w_refs):
        acc = acc + jnp.dot(a_ref[...], w_ref[...], preferred_element_type=F32)
    o_ref[...] = acc


def mm_res(lhs, ws, res, *, tm):
    M, N = res.shape
    n = len(lhs)
    in_specs = ([pl.BlockSpec((tm, a.shape[1]), lambda i: (i, 0)) for a in lhs]
                + [pl.BlockSpec(w.shape, lambda i: (0, 0)) for w in ws]
                + [pl.BlockSpec((tm, N), lambda i: (i, 0))])
    return pl.pallas_call(
        functools.partial(_mm_res_kernel, n_lhs=n),
        grid=(M // tm,),
        in_specs=in_specs,
        out_specs=pl.BlockSpec((tm, N), lambda i: (i, 0)),
        out_shape=jax.ShapeDtypeStruct((M, N), F32),
        compiler_params=_params("parallel"),
        name="mm_res",
    )(*lhs, *ws, res)


def _bias_kernel(tab_ref, o_ref, *, offsets, rows_per_head):
    h = pl.program_id(0)
    R, C = o_ref.shape[-2], o_ref.shape[-1]
    r = lax.broadcasted_iota(I32, (R, C), 0)
    if rows_per_head != R:
        r = r % rows_per_head
    c = lax.broadcasted_iota(I32, (R, C), 1)
    max_exact = N_BUCKETS // 2
    far = tab_ref[N_BUCKETS - 1, h]
    for kind, off in enumerate(offsets):
        rel = off + r - c
        n = jnp.maximum(rel, 0)
        nf = jnp.maximum(n, max_exact).astype(F32)
        large = max_exact + (jnp.log(nf / max_exact) / math.log(MAX_DISTANCE / max_exact)
                             * (N_BUCKETS - max_exact)).astype(I32)
        bucket = jnp.where(n < max_exact, n, jnp.minimum(large, N_BUCKETS - 1))
        acc = jnp.zeros((R, C), F32)
        for b in range(N_BUCKETS - 1):
            acc = jnp.where(bucket == b, (tab_ref[b, h] - far) * LOG2E, acc)
        o_ref[kind] = jnp.where(rel >= 0, acc, NEG_BIG)


def bias_tiles(rel_bias, *, R, C, offsets, rows_per_head=None):
    rows_per_head = R if rows_per_head is None else rows_per_head
    return pl.pallas_call(
        functools.partial(_bias_kernel, offsets=tuple(offsets), rows_per_head=rows_per_head),
        grid=(DIFF_H,),
        in_specs=[pl.BlockSpec(memory_space=pltpu.SMEM)],
        out_specs=pl.BlockSpec((None, len(offsets), R, C), lambda h: (h, 0, 0, 0)),
        out_shape=jax.ShapeDtypeStruct((DIFF_H, len(offsets), R, C), F32),
        compiler_params=_params("arbitrary"),
        name="bias_tiles",
    )(rel_bias)


def _lambda_value(lamp, lambda_init):
    s1 = jnp.sum(lamp[0:1, :] * lamp[1:2, :], axis=-1, keepdims=True)
    s2 = jnp.sum(lamp[2:3, :] * lamp[3:4, :], axis=-1, keepdims=True)
    return jnp.exp(s1) - jnp.exp(s2) + lambda_init


def _diff_finish(o1, o2, lam, subln, lambda_init):
    o = o1 - lam * o2
    return _rms(o, subln, 1e-5) * (1.0 - lambda_init)


def _diff_attn_kernel(qi_tab, ki_tab, q_ref, k_ref, v_ref, bias_ref, lamp_ref, subln_ref, o_ref,
                      q2_sc, m_sc, acc_sc, *, T, RB, lambda_init):
    t = pl.program_id(2)
    qi = qi_tab[t]
    ki = ki_tab[t]

    @pl.when(ki == 0)
    def _init():
        q = q_ref[...] * (DIFF_DH ** -0.5 * LOG2E)
        lane = lax.broadcasted_iota(I32, q.shape, 1)
        q2_sc[0:T, :] = jnp.where(lane < DIFF_DH, q, 0.0).astype(BF16)
        q2_sc[T:2 * T, :] = jnp.where(lane >= DIFF_DH, q, 0.0).astype(BF16)
        m_sc[...] = jnp.full(m_sc.shape, NEG_BIG, F32)
        acc_sc[...] = jnp.zeros(acc_sc.shape, F32)

    def update(kind):
        kb = k_ref[...].astype(BF16)
        vb = jnp.concatenate([v_ref[...].astype(BF16), jnp.ones((T, DIFF_DV), BF16)], axis=1)
        for r0 in range(0, 2 * T, RB):
            rows = slice(r0, r0 + RB)
            s = lax.dot_general(q2_sc[rows, :], kb, NT_DIMS, preferred_element_type=F32)
            if kind is not None:
                s = s + bias_ref[kind, r0 % T:r0 % T + RB, :]
            cols = [s[:, c * LANES:(c + 1) * LANES] for c in range(T // LANES)]
            m_old = m_sc[rows, :]
            m_new = jnp.maximum(m_old, jnp.max(functools.reduce(jnp.maximum, cols), axis=-1, keepdims=True))
            alpha = jnp.exp2(m_old - m_new)
            p = jnp.concatenate([jnp.exp2(c - m_new) for c in cols], axis=1).astype(BF16)
            acc_sc[rows, :] = (jnp.concatenate([alpha, alpha], axis=1) * acc_sc[rows, :]
                               + jnp.dot(p, vb, preferred_element_type=F32))
            m_sc[rows, :] = m_new

    @pl.when(ki < qi - 1)
    def _far():
        update(None)

    @pl.when(ki == qi - 1)
    def _near():
        update(1)

    @pl.when(ki == qi)
    def _diag():
        update(0)
        on = acc_sc[:, 0:DIFF_DV] / acc_sc[:, DIFF_DV:2 * DIFF_DV]
        lam = _lambda_value(lamp_ref[...], lambda_init)
        o_ref[...] = _diff_finish(on[0:T], on[T:2 * T], lam, subln_ref[...], lambda_init).astype(o_ref.dtype)


def diff_attention_prompt(proj, bias, lamp, subln, *, T, lambda_init):
    B, L, _ = proj.shape
    nq = L // T
    pairs = [(qi, ki) for qi in range(nq) for ki in range(qi + 1)]
    qi_tab = jnp.asarray(np.array([p[0] for p in pairs], np.int32))
    ki_tab = jnp.asarray(np.array([p[1] for p in pairs], np.int32))
    grid_spec = pltpu.PrefetchScalarGridSpec(
        num_scalar_prefetch=2,
        grid=(B, DIFF_H, len(pairs)),
        in_specs=[
            pl.BlockSpec((None, T, DIFF_DV), lambda b, h, t, qt, kt: (b, qt[t], COL_Q + h)),
            pl.BlockSpec((None, T, DIFF_DV), lambda b, h, t, qt, kt: (b, kt[t], COL_K + h)),
            pl.BlockSpec((None, T, DIFF_DV), lambda b, h, t, qt, kt: (b, kt[t], COL_V + h)),
            pl.BlockSpec((None, 2, T, T), lambda b, h, t, qt, kt: (h, 0, 0, 0)),
            pl.BlockSpec((4, DIFF_DH), lambda b, h, t, qt, kt: (0, 0)),
            pl.BlockSpec((1, DIFF_DV), lambda b, h, t, qt, kt: (0, 0)),
        ],
        out_specs=pl.BlockSpec((None, T, DIFF_DV), lambda b, h, t, qt, kt: (b, qt[t], h)),
        scratch_shapes=[pltpu.VMEM((2 * T, DIFF_DV), BF16),
                        pltpu.VMEM((2 * T, LANES), F32),
                        pltpu.VMEM((2 * T, 2 * DIFF_DV), F32)],
    )
    return pl.pallas_call(
        functools.partial(_diff_attn_kernel, T=T, RB=min(T, 256), lambda_init=lambda_init),
        grid_spec=grid_spec,
        out_shape=jax.ShapeDtypeStruct((B, L, DIFF_W), BF16),
        compiler_params=_params("parallel", "parallel", "arbitrary"),
        name="diff_attn_prompt",
    )(qi_tab, ki_tab, proj, proj, proj, bias, lamp, subln.reshape(1, DIFF_DV))


def _diff_decode_kernel(pt_ref, proj_ref, bias_ref, lamp_ref, subln_ref, *rest, n_pages, page, lq, lambda_init):
    k_refs, v_refs = rest[:n_pages], rest[n_pages:2 * n_pages]
    o_ref = rest[2 * n_pages]
    lam = _lambda_value(lamp_ref[...], lambda_init)
    lane = lax.broadcasted_iota(I32, (lq, DIFF_DV), 1)
    pad = jnp.zeros((page - lq, DIFF_DV), F32)
    for h in range(DIFF_H):
        cols = slice(h * DIFF_DV, (h + 1) * DIFF_DV)
        q = proj_ref[:, cols] * (DIFF_DH ** -0.5 * LOG2E)
        qbd = jnp.concatenate([jnp.where(lane < DIFF_DH, q, 0.0), jnp.where(lane >= DIFF_DH, q, 0.0)],
                              axis=0).astype(BF16)
        k_tail = jnp.concatenate([proj_ref[:, DIFF_W + h * DIFF_DV:DIFF_W + (h + 1) * DIFF_DV], pad], 0)
        v_tail = jnp.concatenate([proj_ref[:, 2 * DIFF_W + h * DIFF_DV:2 * DIFF_W + (h + 1) * DIFF_DV], pad], 0)
        head_rows = pl.ds(h, page, stride=DIFF_H)
        parts = [lax.dot_general(qbd, k_refs[j][head_rows, :].astype(BF16), NT_DIMS, preferred_element_type=F32)
                 for j in range(n_pages)]
        parts.append(lax.dot_general(qbd, k_tail.astype(BF16), NT_DIMS, preferred_element_type=F32))
        s = jnp.concatenate(parts, axis=1) + bias_ref[h]
        p = jnp.exp2(s - jnp.max(s, axis=-1, keepdims=True))
        l = jnp.sum(p, axis=-1, keepdims=True)
        pb = p.astype(BF16)
        acc = jnp.dot(pb[:, n_pages * page:], v_tail.astype(BF16), preferred_element_type=F32)
        for j in range(n_pages):
            acc = acc + jnp.dot(pb[:, j * page:(j + 1) * page], v_refs[j][head_rows, :].astype(BF16),
                                preferred_element_type=F32)
        on = acc / l
        o = _diff_finish(on[0:lq], on[lq:2 * lq], lam, subln_ref[...], lambda_init)
        o_ref[:, cols] = o.astype(o_ref.dtype)


def diff_attention_decode(proj, cache_k, cache_v, page_table, bias, lamp, subln, *, layer, lambda_init):
    DB, lq, _ = proj.shape
    n_pages = page_table.shape[1]
    page = cache_k.shape[2] // DIFF_H

    def page_spec(j):
        return pl.BlockSpec((None, None, page * DIFF_H, DIFF_DV), lambda b, pt: (layer, pt[b, j], 0, 0))

    grid_spec = pltpu.PrefetchScalarGridSpec(
        num_scalar_prefetch=1,
        grid=(DB,),
        in_specs=([pl.BlockSpec((None, lq, 3 * DIFF_W), lambda b, pt: (b, 0, 0)),
                   pl.BlockSpec(bias.shape, lambda b, pt: (0, 0, 0)),
                   pl.BlockSpec((4, DIFF_DH), lambda b, pt: (0, 0)),
                   pl.BlockSpec((1, DIFF_DV), lambda b, pt: (0, 0))]
                  + [page_spec(j) for j in range(n_pages)]
                  + [page_spec(j) for j in range(n_pages)]),
        out_specs=pl.BlockSpec((None, lq, DIFF_W), lambda b, pt: (b, 0, 0)),
    )
    return pl.pallas_call(
        functools.partial(_diff_decode_kernel, n_pages=n_pages, page=page, lq=lq, lambda_init=lambda_init),
        grid_spec=grid_spec,
        out_shape=jax.ShapeDtypeStruct((DB, lq, DIFF_W), BF16),
        compiler_params=_params("parallel"),
        name="diff_attn_decode",
    )(page_table, proj, bias, lamp, subln.reshape(1, DIFF_DV),
      *([cache_k] * n_pages), *([cache_v] * n_pages))


def _gla_kernel(qk_ref, v_ref, gg_ref, alr_ref, wa_ref, ba_ref, gn_ref, ex_ref, s0_ref, o_ref, sout_ref, st_sc,
                *, C, SB, n_chunks):
    c = pl.program_id(1)

    @pl.when(c == 0)
    def _init():
        st_sc[...] = s0_ref[...]

    z = jnp.dot(alr_ref[...], wa_ref[...], precision=lax.Precision.HIGHEST,
                preferred_element_type=F32) + ba_ref[...]
    logg = (jnp.minimum(z, 0.0) - jnp.log1p(jnp.exp(-jnp.abs(z)))) * (1.0 / GATE_NORM)
    row = lax.broadcasted_iota(I32, logg.shape, 0)
    b = logg
    d = 1
    while d < C:
        b = b + jnp.where(row >= d, pltpu.roll(b, d, 0), 0.0)
        d *= 2
    q_all = qk_ref[:, 0:GLA_H * GLA_DK] * (GLA_DK ** -0.5)
    k_all = qk_ref[:, GLA_H * GLA_DK:2 * GLA_H * GLA_DK]
    v_all = v_ref[...]
    n_sub = C // SB
    hk = lambda a, h: a[:, h * GLA_DK:(h + 1) * GLA_DK]
    hv = lambda a, h: a[:, h * GLA_DV:(h + 1) * GLA_DV]

    b_last = b[C - 1:C, :]
    q_in = (q_all * jnp.exp(b)).astype(BF16)
    k_dec = (k_all * jnp.exp(b_last - b)).astype(BF16)
    e_last = jnp.exp(b_last)
    q_off, k_off = [], []
    for i in range(1, n_sub):
        ref = b[i * SB:i * SB + 1, :]
        q_off.append((q_all[i * SB:(i + 1) * SB] * jnp.exp(b[i * SB:(i + 1) * SB] - ref)).astype(BF16))
        k_off.append((k_all * jnp.exp(jnp.minimum(ref - b, 0.0))).astype(BF16))
    col = lax.broadcasted_iota(I32, (SB, C), 1)

    outs = []
    for h in range(GLA_H):
        vb = hv(v_all, h).astype(BF16)
        st = st_sc[h]
        o = lax.dot_general(hk(q_in, h), st.astype(BF16), NT_DIMS, preferred_element_type=F32)
        if n_sub > 1:
            rows = [jnp.zeros((SB, C), F32)]
            for i in range(1, n_sub):
                a = lax.dot_general(hk(q_off[i - 1], h), hk(k_off[i - 1], h), NT_DIMS,
                                    preferred_element_type=F32)
                rows.append(jnp.where(col < i * SB, a, 0.0))
            a_off = jnp.concatenate(rows, axis=0)
            o = o + jnp.dot(a_off.astype(BF16), vb, preferred_element_type=F32)
        st_sc[h] = st * hk(e_last, h) + lax.dot_general(vb, hk(k_dec, h), TN_DIMS, preferred_element_type=F32)
        outs.append(o)

    rmod = row % SB
    xs = []
    for dlt in range(SB):
        kd, bd = (k_all, b) if dlt == 0 else (pltpu.roll(k_all, dlt, 0), pltpu.roll(b, dlt, 0))
        x = q_all * kd * jnp.exp(jnp.minimum(b - bd, 0.0))
        xs.append(jnp.where(rmod >= dlt, x, 0.0))
    w = jnp.dot(jnp.concatenate(xs, axis=0).astype(BF16), ex_ref[...], preferred_element_type=F32)
    o_diag = w[0:C] * v_all
    for dlt in range(1, SB):
        o_diag = o_diag + w[dlt * C:(dlt + 1) * C] * pltpu.roll(v_all, dlt, 0)

    for h in range(GLA_H):
        gate = _silu(hv(gg_ref[...], h))
        o = outs[h] + hv(o_diag, h)
        o_ref[:, h * GLA_DV:(h + 1) * GLA_DV] = (_rms(o, gn_ref[...], 1e-5) * gate).astype(o_ref.dtype)

    @pl.when(c == n_chunks - 1)
    def _fin():
        sout_ref[...] = st_sc[...]


def gla(proj, w_a2p, b_a, gla_g, s0t, *, C, SB):
    B, L, _ = proj.shape
    n_chunks = L // C
    W = GLA_W
    expand = (jnp.arange(GLA_H * GLA_DK, dtype=I32)[:, None] // GLA_DK
              == jnp.arange(W, dtype=I32)[None, :] // GLA_DV).astype(BF16)
    return pl.pallas_call(
        functools.partial(_gla_kernel, C=C, SB=SB, n_chunks=n_chunks),
        grid=(B, n_chunks),
        in_specs=[
            pl.BlockSpec((None, C, W), lambda b, c: (b, c, COL_QK_GLA)),
            pl.BlockSpec((None, C, W), lambda b, c: (b, c, COL_V_GLA)),
            pl.BlockSpec((None, C, W), lambda b, c: (b, c, COL_G_GLA)),
            pl.BlockSpec((None, C, LANES), lambda b, c: (b, c, COL_ALR)),
            pl.BlockSpec((LANES, GLA_H * GLA_DK), lambda b, c: (0, 0)),
            pl.BlockSpec((1, GLA_H * GLA_DK), lambda b, c: (0, 0)),
            pl.BlockSpec((1, GLA_DV), lambda b, c: (0, 0)),
            pl.BlockSpec((GLA_H * GLA_DK, W), lambda b, c: (0, 0)),
            pl.BlockSpec((None, GLA_H, GLA_DV, GLA_DK), lambda b, c: (b, 0, 0, 0)),
        ],
        out_specs=[pl.BlockSpec((None, C, W), lambda b, c: (b, c, 0)),
                   pl.BlockSpec((None, GLA_H, GLA_DV, GLA_DK), lambda b, c: (b, 0, 0, 0))],
        out_shape=[jax.ShapeDtypeStruct((B, L, W), BF16),
                   jax.ShapeDtypeStruct((B, GLA_H, GLA_DV, GLA_DK), F32)],
        scratch_shapes=[pltpu.VMEM((GLA_H, GLA_DV, GLA_DK), F32)],
        compiler_params=_params("parallel", "arbitrary"),
        name="gla",
    )(proj, proj, proj, proj, w_a2p, b_a.reshape(1, -1), gla_g.reshape(1, -1), expand, s0t)


def _xattn_kernel(q_ref, mk_ref, mv_ref, o_ref):
    for h in range(MEM_H):
        cols = slice(h * MEM_DH, (h + 1) * MEM_DH)
        s = lax.dot_general(q_ref[:, cols], mk_ref[:, h, :].astype(BF16), NT_DIMS, preferred_element_type=F32)
        m = jnp.max(s, axis=-1, keepdims=True)
        p = jnp.exp(s - m)
        l = jnp.sum(p, axis=-1, keepdims=True)
        o = jnp.dot(p.astype(BF16), mv_ref[:, h, :].astype(BF16), preferred_element_type=F32) / l
        o_ref[:, cols] = o.astype(o_ref.dtype)


def xattn_core(q, mk, mv, *, layer, tm):
    Bx, Lx, W = q.shape
    n_mem = mk.shape[2]
    mem_spec = pl.BlockSpec((None, None, n_mem, MEM_H, MEM_DH), lambda b, i: (layer, b, 0, 0, 0))
    return pl.pallas_call(
        _xattn_kernel,
        grid=(Bx, Lx // tm),
        in_specs=[pl.BlockSpec((None, tm, W), lambda b, i: (b, i, 0)), mem_spec, mem_spec],
        out_specs=pl.BlockSpec((None, tm, W), lambda b, i: (b, i, 0)),
        out_shape=jax.ShapeDtypeStruct((Bx, Lx, W), BF16),
        compiler_params=_params("parallel", "arbitrary"),
        name="xattn_core",
    )(q, mk, mv)


def _ffn_kernel(te_ref, tv_ref, x_ref, g_ref, wg_ref, wu_ref, wd_ref, sc_ref, o_ref, xn_sc, acc_sc,
                *, n_f, dense):
    i = pl.program_id(0)
    f = pl.program_id(1)

    @pl.when(f == 0)
    def _init():
        xn_sc[...] = _rms(x_ref[...], g_ref[...], 1e-6).astype(BF16)
        acc_sc[...] = jnp.zeros(acc_sc.shape, F32)

    @pl.when(tv_ref[i] != 0)
    def _compute():
        xn = xn_sc[...]
        g = jnp.dot(xn, wg_ref[...], preferred_element_type=F32)
        u = jnp.dot(xn, wu_ref[...], preferred_element_type=F32)
        a = (_silu(g) * u).astype(BF16)
        acc_sc[...] += jnp.dot(a, wd_ref[...], preferred_element_type=F32)

    @pl.when(f == n_f - 1)
    def _fin():
        if dense:
            o_ref[...] = x_ref[...] + acc_sc[...]
        else:
            o_ref[...] = acc_sc[...] * sc_ref[...]


def ffn(x, g, w_gu, w_d, tile_expert, tile_valid, row_scale, *, tm, tf, dense):
    Mp, D = x.shape
    F = w_d.shape[1]
    n_f = F // tf
    last = n_f - 1

    def fblk(i, f, tv):
        return f * tv[i] + last * (1 - tv[i])

    grid_spec = pltpu.PrefetchScalarGridSpec(
        num_scalar_prefetch=2,
        grid=(Mp // tm, n_f),
        in_specs=[
            pl.BlockSpec((tm, D), lambda i, f, te, tv: (i, 0)),
            pl.BlockSpec((1, D), lambda i, f, te, tv: (0, 0)),
            pl.BlockSpec((None, D, tf), lambda i, f, te, tv: (te[i], 0, fblk(i, f, tv))),
            pl.BlockSpec((None, D, tf), lambda i, f, te, tv: (te[i], 0, n_f + fblk(i, f, tv))),
            pl.BlockSpec((None, tf, D), lambda i, f, te, tv: (te[i], fblk(i, f, tv), 0)),
            pl.BlockSpec((tm, 1), lambda i, f, te, tv: (i, 0)),
        ],
        out_specs=pl.BlockSpec((tm, D), lambda i, f, te, tv: (i, 0)),
        scratch_shapes=[pltpu.VMEM((tm, D), BF16), pltpu.VMEM((tm, D), F32)],
    )
    return pl.pallas_call(
        functools.partial(_ffn_kernel, n_f=n_f, dense=dense),
        grid_spec=grid_spec,
        out_shape=jax.ShapeDtypeStruct((Mp, D), F32),
        compiler_params=_params("parallel", "arbitrary"),
        name="ffn_dense" if dense else "ffn_grouped",
    )(tile_expert, tile_valid, x, g.reshape(1, D), w_gu, w_gu, w_d, row_scale)


def _router_kernel(x_ref, g_ref, wr_ref, idx_ref, gate_ref):
    xn = _rms(x_ref[...], g_ref[...], 1e-6)
    logits = jnp.dot(xn, wr_ref[...], precision=lax.Precision.HIGHEST, preferred_element_type=F32)
    lane = lax.broadcasted_iota(I32, logits.shape, 1)
    real = lane < N_EXPERTS
    logits = jnp.where(real, logits, NEG_BIG)
    e = jnp.exp(logits - jnp.max(logits, axis=-1, keepdims=True))
    probs = jnp.where(real, e / jnp.sum(e, axis=-1, keepdims=True), -1.0)
    v1 = jnp.max(probs, axis=-1, keepdims=True)
    i1 = jnp.min(jnp.where(probs == v1, lane, LANES), axis=-1, keepdims=True)
    rest = jnp.where(lane == i1, -1.0, probs)
    v2 = jnp.max(rest, axis=-1, keepdims=True)
    i2 = jnp.min(jnp.where(rest == v2, lane, LANES), axis=-1, keepdims=True)
    den = v1 + v2
    idx_ref[...] = jnp.where(lane == 0, i1, jnp.where(lane == 1, i2, 0))
    gate_ref[...] = jnp.where(lane == 0, v1 / den, jnp.where(lane == 1, v2 / den, 0.0))


def router(x, g, w_router_pad, *, tm):
    M, D = x.shape
    return pl.pallas_call(
        _router_kernel,
        grid=(M // tm,),
        in_specs=[pl.BlockSpec((tm, D), lambda i: (i, 0)),
                  pl.BlockSpec((1, D), lambda i: (0, 0)),
                  pl.BlockSpec((D, LANES), lambda i: (0, 0))],
        out_specs=[pl.BlockSpec((tm, LANES), lambda i: (i, 0)),
                   pl.BlockSpec((tm, LANES), lambda i: (i, 0))],
        out_shape=[jax.ShapeDtypeStruct((M, LANES), I32), jax.ShapeDtypeStruct((M, LANES), F32)],
        compiler_params=_params("parallel"),
        name="router",
    )(x, g.reshape(1, D), w_router_pad)


def _row_copy(src_hbm, dst_ref, src_row, dst_row, sem):
    return pltpu.make_async_copy(src_hbm.at[pl.ds(src_row, 1), :], dst_ref.at[pl.ds(dst_row, 1), :], sem)


def _gather_kernel(idx_ref, src_hbm, o_ref, sem, *, R):
    def start(r, carry):
        _row_copy(src_hbm, o_ref, idx_ref[0, r], r, sem).start()
        return carry

    lax.fori_loop(0, R, start, 0, unroll=8)
    pltpu.make_async_copy(src_hbm.at[pl.ds(0, R), :], o_ref, sem).wait()


def gather_rows(src, idx, *, R):
    Mp = idx.shape[0]
    D = src.shape[1]
    return pl.pallas_call(
        functools.partial(_gather_kernel, R=R),
        grid=(Mp // R,),
        in_specs=[pl.BlockSpec((None, 1, R), lambda i: (i, 0, 0), memory_space=pltpu.SMEM),
                  pl.BlockSpec(memory_space=pl.ANY)],
        out_specs=pl.BlockSpec((R, D), lambda i: (i, 0)),
        out_shape=jax.ShapeDtypeStruct((Mp, D), src.dtype),
        scratch_shapes=[pltpu.SemaphoreType.DMA(())],
        compiler_params=_params("arbitrary"),
        name="gather_rows",
    )(idx.reshape(Mp // R, 1, R), src)


def _combine_kernel(pos_ref, h_ref, y_hbm, g_ref, o_ref, a_sc, b_sc, sem, *, R, final_norm):
    def start(r, carry):
        _row_copy(y_hbm, a_sc, pos_ref[0, 2 * r], r, sem.at[0]).start()
        _row_copy(y_hbm, b_sc, pos_ref[0, 2 * r + 1], r, sem.at[1]).start()
        return carry

    lax.fori_loop(0, R, start, 0, unroll=8)
    pltpu.make_async_copy(y_hbm.at[pl.ds(0, R), :], a_sc, sem.at[0]).wait()
    pltpu.make_async_copy(y_hbm.at[pl.ds(0, R), :], b_sc, sem.at[1]).wait()
    out = h_ref[...] + (a_sc[...] + b_sc[...])
    if final_norm:
        out = _rms(out, g_ref[...], 1e-6)
    o_ref[...] = out


def combine(h, y, pos, g, *, R, final_norm):
    M, D = h.shape
    return pl.pallas_call(
        functools.partial(_combine_kernel, R=R, final_norm=final_norm),
        grid=(M // R,),
        in_specs=[pl.BlockSpec((None, 1, 2 * R), lambda i: (i, 0, 0), memory_space=pltpu.SMEM),
                  pl.BlockSpec((R, D), lambda i: (i, 0)),
                  pl.BlockSpec(memory_space=pl.ANY),
                  pl.BlockSpec((1, D), lambda i: (0, 0))],
        out_specs=pl.BlockSpec((R, D), lambda i: (i, 0)),
        out_shape=jax.ShapeDtypeStruct((M, D), F32),
        scratch_shapes=[pltpu.VMEM((R, D), F32), pltpu.VMEM((R, D), F32), pltpu.SemaphoreType.DMA((2,))],
        compiler_params=_params("arbitrary"),
        name="combine",
    )(pos.reshape(M // R, 1, 2 * R), h, y, g.reshape(1, D))


def _rms_only_kernel(x_ref, g_ref, o_ref):
    o_ref[...] = _rms(x_ref[...], g_ref[...], 1e-6)


def rms_only(x, g, *, tm):
    M, D = x.shape
    return pl.pallas_call(
        _rms_only_kernel,
        grid=(M // tm,),
        in_specs=[pl.BlockSpec((tm, D), lambda i: (i, 0)), pl.BlockSpec((1, D), lambda i: (0, 0))],
        out_specs=pl.BlockSpec((tm, D), lambda i: (i, 0)),
        out_shape=jax.ShapeDtypeStruct((M, D), F32),
        compiler_params=_params("parallel"),
        name="rms_only",
    )(x, g.reshape(1, D))


def moe_ffn(h, g_norm, w_router_pad, w_gu, w_d, g_final, *, tm, R, final_norm):
    M, D = h.shape
    idx_p, gate_p = router(h, g_norm, w_router_pad, tm=min(512, M))
    expert = idx_p[:, :TOP_K].reshape(-1)
    gate = gate_p[:, :TOP_K].reshape(-1)
    n_pairs = TOP_K * M
    n_tiles = n_pairs // tm + N_EXPERTS
    Mp = n_tiles * tm
    onehot = (expert[:, None] == jnp.arange(N_EXPERTS, dtype=I32)[None, :]).astype(I32)
    csum = jnp.cumsum(onehot, axis=0)
    counts = csum[-1]
    tiles_per = (counts + tm - 1) // tm
    tile_end = jnp.cumsum(tiles_per)
    tile_start = tile_end - tiles_per
    group_start = jnp.cumsum(counts) - counts
    rank = jnp.sum(onehot * (csum - 1), axis=1)
    slot_of_pair = jnp.sum(onehot * tile_start[None, :], axis=1) * tm + rank
    _, sorted_pair, sorted_gate = lax.sort((expert, jnp.arange(n_pairs, dtype=I32), gate), num_keys=1,
                                           is_stable=True)
    tile_ids = jnp.arange(n_tiles, dtype=I32)
    tile_valid = (tile_ids < tile_end[-1]).astype(I32)
    tile_expert = jnp.minimum(jnp.sum((tile_ids[:, None] >= tile_end[None, :]).astype(I32), axis=1),
                              N_EXPERTS - 1)
    within = (tile_ids - tile_start[tile_expert]) * tm
    pos = within[:, None] + jnp.arange(tm, dtype=I32)[None, :]
    live = (tile_valid[:, None] != 0) & (pos < counts[tile_expert][:, None])
    src = jnp.clip(group_start[tile_expert][:, None] + pos, 0, n_pairs - 1).reshape(Mp)
    live = live.reshape(Mp)
    token_of_slot = jnp.where(live, sorted_pair[src] // TOP_K, 0)
    scale_of_slot = jnp.where(live, sorted_gate[src], 0.0)
    last_expert = tile_expert[jnp.maximum(tile_end[-1] - 1, 0)]
    tile_expert = jnp.where(tile_valid != 0, tile_expert, last_expert)

    x_sorted = gather_rows(h, token_of_slot, R=R)
    y_sorted = ffn(x_sorted, g_norm, w_gu, w_d, tile_expert, tile_valid, scale_of_slot.reshape(Mp, 1),
                   tm=tm, tf=w_d.shape[1] // 2, dense=False)
    return combine(h, y_sorted, slot_of_pair, g_final, R=R, final_norm=final_norm)


def kernel(x_prompt, x_sample, mem_prompt, cache_attn_k, cache_attn_v, cache_mem_k, cache_mem_v, state_gla,
           page_table, rel_bias, norm_mix, w_in, w_gla_a2, b_gla_a, gla_norm, diff_subln, lambda_q1, lambda_k1,
           lambda_q2, lambda_k2, w_out, norm_mem, norm_memkv, w_mq, w_mkv, w_mo, norm_ffn, w_ffn_gu,
           w_ffn_down, w_router, w_exp_gu, w_exp_down, norm_final):
    B, L, D = x_prompt.shape
    DB, LS, _ = x_sample.shape
    depth = w_in.shape[0]
    n_mem = mem_prompt.shape[1]
    n_pages, page = page_table.shape[1], cache_attn_k.shape[2]
    past_len = n_pages * page
    M, MS = B * L, DB * LS
    T_ATT = 512
    GLA_C = 64

    bias_p = bias_tiles(rel_bias, R=T_ATT, C=T_ATT, offsets=(0, T_ATT))
    bias_s = bias_tiles(rel_bias, R=2 * LS, C=past_len + page, offsets=(past_len,),
                        rows_per_head=LS).reshape(DIFF_H, 2 * LS, past_len + page)
    cache_k = cache_attn_k.reshape(depth, -1, page * DIFF_H, DIFF_DV)
    cache_v = cache_attn_v.reshape(depth, -1, page * DIFF_H, DIFF_DV)
    kv_heads = [(DIFF_W, DIFF_H, DIFF_DV), (2 * DIFF_W, DIFF_H, DIFF_DV)]
    mem_heads = [(0, MEM_H, MEM_DH), (MEM_H * MEM_DH, MEM_H, MEM_DH)]
    w_router_pad = jnp.pad(w_router, ((0, 0), (0, 0), (0, LANES - N_EXPERTS)))

    hp = x_prompt.reshape(M, D)
    hs = x_sample.reshape(MS, D)
    pk, pv, ps, pmk, pmv, sk, sv, ss = [], [], [], [], [], [], [], []
    for l in range(depth):
        lambda_init = 0.8 - 0.6 * math.exp(-0.3 * l)
        w_in_p = jnp.pad(w_in[l], ((0, 0), (0, N_IN_PAD - N_IN))).astype(BF16)
        w_a2p = jnp.pad(w_gla_a2[l], ((0, LANES - GLA_LR), (0, 0)))
        lamp = jnp.stack([lambda_q1[l], lambda_k1[l], lambda_q2[l], lambda_k2[l]])
        w_out_a = w_out[l, :DIFF_W].astype(BF16)
        w_out_g = w_out[l, DIFF_W:].astype(BF16)
        w_mq_b, w_mo_b, w_mkv_b = w_mq[l].astype(BF16), w_mo[l].astype(BF16), w_mkv[l].astype(BF16)
        last = l == depth - 1

        proj_p, k_p, v_p = rms_matmul_heads(hp, norm_mix[l], w_in_p, tm=512, keep_full=True, head_outs=kv_heads)
        proj_s, k_s, v_s = rms_matmul_heads(hs, norm_mix[l], w_in_p, tm=512, keep_full=True, head_outs=kv_heads)
        proj_p = proj_p.reshape(B, L, N_IN_PAD)
        proj_s = proj_s.reshape(DB, LS, N_IN_PAD)
        att_p = diff_attention_prompt(proj_p, bias_p, lamp, diff_subln[l], T=T_ATT, lambda_init=lambda_init)
        att_s = diff_attention_decode(proj_s, cache_k, cache_v, page_table, bias_s, lamp, diff_subln[l],
                                      layer=l, lambda_init=lambda_init)
        zero_state = jnp.zeros((B, GLA_H, GLA_DV, GLA_DK), F32)
        gla_p, st_p = gla(proj_p, w_a2p, b_gla_a[l], gla_norm[l], zero_state, C=GLA_C, SB=16)
        gla_s, st_s = gla(proj_s, w_a2p, b_gla_a[l], gla_norm[l], jnp.swapaxes(state_gla[l], -1, -2),
                          C=LS, SB=LS)
        hp = mm_res([att_p.reshape(M, DIFF_W), gla_p.reshape(M, GLA_W)], [w_out_a, w_out_g], hp, tm=512)
        hs = mm_res([att_s.reshape(MS, DIFF_W), gla_s.reshape(MS, GLA_W)], [w_out_a, w_out_g], hs, tm=512)

        mk_p, mv_p = rms_matmul_heads(mem_prompt.reshape(B * n_mem, D), norm_memkv[l], w_mkv_b, tm=512,
                                      keep_full=False, head_outs=mem_heads)
        mk_p = mk_p.reshape(B, n_mem, MEM_H, MEM_DH)
        mv_p = mv_p.reshape(B, n_mem, MEM_H, MEM_DH)
        q_p = rms_matmul(hp, norm_mem[l], w_mq_b, tm=512, out_dtype=BF16, scale=MEM_DH ** -0.5)
        q_s = rms_matmul(hs, norm_mem[l], w_mq_b, tm=512, out_dtype=BF16, scale=MEM_DH ** -0.5)
        xo_p = xattn_core(q_p.reshape(B, L, D), mk_p[None], mv_p[None], layer=0, tm=512)
        xo_s = xattn_core(q_s.reshape(DB, LS, D), cache_mem_k, cache_mem_v, layer=l, tm=LS)
        hp = mm_res([xo_p.reshape(M, D)], [w_mo_b], hp, tm=512)
        hs = mm_res([xo_s.reshape(MS, D)], [w_mo_b], hs, tm=512)

        if l % 2 == 0:
            w_gu = w_ffn_gu[l // 2].astype(BF16)[None]
            w_d = w_ffn_down[l // 2].astype(BF16)[None]
            for_dense = lambda h, tm: ffn(
                h, norm_ffn[l], w_gu, w_d, jnp.zeros((h.shape[0] // tm,), I32),
                jnp.ones((h.shape[0] // tm,), I32), jnp.ones((h.shape[0], 1), F32),
                tm=tm, tf=w_d.shape[1] // 2, dense=True)
            hp, hs = for_dense(hp, 512), for_dense(hs, 512)
            if last:
                hp, hs = rms_only(hp, norm_final, tm=512), rms_only(hs, norm_final, tm=512)
        else:
            w_gu = w_exp_gu[l // 2].astype(BF16)
            w_d = w_exp_down[l // 2].astype(BF16)
            hp = moe_ffn(hp, norm_ffn[l], w_router_pad[l // 2], w_gu, w_d, norm_final,
                         tm=512, R=256, final_norm=last)
            hs = moe_ffn(hs, norm_ffn[l], w_router_pad[l // 2], w_gu, w_d, norm_final,
                         tm=256, R=256, final_norm=last)

        pk.append(k_p.reshape(B, L, DIFF_H, DIFF_DV))
        pv.append(v_p.reshape(B, L, DIFF_H, DIFF_DV))
        ps.append(jnp.swapaxes(st_p, -1, -2))
        pmk.append(mk_p)
        pmv.append(mv_p)
        sk.append(k_s.reshape(DB, LS, DIFF_H, DIFF_DV))
        sv.append(v_s.reshape(DB, LS, DIFF_H, DIFF_DV))
        ss.append(jnp.swapaxes(st_s, -1, -2))

    return (hp.reshape(B, L, D), hs.reshape(DB, LS, D), jnp.stack(pk), jnp.stack(pv), jnp.stack(ps),
            jnp.stack(pmk), jnp.stack(pmv), jnp.stack(sk), jnp.stack(sv), jnp.stack(ss))
```

```python
import functools
import math

import numpy as np
import jax
import jax.numpy as jnp
from jax import lax
from jax.experimental import pallas as pl
from jax.experimental.pallas import tpu as pltpu

F32 = jnp.float32
BF16 = jnp.bfloat16
I32 = jnp.int32

LANES = 128
SUBLANES = 8
VMEM_LIMIT_BYTES = 56 * 1024 * 1024

D_MODEL = 1024
DIFF_H = 4
DIFF_DV = 128
DIFF_DH = 64
DIFF_W = DIFF_H * DIFF_DV
GLA_H = 4
GLA_DK = 64
GLA_DV = 128
GLA_W = GLA_H * GLA_DV
GLA_LR = 16
GATE_NORM = 16.0
N_BUCKETS = 32
MAX_DISTANCE = 128
MEM_H = 4
MEM_DH = 256
N_EXPERTS = 8
TOP_K = 2
N_IN = 3 * DIFF_W + 2 * GLA_H * GLA_DK + 2 * GLA_W + GLA_LR
N_IN_PAD = 3200
NEG_BIG = -1e30
LOG2E = math.log2(math.e)

COL_Q = 0
COL_K = 4
COL_V = 8
COL_QK_GLA = 3
COL_V_GLA = 4
COL_G_GLA = 5
COL_ALR = 24

NT_DIMS = (((1,), (1,)), ((), ()))
TN_DIMS = (((0,), (0,)), ((), ()))


def _params(*sem):
    return pltpu.CompilerParams(dimension_semantics=sem, vmem_limit_bytes=VMEM_LIMIT_BYTES)


def _rms(x, g, eps):
    ms = jnp.mean(x * x, axis=-1, keepdims=True)
    return (x * lax.rsqrt(ms + eps)) * g


def _silu(x):
    return x / (1.0 + jnp.exp(-x))


def _rms_matmul_kernel(x_ref, g_ref, w_ref, o_ref, *, eps, scale):
    xn = _rms(x_ref[...], g_ref[...], eps).astype(BF16)
    y = jnp.dot(xn, w_ref[...], preferred_element_type=F32)
    if scale != 1.0:
        y = y * scale
    o_ref[...] = y.astype(o_ref.dtype)


def rms_matmul(x, g, w, *, tm, out_dtype=F32, scale=1.0, eps=1e-6):
    M, K = x.shape
    N = w.shape[1]
    return pl.pallas_call(
        functools.partial(_rms_matmul_kernel, eps=eps, scale=scale),
        grid=(M // tm,),
        in_specs=[pl.BlockSpec((tm, K), lambda i: (i, 0)),
                  pl.BlockSpec((1, K), lambda i: (0, 0)),
                  pl.BlockSpec((K, N), lambda i: (0, 0))],
        out_specs=pl.BlockSpec((tm, N), lambda i: (i, 0)),
        out_shape=jax.ShapeDtypeStruct((M, N), out_dtype),
        compiler_params=_params("parallel"),
        name="rms_matmul",
    )(x, g.reshape(1, K), w)


def _rms_matmul_heads_kernel(x_ref, g_ref, w_ref, *o_refs, eps, keep_full, head_outs):
    xn = _rms(x_ref[...], g_ref[...], eps).astype(BF16)
    y = jnp.dot(xn, w_ref[...], preferred_element_type=F32)
    if keep_full:
        o_refs[0][...] = y
    for o_ref, (col0, n_heads, width) in zip(o_refs[1 if keep_full else 0:], head_outs):
        for h in range(n_heads):
            o_ref[:, h, :] = y[:, col0 + h * width:col0 + (h + 1) * width]


def rms_matmul_heads(x, g, w, *, tm, keep_full, head_outs, eps=1e-6):
    M, K = x.shape
    N = w.shape[1]
    out_specs = [pl.BlockSpec((tm, nh, wd), lambda i: (i, 0, 0)) for _, nh, wd in head_outs]
    out_shape = [jax.ShapeDtypeStruct((M, nh, wd), F32) for _, nh, wd in head_outs]
    if keep_full:
        out_specs.insert(0, pl.BlockSpec((tm, N), lambda i: (i, 0)))
        out_shape.insert(0, jax.ShapeDtypeStruct((M, N), F32))
    return pl.pallas_call(
        functools.partial(_rms_matmul_heads_kernel, eps=eps, keep_full=keep_full, head_outs=tuple(head_outs)),
        grid=(M // tm,),
        in_specs=[pl.BlockSpec((tm, K), lambda i: (i, 0)),
                  pl.BlockSpec((1, K), lambda i: (0, 0)),
                  pl.BlockSpec((K, N), lambda i: (0, 0))],
        out_specs=out_specs,
        out_shape=out_shape,
        compiler_params=_params("parallel"),
        name="rms_matmul_heads",
    )(x, g.reshape(1, K), w)


def _to_token_major(o_ref, y):
    rows, n = y.shape
    for c in range(n // LANES):
        o_ref[pl.ds(c, rows, stride=n // LANES), :] = y[:, c * LANES:(c + 1) * LANES]


def _from_token_major(x_ref, n):
    rows = x_ref.shape[0] // (n // LANES)
    return [x_ref[pl.ds(c, rows, stride=n // LANES), :] for c in range(n // LANES)]


def _mm_res_kernel(*refs, n_lhs, token_major):
    a_refs, w_refs = refs[:n_lhs], refs[n_lhs:2 * n_lhs]
    res_ref, o_ref = refs[2 * n_lhs], refs[2 * n_lhs + 1]
    acc = res_ref[...]
    for a_ref, w_ref in zip(a_refs, w_refs):
        acc = acc + jnp.dot(a_ref[...], w_ref[...], preferred_element_type=F32)
    o_ref[...] = acc
    if token_major:
        _to_token_major(refs[2 * n_lhs + 2], acc)


def mm_res(lhs, ws, res, *, tm, token_major=False):
    M, N = res.shape
    n = len(lhs)
    in_specs = ([pl.BlockSpec((tm, a.shape[1]), lambda i: (i, 0)) for a in lhs]
                + [pl.BlockSpec(w.shape, lambda i: (0, 0)) for w in ws]
                + [pl.BlockSpec((tm, N), lambda i: (i, 0))])
    out_specs = [pl.BlockSpec((tm, N), lambda i: (i, 0))]
    out_shape = [jax.ShapeDtypeStruct((M, N), F32)]
    if token_major:
        out_specs.append(pl.BlockSpec((tm * N // LANES, LANES), lambda i: (i, 0)))
        out_shape.append(jax.ShapeDtypeStruct((M * N // LANES, LANES), F32))
    out = pl.pallas_call(
        functools.partial(_mm_res_kernel, n_lhs=n, token_major=token_major),
        grid=(M // tm,),
        in_specs=in_specs,
        out_specs=out_specs,
        out_shape=out_shape,
        compiler_params=_params("parallel"),
        name="mm_res",
    )(*lhs, *ws, res)
    return out if token_major else out[0]


def _bias_kernel(tab_ref, o_ref, *, offsets, rows_per_head):
    h = pl.program_id(0)
    R, C = o_ref.shape[-2], o_ref.shape[-1]
    r = lax.broadcasted_iota(I32, (R, C), 0)
    if rows_per_head != R:
        r = r % rows_per_head
    c = lax.broadcasted_iota(I32, (R, C), 1)
    max_exact = N_BUCKETS // 2
    far = tab_ref[N_BUCKETS - 1, h]
    for kind, off in enumerate(offsets):
        rel = off + r - c
        n = jnp.maximum(rel, 0)
        nf = jnp.maximum(n, max_exact).astype(F32)
        large = max_exact + (jnp.log(nf / max_exact) / math.log(MAX_DISTANCE / max_exact)
                             * (N_BUCKETS - max_exact)).astype(I32)
        bucket = jnp.where(n < max_exact, n, jnp.minimum(large, N_BUCKETS - 1))
        acc = jnp.zeros((R, C), F32)
        for b in range(N_BUCKETS - 1):
            acc = jnp.where(bucket == b, (tab_ref[b, h] - far) * LOG2E, acc)
        o_ref[kind] = jnp.where(rel >= 0, acc, NEG_BIG)


def bias_tiles(rel_bias, *, R, C, offsets, rows_per_head=None):
    rows_per_head = R if rows_per_head is None else rows_per_head
    return pl.pallas_call(
        functools.partial(_bias_kernel, offsets=tuple(offsets), rows_per_head=rows_per_head),
        grid=(DIFF_H,),
        in_specs=[pl.BlockSpec(memory_space=pltpu.SMEM)],
        out_specs=pl.BlockSpec((None, len(offsets), R, C), lambda h: (h, 0, 0, 0)),
        out_shape=jax.ShapeDtypeStruct((DIFF_H, len(offsets), R, C), F32),
        compiler_params=_params("arbitrary"),
        name="bias_tiles",
    )(rel_bias)


def _lambda_value(lamp, lambda_init):
    s1 = jnp.sum(lamp[0:1, :] * lamp[1:2, :], axis=-1, keepdims=True)
    s2 = jnp.sum(lamp[2:3, :] * lamp[3:4, :], axis=-1, keepdims=True)
    return jnp.exp(s1) - jnp.exp(s2) + lambda_init


def _diff_finish(o1, o2, lam, subln, lambda_init):
    o = o1 - lam * o2
    return _rms(o, subln, 1e-5) * (1.0 - lambda_init)


def _diff_attn_kernel(qi_tab, ki_tab, q_ref, k_ref, v_ref, bias_ref, lamp_ref, subln_ref, o_ref,
                      q2_sc, m_sc, acc_sc, *, T, RB, lambda_init):
    t = pl.program_id(2)
    qi = qi_tab[t]
    ki = ki_tab[t]

    @pl.when(ki == 0)
    def _init():
        q = q_ref[...] * (DIFF_DH ** -0.5 * LOG2E)
        lane = lax.broadcasted_iota(I32, q.shape, 1)
        q2_sc[0:T, :] = jnp.where(lane < DIFF_DH, q, 0.0).astype(BF16)
        q2_sc[T:2 * T, :] = jnp.where(lane >= DIFF_DH, q, 0.0).astype(BF16)
        m_sc[...] = jnp.full(m_sc.shape, NEG_BIG, F32)
        acc_sc[...] = jnp.zeros(acc_sc.shape, F32)

    def update(kind):
        kb = k_ref[...].astype(BF16)
        vb = jnp.concatenate([v_ref[...].astype(BF16), jnp.ones((T, DIFF_DV), BF16)], axis=1)
        for r0 in range(0, 2 * T, RB):
            rows = slice(r0, r0 + RB)
            s = lax.dot_general(q2_sc[rows, :], kb, NT_DIMS, preferred_element_type=F32)
            if kind is not None:
                s = s + bias_ref[kind, r0 % T:r0 % T + RB, :]
            cols = [s[:, c * LANES:(c + 1) * LANES] for c in range(T // LANES)]
            m_old = m_sc[rows, :]
            m_new = jnp.maximum(m_old, jnp.max(functools.reduce(jnp.maximum, cols), axis=-1, keepdims=True))
            alpha = jnp.exp2(m_old - m_new)
            p = jnp.concatenate([jnp.exp2(c - m_new) for c in cols], axis=1).astype(BF16)
            acc_sc[rows, :] = (jnp.concatenate([alpha, alpha], axis=1) * acc_sc[rows, :]
                               + jnp.dot(p, vb, preferred_element_type=F32))
            m_sc[rows, :] = m_new

    @pl.when(ki < qi - 1)
    def _far():
        update(None)

    @pl.when(ki == qi - 1)
    def _near():
        update(1)

    @pl.when(ki == qi)
    def _diag():
        update(0)
        on = acc_sc[:, 0:DIFF_DV] / acc_sc[:, DIFF_DV:2 * DIFF_DV]
        lam = _lambda_value(lamp_ref[...], lambda_init)
        o_ref[...] = _diff_finish(on[0:T], on[T:2 * T], lam, subln_ref[...], lambda_init).astype(o_ref.dtype)


def diff_attention_prompt(proj, bias, lamp, subln, *, T, lambda_init):
    B, L, _ = proj.shape
    nq = L // T
    pairs = [(qi, ki) for qi in range(nq) for ki in range(qi + 1)]
    qi_tab = jnp.asarray(np.array([p[0] for p in pairs], np.int32))
    ki_tab = jnp.asarray(np.array([p[1] for p in pairs], np.int32))
    grid_spec = pltpu.PrefetchScalarGridSpec(
        num_scalar_prefetch=2,
        grid=(B, DIFF_H, len(pairs)),
        in_specs=[
            pl.BlockSpec((None, T, DIFF_DV), lambda b, h, t, qt, kt: (b, qt[t], COL_Q + h)),
            pl.BlockSpec((None, T, DIFF_DV), lambda b, h, t, qt, kt: (b, kt[t], COL_K + h)),
            pl.BlockSpec((None, T, DIFF_DV), lambda b, h, t, qt, kt: (b, kt[t], COL_V + h)),
            pl.BlockSpec((None, 2, T, T), lambda b, h, t, qt, kt: (h, 0, 0, 0)),
            pl.BlockSpec((4, DIFF_DH), lambda b, h, t, qt, kt: (0, 0)),
            pl.BlockSpec((1, DIFF_DV), lambda b, h, t, qt, kt: (0, 0)),
        ],
        out_specs=pl.BlockSpec((None, T, DIFF_DV), lambda b, h, t, qt, kt: (b, qt[t], h)),
        scratch_shapes=[pltpu.VMEM((2 * T, DIFF_DV), BF16),
                        pltpu.VMEM((2 * T, LANES), F32),
                        pltpu.VMEM((2 * T, 2 * DIFF_DV), F32)],
    )
    return pl.pallas_call(
        functools.partial(_diff_attn_kernel, T=T, RB=min(T, 128), lambda_init=lambda_init),
        grid_spec=grid_spec,
        out_shape=jax.ShapeDtypeStruct((B, L, DIFF_W), BF16),
        compiler_params=_params("parallel", "parallel", "arbitrary"),
        name="diff_attn_prompt",
    )(qi_tab, ki_tab, proj, proj, proj, bias, lamp, subln.reshape(1, DIFF_DV))


def _diff_decode_kernel(pt_ref, proj_ref, bias_ref, lamp_ref, subln_ref, *rest, n_pages, page, lq, lambda_init):
    k_refs, v_refs = rest[:n_pages], rest[n_pages:2 * n_pages]
    o_ref = rest[2 * n_pages]
    lam = _lambda_value(lamp_ref[...], lambda_init)
    lane = lax.broadcasted_iota(I32, (lq, DIFF_DV), 1)
    pad = jnp.zeros((page - lq, DIFF_DV), F32)
    for h in range(DIFF_H):
        cols = slice(h * DIFF_DV, (h + 1) * DIFF_DV)
        q = proj_ref[:, cols] * (DIFF_DH ** -0.5 * LOG2E)
        qbd = jnp.concatenate([jnp.where(lane < DIFF_DH, q, 0.0), jnp.where(lane >= DIFF_DH, q, 0.0)],
                              axis=0).astype(BF16)
        k_tail = jnp.concatenate([proj_ref[:, DIFF_W + h * DIFF_DV:DIFF_W + (h + 1) * DIFF_DV], pad], 0)
        v_tail = jnp.concatenate([proj_ref[:, 2 * DIFF_W + h * DIFF_DV:2 * DIFF_W + (h + 1) * DIFF_DV], pad], 0)
        head_rows = pl.ds(h, page, stride=DIFF_H)
        parts = [lax.dot_general(qbd, k_refs[j][head_rows, :].astype(BF16), NT_DIMS, preferred_element_type=F32)
                 for j in range(n_pages)]
        parts.append(lax.dot_general(qbd, k_tail.astype(BF16), NT_DIMS, preferred_element_type=F32))
        s = jnp.concatenate(parts, axis=1) + bias_ref[h]
        p = jnp.exp2(s - jnp.max(s, axis=-1, keepdims=True))
        l = jnp.sum(p, axis=-1, keepdims=True)
        pb = p.astype(BF16)
        acc = jnp.dot(pb[:, n_pages * page:], v_tail.astype(BF16), preferred_element_type=F32)
        for j in range(n_pages):
            acc = acc + jnp.dot(pb[:, j * page:(j + 1) * page], v_refs[j][head_rows, :].astype(BF16),
                                preferred_element_type=F32)
        on = acc / l
        o = _diff_finish(on[0:lq], on[lq:2 * lq], lam, subln_ref[...], lambda_init)
        o_ref[:, cols] = o.astype(o_ref.dtype)


def diff_attention_decode(proj, cache_k, cache_v, page_table, bias, lamp, subln, *, layer, lambda_init):
    DB, lq, _ = proj.shape
    n_pages = page_table.shape[1]
    page = cache_k.shape[2] // DIFF_H

    def page_spec(j):
        return pl.BlockSpec((None, None, page * DIFF_H, DIFF_DV), lambda b, pt: (layer, pt[b, j], 0, 0))

    grid_spec = pltpu.PrefetchScalarGridSpec(
        num_scalar_prefetch=1,
        grid=(DB,),
        in_specs=([pl.BlockSpec((None, lq, 3 * DIFF_W), lambda b, pt: (b, 0, 0)),
                   pl.BlockSpec(bias.shape, lambda b, pt: (0, 0, 0)),
                   pl.BlockSpec((4, DIFF_DH), lambda b, pt: (0, 0)),
                   pl.BlockSpec((1, DIFF_DV), lambda b, pt: (0, 0))]
                  + [page_spec(j) for j in range(n_pages)]
                  + [page_spec(j) for j in range(n_pages)]),
        out_specs=pl.BlockSpec((None, lq, DIFF_W), lambda b, pt: (b, 0, 0)),
    )
    return pl.pallas_call(
        functools.partial(_diff_decode_kernel, n_pages=n_pages, page=page, lq=lq, lambda_init=lambda_init),
        grid_spec=grid_spec,
        out_shape=jax.ShapeDtypeStruct((DB, lq, DIFF_W), BF16),
        compiler_params=_params("parallel"),
        name="diff_attn_decode",
    )(page_table, proj, bias, lamp, subln.reshape(1, DIFF_DV),
      *([cache_k] * n_pages), *([cache_v] * n_pages))


def _gla_kernel(qk_ref, v_ref, gg_ref, alr_ref, wa_ref, ba_ref, gn_ref, ex_ref, s0_ref, o_ref, sout_ref, st_sc,
                *, C, SB, n_chunks):
    c = pl.program_id(1)

    @pl.when(c == 0)
    def _init():
        st_sc[...] = s0_ref[...]

    z = jnp.dot(alr_ref[...], wa_ref[...], precision=lax.Precision.HIGHEST,
                preferred_element_type=F32) + ba_ref[...]
    logg = (jnp.minimum(z, 0.0) - jnp.log1p(jnp.exp(-jnp.abs(z)))) * (1.0 / GATE_NORM)
    row = lax.broadcasted_iota(I32, logg.shape, 0)
    b = logg
    d = 1
    while d < C:
        b = b + jnp.where(row >= d, pltpu.roll(b, d, 0), 0.0)
        d *= 2
    q_all = qk_ref[:, 0:GLA_H * GLA_DK] * (GLA_DK ** -0.5)
    k_all = qk_ref[:, GLA_H * GLA_DK:2 * GLA_H * GLA_DK]
    v_all = v_ref[...]
    n_sub = C // SB
    hk = lambda a, h: a[:, h * GLA_DK:(h + 1) * GLA_DK]
    hv = lambda a, h: a[:, h * GLA_DV:(h + 1) * GLA_DV]

    b_last = b[C - 1:C, :]
    q_in = (q_all * jnp.exp(b)).astype(BF16)
    k_dec = (k_all * jnp.exp(b_last - b)).astype(BF16)
    e_last = jnp.exp(b_last)
    q_off, k_off = [], []
    for i in range(1, n_sub):
        ref = b[i * SB:i * SB + 1, :]
        q_off.append((q_all[i * SB:(i + 1) * SB] * jnp.exp(b[i * SB:(i + 1) * SB] - ref)).astype(BF16))
        k_off.append((k_all * jnp.exp(jnp.minimum(ref - b, 0.0))).astype(BF16))
    col = lax.broadcasted_iota(I32, (SB, C), 1)

    outs = []
    for h in range(GLA_H):
        vb = hv(v_all, h).astype(BF16)
        st = st_sc[h]
        o = lax.dot_general(hk(q_in, h), st.astype(BF16), NT_DIMS, preferred_element_type=F32)
        if n_sub > 1:
            rows = [jnp.zeros((SB, C), F32)]
            for i in range(1, n_sub):
                a = lax.dot_general(hk(q_off[i - 1], h), hk(k_off[i - 1], h), NT_DIMS,
                                    preferred_element_type=F32)
                rows.append(jnp.where(col < i * SB, a, 0.0))
            a_off = jnp.concatenate(rows, axis=0)
            o = o + jnp.dot(a_off.astype(BF16), vb, preferred_element_type=F32)
        st_sc[h] = st * hk(e_last, h) + lax.dot_general(vb, hk(k_dec, h), TN_DIMS, preferred_element_type=F32)
        outs.append(o)

    rmod = row % SB
    xs = []
    for dlt in range(SB):
        kd, bd = (k_all, b) if dlt == 0 else (pltpu.roll(k_all, dlt, 0), pltpu.roll(b, dlt, 0))
        x = q_all * kd * jnp.exp(jnp.minimum(b - bd, 0.0))
        xs.append(jnp.where(rmod >= dlt, x, 0.0))
    w = jnp.dot(jnp.concatenate(xs, axis=0).astype(BF16), ex_ref[...], preferred_element_type=F32)
    o_diag = w[0:C] * v_all
    for dlt in range(1, SB):
        o_diag = o_diag + w[dlt * C:(dlt + 1) * C] * pltpu.roll(v_all, dlt, 0)

    for h in range(GLA_H):
        gate = _silu(hv(gg_ref[...], h))
        o = outs[h] + hv(o_diag, h)
        o_ref[:, h * GLA_DV:(h + 1) * GLA_DV] = (_rms(o, gn_ref[...], 1e-5) * gate).astype(o_ref.dtype)

    @pl.when(c == n_chunks - 1)
    def _fin():
        sout_ref[...] = st_sc[...]


def gla(proj, w_a2p, b_a, gla_g, s0t, *, C, SB):
    B, L, _ = proj.shape
    n_chunks = L // C
    W = GLA_W
    expand = (jnp.arange(GLA_H * GLA_DK, dtype=I32)[:, None] // GLA_DK
              == jnp.arange(W, dtype=I32)[None, :] // GLA_DV).astype(BF16)
    return pl.pallas_call(
        functools.partial(_gla_kernel, C=C, SB=SB, n_chunks=n_chunks),
        grid=(B, n_chunks),
        in_specs=[
            pl.BlockSpec((None, C, W), lambda b, c: (b, c, COL_QK_GLA)),
            pl.BlockSpec((None, C, W), lambda b, c: (b, c, COL_V_GLA)),
            pl.BlockSpec((None, C, W), lambda b, c: (b, c, COL_G_GLA)),
            pl.BlockSpec((None, C, LANES), lambda b, c: (b, c, COL_ALR)),
            pl.BlockSpec((LANES, GLA_H * GLA_DK), lambda b, c: (0, 0)),
            pl.BlockSpec((1, GLA_H * GLA_DK), lambda b, c: (0, 0)),
            pl.BlockSpec((1, GLA_DV), lambda b, c: (0, 0)),
            pl.BlockSpec((GLA_H * GLA_DK, W), lambda b, c: (0, 0)),
            pl.BlockSpec((None, GLA_H, GLA_DV, GLA_DK), lambda b, c: (b, 0, 0, 0)),
        ],
        out_specs=[pl.BlockSpec((None, C, W), lambda b, c: (b, c, 0)),
                   pl.BlockSpec((None, GLA_H, GLA_DV, GLA_DK), lambda b, c: (b, 0, 0, 0))],
        out_shape=[jax.ShapeDtypeStruct((B, L, W), BF16),
                   jax.ShapeDtypeStruct((B, GLA_H, GLA_DV, GLA_DK), F32)],
        scratch_shapes=[pltpu.VMEM((GLA_H, GLA_DV, GLA_DK), F32)],
        compiler_params=_params("parallel", "arbitrary"),
        name="gla",
    )(proj, proj, proj, proj, w_a2p, b_a.reshape(1, -1), gla_g.reshape(1, -1), expand, s0t)


def _xattn_kernel(q_ref, *refs):
    k_refs, v_refs, o_ref = refs[:MEM_H], refs[MEM_H:2 * MEM_H], refs[2 * MEM_H]
    for h in range(MEM_H):
        cols = slice(h * MEM_DH, (h + 1) * MEM_DH)
        s = lax.dot_general(q_ref[:, cols], k_refs[h][...].astype(BF16), NT_DIMS, preferred_element_type=F32)
        m = jnp.max(s, axis=-1, keepdims=True)
        p = jnp.exp(s - m)
        l = jnp.sum(p, axis=-1, keepdims=True)
        o = jnp.dot(p.astype(BF16), v_refs[h][...].astype(BF16), preferred_element_type=F32) / l
        o_ref[:, cols] = o.astype(o_ref.dtype)


def xattn_core(q, mk, mv, k_spec, v_spec, *, tm):
    Bx, Lx, W = q.shape
    return pl.pallas_call(
        _xattn_kernel,
        grid=(Bx, Lx // tm),
        in_specs=([pl.BlockSpec((None, tm, W), lambda b, i: (b, i, 0))]
                  + [k_spec(h) for h in range(MEM_H)] + [v_spec(h) for h in range(MEM_H)]),
        out_specs=pl.BlockSpec((None, tm, W), lambda b, i: (b, i, 0)),
        out_shape=jax.ShapeDtypeStruct((Bx, Lx, W), BF16),
        compiler_params=_params("parallel", "arbitrary"),
        name="xattn_core",
    )(q, *([mk] * MEM_H), *([mv] * MEM_H))


def _xattn_cached_kernel(q_ref, mk_hbm, mv_hbm, o_ref, k_buf, v_buf, sem, *, layer, n_seq):
    b = pl.program_id(0)
    slot = b % 2

    def copies(seq, buf_slot):
        out = []
        for h in range(MEM_H):
            out.append(pltpu.make_async_copy(mk_hbm.at[layer, seq, :, h, :], k_buf.at[buf_slot, h],
                                             sem.at[0, buf_slot]))
            out.append(pltpu.make_async_copy(mv_hbm.at[layer, seq, :, h, :], v_buf.at[buf_slot, h],
                                             sem.at[1, buf_slot]))
        return out

    @pl.when(b == 0)
    def _first():
        for c in copies(0, 0):
            c.start()

    @pl.when(b + 1 < n_seq)
    def _prefetch():
        for c in copies(b + 1, 1 - slot):
            c.start()

    for c in copies(b, slot):
        c.wait()
    _xattn_kernel(q_ref, *[k_buf.at[slot, h] for h in range(MEM_H)],
                  *[v_buf.at[slot, h] for h in range(MEM_H)], o_ref)


def xattn_cached(q, mk, mv, *, layer):
    DB, LS, W = q.shape
    n_mem = mk.shape[2]
    return pl.pallas_call(
        functools.partial(_xattn_cached_kernel, layer=layer, n_seq=DB),
        grid=(DB,),
        in_specs=[pl.BlockSpec((None, LS, W), lambda b: (b, 0, 0)),
                  pl.BlockSpec(memory_space=pl.ANY), pl.BlockSpec(memory_space=pl.ANY)],
        out_specs=pl.BlockSpec((None, LS, W), lambda b: (b, 0, 0)),
        out_shape=jax.ShapeDtypeStruct((DB, LS, W), BF16),
        scratch_shapes=[pltpu.VMEM((2, MEM_H, n_mem, MEM_DH), F32), pltpu.VMEM((2, MEM_H, n_mem, MEM_DH), F32),
                        pltpu.SemaphoreType.DMA((2, 2))],
        compiler_params=_params("arbitrary"),
        name="xattn_cached",
    )(q, mk, mv)


def _ffn_kernel(te_ref, tv_ref, x_ref, g_ref, wg_ref, wu_ref, wd_ref, sc_ref, o_ref, xn_sc, acc_sc,
                *, n_f, dense):
    i = pl.program_id(0)
    f = pl.program_id(1)

    @pl.when(f == 0)
    def _init():
        if dense:
            xn_sc[...] = _rms(x_ref[...], g_ref[...], 1e-6).astype(BF16)
        else:
            parts = _from_token_major(x_ref, xn_sc.shape[1])
            ms = sum(jnp.sum(p * p, axis=-1, keepdims=True) for p in parts) * (1.0 / xn_sc.shape[1])
            inv = lax.rsqrt(ms + 1e-6)
            for c, p in enumerate(parts):
                cols = slice(c * LANES, (c + 1) * LANES)
                xn_sc[:, cols] = ((p * inv) * g_ref[:, cols]).astype(BF16)
        acc_sc[...] = jnp.zeros(acc_sc.shape, F32)

    @pl.when(tv_ref[i] != 0)
    def _compute():
        xn = xn_sc[...]
        g = jnp.dot(xn, wg_ref[...], preferred_element_type=F32)
        u = jnp.dot(xn, wu_ref[...], preferred_element_type=F32)
        a = (_silu(g) * u).astype(BF16)
        acc_sc[...] += jnp.dot(a, wd_ref[...], preferred_element_type=F32)

    @pl.when(f == n_f - 1)
    def _fin():
        if dense:
            o_ref[...] = x_ref[...] + acc_sc[...]
        else:
            _to_token_major(o_ref, acc_sc[...] * sc_ref[...])


def ffn(x, g, w_gu, w_d, tile_expert, tile_valid, row_scale, *, tm, tf, dense):
    D = w_gu.shape[1]
    Mp = row_scale.shape[0]
    F = w_d.shape[1]
    n_f = F // tf
    last = n_f - 1

    def fblk(i, f, tv):
        return f * tv[i] + last * (1 - tv[i])

    if dense:
        x_spec = pl.BlockSpec((tm, D), lambda i, f, te, tv: (i, 0))
    else:
        x_spec = pl.BlockSpec((tm * D // LANES, LANES), lambda i, f, te, tv: (i, 0))
    grid_spec = pltpu.PrefetchScalarGridSpec(
        num_scalar_prefetch=2,
        grid=(Mp // tm, n_f),
        in_specs=[
            x_spec,
            pl.BlockSpec((1, D), lambda i, f, te, tv: (0, 0)),
            pl.BlockSpec((None, D, tf), lambda i, f, te, tv: (te[i], 0, fblk(i, f, tv))),
            pl.BlockSpec((None, D, tf), lambda i, f, te, tv: (te[i], 0, n_f + fblk(i, f, tv))),
            pl.BlockSpec((None, tf, D), lambda i, f, te, tv: (te[i], fblk(i, f, tv), 0)),
            pl.BlockSpec((tm, 1), lambda i, f, te, tv: (i, 0)),
        ],
        out_specs=x_spec,
        scratch_shapes=[pltpu.VMEM((tm, D), BF16), pltpu.VMEM((tm, D), F32)],
    )
    return pl.pallas_call(
        functools.partial(_ffn_kernel, n_f=n_f, dense=dense),
        grid_spec=grid_spec,
        out_shape=jax.ShapeDtypeStruct(x.shape, F32),
        compiler_params=_params("parallel", "arbitrary"),
        name="ffn_dense" if dense else "ffn_grouped",
    )(tile_expert, tile_valid, x, g.reshape(1, D), w_gu, w_gu, w_d, row_scale)


def _router_kernel(x_ref, g_ref, wr_ref, idx_ref, gate_ref):
    xn = _rms(x_ref[...], g_ref[...], 1e-6)
    logits = jnp.dot(xn, wr_ref[...], precision=lax.Precision.HIGHEST, preferred_element_type=F32)
    lane = lax.broadcasted_iota(I32, logits.shape, 1)
    real = lane < N_EXPERTS
    logits = jnp.where(real, logits, NEG_BIG)
    e = jnp.exp(logits - jnp.max(logits, axis=-1, keepdims=True))
    probs = jnp.where(real, e / jnp.sum(e, axis=-1, keepdims=True), -1.0)
    v1 = jnp.max(probs, axis=-1, keepdims=True)
    i1 = jnp.min(jnp.where(probs == v1, lane, LANES), axis=-1, keepdims=True)
    rest = jnp.where(lane == i1, -1.0, probs)
    v2 = jnp.max(rest, axis=-1, keepdims=True)
    i2 = jnp.min(jnp.where(rest == v2, lane, LANES), axis=-1, keepdims=True)
    den = v1 + v2
    idx_ref[...] = jnp.where(lane == 0, i1, jnp.where(lane == 1, i2, 0))
    gate_ref[...] = jnp.where(lane == 0, v1 / den, jnp.where(lane == 1, v2 / den, 0.0))


def router(x, g, w_router_pad, *, tm):
    M, D = x.shape
    return pl.pallas_call(
        _router_kernel,
        grid=(M // tm,),
        in_specs=[pl.BlockSpec((tm, D), lambda i: (i, 0)),
                  pl.BlockSpec((1, D), lambda i: (0, 0)),
                  pl.BlockSpec((D, LANES), lambda i: (0, 0))],
        out_specs=[pl.BlockSpec((tm, LANES), lambda i: (i, 0)),
                   pl.BlockSpec((tm, LANES), lambda i: (i, 0))],
        out_shape=[jax.ShapeDtypeStruct((M, LANES), I32), jax.ShapeDtypeStruct((M, LANES), F32)],
        compiler_params=_params("parallel"),
        name="router",
    )(x, g.reshape(1, D), w_router_pad)


def _row_copy(src_hbm, dst_ref, src_row, dst_row, sem):
    return pltpu.make_async_copy(src_hbm.at[pl.ds(pl.multiple_of(src_row * SUBLANES, SUBLANES), SUBLANES), :],
                                 dst_ref.at[pl.ds(pl.multiple_of(dst_row * SUBLANES, SUBLANES), SUBLANES), :], sem)


def _gather_kernel(idx_ref, src_hbm, o_ref, sem, *, R):
    def start(j, carry):
        r = 2 * j
        _row_copy(src_hbm, o_ref, idx_ref[0, r], r, sem).start(priority=0)
        _row_copy(src_hbm, o_ref, idx_ref[0, r + 1], r + 1, sem).start(priority=1)
        return carry

    lax.fori_loop(0, R // 2, start, 0, unroll=4)
    pltpu.make_async_copy(src_hbm.at[pl.ds(0, R * SUBLANES), :], o_ref, sem).wait()


def gather_rows(src, idx, *, R):
    Mp = idx.shape[0]
    return pl.pallas_call(
        functools.partial(_gather_kernel, R=R),
        grid=(Mp // R,),
        in_specs=[pl.BlockSpec((None, 1, R), lambda i: (i, 0, 0), memory_space=pltpu.SMEM),
                  pl.BlockSpec(memory_space=pl.ANY)],
        out_specs=pl.BlockSpec((R * SUBLANES, LANES), lambda i: (i, 0)),
        out_shape=jax.ShapeDtypeStruct((Mp * SUBLANES, LANES), src.dtype),
        scratch_shapes=[pltpu.SemaphoreType.DMA(())],
        compiler_params=_params("arbitrary"),
        name="gather_rows",
    )(idx.reshape(Mp // R, 1, R), src)


def _combine_kernel(pos_ref, h_ref, y_hbm, g_ref, o_ref, a_sc, b_sc, sem, *, R, final_norm):
    def start(r, carry):
        _row_copy(y_hbm, a_sc, pos_ref[0, 2 * r], r, sem.at[0]).start(priority=0)
        _row_copy(y_hbm, b_sc, pos_ref[0, 2 * r + 1], r, sem.at[1]).start(priority=1)
        return carry

    lax.fori_loop(0, R, start, 0, unroll=4)
    pltpu.make_async_copy(y_hbm.at[pl.ds(0, R * SUBLANES), :], a_sc, sem.at[0]).wait()
    pltpu.make_async_copy(y_hbm.at[pl.ds(0, R * SUBLANES), :], b_sc, sem.at[1]).wait()
    n = h_ref.shape[1]
    moe = jnp.concatenate([a + b for a, b in zip(_from_token_major(a_sc, n), _from_token_major(b_sc, n))], axis=1)
    out = h_ref[...] + moe
    if final_norm:
        out = _rms(out, g_ref[...], 1e-6)
    o_ref[...] = out


def combine(h, y, pos, g, *, R, final_norm):
    M, D = h.shape
    return pl.pallas_call(
        functools.partial(_combine_kernel, R=R, final_norm=final_norm),
        grid=(M // R,),
        in_specs=[pl.BlockSpec((None, 1, 2 * R), lambda i: (i, 0, 0), memory_space=pltpu.SMEM),
                  pl.BlockSpec((R, D), lambda i: (i, 0)),
                  pl.BlockSpec(memory_space=pl.ANY),
                  pl.BlockSpec((1, D), lambda i: (0, 0))],
        out_specs=pl.BlockSpec((R, D), lambda i: (i, 0)),
        out_shape=jax.ShapeDtypeStruct((M, D), F32),
        scratch_shapes=[pltpu.VMEM((R * SUBLANES, LANES), F32), pltpu.VMEM((R * SUBLANES, LANES), F32),
                        pltpu.SemaphoreType.DMA((2,))],
        compiler_params=_params("arbitrary"),
        name="combine",
    )(pos.reshape(M // R, 1, 2 * R), h, y, g.reshape(1, D))


def _rms_only_kernel(x_ref, g_ref, o_ref):
    o_ref[...] = _rms(x_ref[...], g_ref[...], 1e-6)


def rms_only(x, g, *, tm):
    M, D = x.shape
    return pl.pallas_call(
        _rms_only_kernel,
        grid=(M // tm,),
        in_specs=[pl.BlockSpec((tm, D), lambda i: (i, 0)), pl.BlockSpec((1, D), lambda i: (0, 0))],
        out_specs=pl.BlockSpec((tm, D), lambda i: (i, 0)),
        out_shape=jax.ShapeDtypeStruct((M, D), F32),
        compiler_params=_params("parallel"),
        name="rms_only",
    )(x, g.reshape(1, D))


def moe_ffn(h, h_tok, g_norm, w_router_pad, w_gu, w_d, g_final, *, tm, R, final_norm):
    M, D = h.shape
    idx_p, gate_p = router(h, g_norm, w_router_pad, tm=min(512, M))
    expert = idx_p[:, :TOP_K].reshape(-1)
    gate = gate_p[:, :TOP_K].reshape(-1)
    n_pairs = TOP_K * M
    n_tiles = n_pairs // tm + N_EXPERTS
    Mp = n_tiles * tm
    onehot = (expert[:, None] == jnp.arange(N_EXPERTS, dtype=I32)[None, :]).astype(I32)
    csum = jnp.cumsum(onehot, axis=0)
    counts = csum[-1]
    tiles_per = (counts + tm - 1) // tm
    tile_end = jnp.cumsum(tiles_per)
    tile_start = tile_end - tiles_per
    group_start = jnp.cumsum(counts) - counts
    rank = jnp.sum(onehot * (csum - 1), axis=1)
    slot_of_pair = jnp.sum(onehot * tile_start[None, :], axis=1) * tm + rank
    _, sorted_pair, sorted_gate = lax.sort((expert, jnp.arange(n_pairs, dtype=I32), gate), num_keys=1,
                                           is_stable=True)
    tile_ids = jnp.arange(n_tiles, dtype=I32)
    tile_valid = (tile_ids < tile_end[-1]).astype(I32)
    tile_expert = jnp.minimum(jnp.sum((tile_ids[:, None] >= tile_end[None, :]).astype(I32), axis=1),
                              N_EXPERTS - 1)
    within = (tile_ids - tile_start[tile_expert]) * tm
    pos = within[:, None] + jnp.arange(tm, dtype=I32)[None, :]
    live = (tile_valid[:, None] != 0) & (pos < counts[tile_expert][:, None])
    src = jnp.clip(group_start[tile_expert][:, None] + pos, 0, n_pairs - 1).reshape(Mp)
    live = live.reshape(Mp)
    token_of_slot = jnp.where(live, sorted_pair[src] // TOP_K, 0)
    scale_of_slot = jnp.where(live, sorted_gate[src], 0.0)
    last_expert = tile_expert[jnp.maximum(tile_end[-1] - 1, 0)]
    tile_expert = jnp.where(tile_valid != 0, tile_expert, last_expert)

    x_sorted = gather_rows(h_tok, token_of_slot, R=R)
    y_sorted = ffn(x_sorted, g_norm, w_gu, w_d, tile_expert, tile_valid, scale_of_slot.reshape(Mp, 1),
                   tm=tm, tf=w_d.shape[1] // 2, dense=False)
    return combine(h, y_sorted, slot_of_pair, g_final, R=R, final_norm=final_norm)


def kernel(x_prompt, x_sample, mem_prompt, cache_attn_k, cache_attn_v, cache_mem_k, cache_mem_v, state_gla,
           page_table, rel_bias, norm_mix, w_in, w_gla_a2, b_gla_a, gla_norm, diff_subln, lambda_q1, lambda_k1,
           lambda_q2, lambda_k2, w_out, norm_mem, norm_memkv, w_mq, w_mkv, w_mo, norm_ffn, w_ffn_gu,
           w_ffn_down, w_router, w_exp_gu, w_exp_down, norm_final):
    B, L, D = x_prompt.shape
    DB, LS, _ = x_sample.shape
    depth = w_in.shape[0]
    n_mem = mem_prompt.shape[1]
    n_pages, page = page_table.shape[1], cache_attn_k.shape[2]
    past_len = n_pages * page
    M, MS = B * L, DB * LS
    T_ATT = 512
    GLA_C = 64

    bias_p = bias_tiles(rel_bias, R=T_ATT, C=T_ATT, offsets=(0, T_ATT))
    bias_s = bias_tiles(rel_bias, R=2 * LS, C=past_len + page, offsets=(past_len,),
                        rows_per_head=LS).reshape(DIFF_H, 2 * LS, past_len + page)
    cache_k = cache_attn_k.reshape(depth, -1, page * DIFF_H, DIFF_DV)
    cache_v = cache_attn_v.reshape(depth, -1, page * DIFF_H, DIFF_DV)
    kv_heads = [(DIFF_W, DIFF_H, DIFF_DV), (2 * DIFF_W, DIFF_H, DIFF_DV)]
    mem_heads = [(0, MEM_H, MEM_DH), (MEM_H * MEM_DH, MEM_H, MEM_DH)]
    w_router_pad = jnp.pad(w_router, ((0, 0), (0, 0), (0, LANES - N_EXPERTS)))

    hp = x_prompt.reshape(M, D)
    hs = x_sample.reshape(MS, D)
    pk, pv, ps, pmk, pmv, sk, sv, ss = [], [], [], [], [], [], [], []
    for l in range(depth):
        lambda_init = 0.8 - 0.6 * math.exp(-0.3 * l)
        w_in_p = jnp.pad(w_in[l], ((0, 0), (0, N_IN_PAD - N_IN))).astype(BF16)
        w_a2p = jnp.pad(w_gla_a2[l], ((0, LANES - GLA_LR), (0, 0)))
        lamp = jnp.stack([lambda_q1[l], lambda_k1[l], lambda_q2[l], lambda_k2[l]])
        w_out_a = w_out[l, :DIFF_W].astype(BF16)
        w_out_g = w_out[l, DIFF_W:].astype(BF16)
        w_mq_b, w_mo_b, w_mkv_b = w_mq[l].astype(BF16), w_mo[l].astype(BF16), w_mkv[l].astype(BF16)
        last = l == depth - 1

        proj_p, k_p, v_p = rms_matmul_heads(hp, norm_mix[l], w_in_p, tm=512, keep_full=True, head_outs=kv_heads)
        proj_s, k_s, v_s = rms_matmul_heads(hs, norm_mix[l], w_in_p, tm=512, keep_full=True, head_outs=kv_heads)
        proj_p = proj_p.reshape(B, L, N_IN_PAD)
        proj_s = proj_s.reshape(DB, LS, N_IN_PAD)
        att_p = diff_attention_prompt(proj_p, bias_p, lamp, diff_subln[l], T=T_ATT, lambda_init=lambda_init)
        att_s = diff_attention_decode(proj_s, cache_k, cache_v, page_table, bias_s, lamp, diff_subln[l],
                                      layer=l, lambda_init=lambda_init)
        zero_state = jnp.zeros((B, GLA_H, GLA_DV, GLA_DK), F32)
        gla_p, st_p = gla(proj_p, w_a2p, b_gla_a[l], gla_norm[l], zero_state, C=GLA_C, SB=16)
        gla_s, st_s = gla(proj_s, w_a2p, b_gla_a[l], gla_norm[l], jnp.swapaxes(state_gla[l], -1, -2),
                          C=LS, SB=LS)
        hp = mm_res([att_p.reshape(M, DIFF_W), gla_p.reshape(M, GLA_W)], [w_out_a, w_out_g], hp, tm=512)
        hs = mm_res([att_s.reshape(MS, DIFF_W), gla_s.reshape(MS, GLA_W)], [w_out_a, w_out_g], hs, tm=512)

        mkv_p, mk_p, mv_p = rms_matmul_heads(mem_prompt.reshape(B * n_mem, D), norm_memkv[l], w_mkv_b, tm=512,
                                             keep_full=True, head_outs=mem_heads)
        mkv_p = mkv_p.reshape(B, n_mem, 2 * MEM_H * MEM_DH)
        mk_p = mk_p.reshape(B, n_mem, MEM_H, MEM_DH)
        mv_p = mv_p.reshape(B, n_mem, MEM_H, MEM_DH)
        q_p = rms_matmul(hp, norm_mem[l], w_mq_b, tm=512, out_dtype=BF16, scale=MEM_DH ** -0.5)
        q_s = rms_matmul(hs, norm_mem[l], w_mq_b, tm=512, out_dtype=BF16, scale=MEM_DH ** -0.5)
        xo_p = xattn_core(
            q_p.reshape(B, L, D), mkv_p, mkv_p,
            lambda h: pl.BlockSpec((None, n_mem, MEM_DH), lambda b, i: (b, 0, h)),
            lambda h: pl.BlockSpec((None, n_mem, MEM_DH), lambda b, i: (b, 0, MEM_H + h)), tm=512)
        xo_s = xattn_cached(q_s.reshape(DB, LS, D), cache_mem_k, cache_mem_v, layer=l)
        moe_layer = l % 2 == 1
        hp = mm_res([xo_p.reshape(M, D)], [w_mo_b], hp, tm=512, token_major=moe_layer)
        hs = mm_res([xo_s.reshape(MS, D)], [w_mo_b], hs, tm=512, token_major=moe_layer)
        if moe_layer:
            (hp, hp_tok), (hs, hs_tok) = hp, hs

        if l % 2 == 0:
            w_gu = w_ffn_gu[l // 2].astype(BF16)[None]
            w_d = w_ffn_down[l // 2].astype(BF16)[None]
            for_dense = lambda h, tm: ffn(
                h, norm_ffn[l], w_gu, w_d, jnp.zeros((h.shape[0] // tm,), I32),
                jnp.ones((h.shape[0] // tm,), I32), jnp.ones((h.shape[0], 1), F32),
                tm=tm, tf=w_d.shape[1] // 2, dense=True)
            hp, hs = for_dense(hp, 512), for_dense(hs, 512)
            if last:
                hp, hs = rms_only(hp, norm_final, tm=512), rms_only(hs, norm_final, tm=512)
        else:
            w_gu = w_exp_gu[l // 2].astype(BF16)
            w_d = w_exp_down[l // 2].astype(BF16)
            hp = moe_ffn(hp, hp_tok, norm_ffn[l], w_router_pad[l // 2], w_gu, w_d, norm_final,
                         tm=512, R=256, final_norm=last)
            hs = moe_ffn(hs, hs_tok, norm_ffn[l], w_router_pad[l // 2], w_gu, w_d, norm_final,
                         tm=256, R=256, final_norm=last)

        pk.append(k_p.reshape(B, L, DIFF_H, DIFF_DV))
        pv.append(v_p.reshape(B, L, DIFF_H, DIFF_DV))
        ps.append(jnp.swapaxes(st_p, -1, -2))
        pmk.append(mk_p)
        pmv.append(mv_p)
        sk.append(k_s.reshape(DB, LS, DIFF_H, DIFF_DV))
        sv.append(v_s.reshape(DB, LS, DIFF_H, DIFF_DV))
        ss.append(jnp.swapaxes(st_s, -1, -2))

    return (hp.reshape(B, L, D), hs.reshape(DB, LS, D), jnp.stack(pk), jnp.stack(pv), jnp.stack(ps),
            jnp.stack(pmk), jnp.stack(pmv), jnp.stack(sk), jnp.stack(sv), jnp.stack(ss))
```

```python
import functools
import math

import numpy as np
import jax
import jax.numpy as jnp
from jax import lax
from jax.experimental import pallas as pl
from jax.experimental.pallas import tpu as pltpu

F32 = jnp.float32
BF16 = jnp.bfloat16
I32 = jnp.int32

LANES = 128
SUBLANES = 8
VMEM_LIMIT_BYTES = 56 * 1024 * 1024

D_MODEL = 1024
DIFF_H = 4
DIFF_DV = 128
DIFF_DH = 64
DIFF_W = DIFF_H * DIFF_DV
GLA_H = 4
GLA_DK = 64
GLA_DV = 128
GLA_W = GLA_H * GLA_DV
GLA_LR = 16
GATE_NORM = 16.0
N_BUCKETS = 32
MAX_DISTANCE = 128
MEM_H = 4
MEM_DH = 256
N_EXPERTS = 8
TOP_K = 2
N_IN = 3 * DIFF_W + 2 * GLA_H * GLA_DK + 2 * GLA_W + GLA_LR
N_IN_PAD = 3200
NEG_BIG = -1e30
LOG2E = math.log2(math.e)

COL_Q = 0
COL_K = 4
COL_V = 8
COL_QK_GLA = 3
COL_V_GLA = 4
COL_G_GLA = 5
COL_ALR = 24

NT_DIMS = (((1,), (1,)), ((), ()))
TN_DIMS = (((0,), (0,)), ((), ()))


def _params(*sem):
    return pltpu.CompilerParams(dimension_semantics=sem, vmem_limit_bytes=VMEM_LIMIT_BYTES)


def _rms(x, g, eps):
    ms = jnp.mean(x * x, axis=-1, keepdims=True)
    return (x * lax.rsqrt(ms + eps)) * g


def _silu(x):
    return x / (1.0 + jnp.exp(-x))


def _rms_matmul_kernel(x_ref, g_ref, w_ref, o_ref, *, eps, scale):
    xn = _rms(x_ref[...], g_ref[...], eps).astype(BF16)
    y = jnp.dot(xn, w_ref[...], preferred_element_type=F32)
    if scale != 1.0:
        y = y * scale
    o_ref[...] = y.astype(o_ref.dtype)


def rms_matmul(x, g, w, *, tm, out_dtype=F32, scale=1.0, eps=1e-6):
    M, K = x.shape
    N = w.shape[1]
    return pl.pallas_call(
        functools.partial(_rms_matmul_kernel, eps=eps, scale=scale),
        grid=(M // tm,),
        in_specs=[pl.BlockSpec((tm, K), lambda i: (i, 0)),
                  pl.BlockSpec((1, K), lambda i: (0, 0)),
                  pl.BlockSpec((K, N), lambda i: (0, 0))],
        out_specs=pl.BlockSpec((tm, N), lambda i: (i, 0)),
        out_shape=jax.ShapeDtypeStruct((M, N), out_dtype),
        compiler_params=_params("parallel"),
        name="rms_matmul",
    )(x, g.reshape(1, K), w)


def _rms_matmul_heads_kernel(x_ref, g_ref, w_ref, *refs, eps, keep_full, head_outs, n_alias, layer, depth):
    o_refs = refs[n_alias:]
    xn = _rms(x_ref[...], g_ref[...], eps).astype(BF16)
    y = jnp.dot(xn, w_ref[...], preferred_element_type=F32)
    if keep_full:
        o_refs[0][...] = y
    for o_ref, (col0, n_heads, width) in zip(o_refs[1 if keep_full else 0:], head_outs):
        if n_alias == 0:
            for other in range(depth):
                if other != layer:
                    o_ref[other] = jnp.zeros(o_ref.shape[1:], F32)
            o_ref = o_ref.at[layer]
        for h in range(n_heads):
            o_ref[:, h, :] = y[:, col0 + h * width:col0 + (h + 1) * width]


def rms_matmul_heads(x, g, w, *, tm, keep_full, head_outs, layer=0, depth=1, stacks=None, eps=1e-6):
    M, K = x.shape
    N = w.shape[1]
    stacks = [] if stacks is None else list(stacks)
    if stacks:
        out_specs = [pl.BlockSpec((None, tm, nh, wd), lambda i: (layer, i, 0, 0)) for _, nh, wd in head_outs]
    else:
        out_specs = [pl.BlockSpec((depth, tm, nh, wd), lambda i: (0, i, 0, 0)) for _, nh, wd in head_outs]
    out_shape = [jax.ShapeDtypeStruct((depth, M, nh, wd), F32) for _, nh, wd in head_outs]
    if keep_full:
        out_specs.insert(0, pl.BlockSpec((tm, N), lambda i: (i, 0)))
        out_shape.insert(0, jax.ShapeDtypeStruct((M, N), F32))
    first_stack_out = 1 if keep_full else 0
    return pl.pallas_call(
        functools.partial(_rms_matmul_heads_kernel, eps=eps, keep_full=keep_full, head_outs=tuple(head_outs),
                          n_alias=len(stacks), layer=layer, depth=depth),
        grid=(M // tm,),
        in_specs=[pl.BlockSpec((tm, K), lambda i: (i, 0)),
                  pl.BlockSpec((1, K), lambda i: (0, 0)),
                  pl.BlockSpec((K, N), lambda i: (0, 0))] + [pl.BlockSpec(memory_space=pl.ANY)] * len(stacks),
        out_specs=out_specs,
        out_shape=out_shape,
        input_output_aliases={3 + j: first_stack_out + j for j in range(len(stacks))},
        compiler_params=_params("parallel"),
        name="rms_matmul_heads",
    )(x, g.reshape(1, K), w, *stacks)


def _to_token_major(o_ref, y):
    rows, n = y.shape
    for c in range(n // LANES):
        o_ref[pl.ds(c, rows, stride=n // LANES), :] = y[:, c * LANES:(c + 1) * LANES]


def _from_token_major(x_ref, n):
    rows = x_ref.shape[0] // (n // LANES)
    return [x_ref[pl.ds(c, rows, stride=n // LANES), :] for c in range(n // LANES)]


def _mm_res_kernel(*refs, n_lhs, token_major):
    a_refs, w_refs = refs[:n_lhs], refs[n_lhs:2 * n_lhs]
    res_ref, o_ref = refs[2 * n_lhs], refs[2 * n_lhs + 1]
    acc = res_ref[...]
    for a_ref, w_ref in zip(a_refs, w_refs):
        acc = acc + jnp.dot(a_ref[...], w_ref[...], preferred_element_type=F32)
    o_ref[...] = acc
    if token_major:
        _to_token_major(refs[2 * n_lhs + 2], acc)


def mm_res(lhs, ws, res, *, tm, token_major=False):
    M, N = res.shape
    n = len(lhs)
    in_specs = ([pl.BlockSpec((tm, a.shape[1]), lambda i: (i, 0)) for a in lhs]
                + [pl.BlockSpec(w.shape, lambda i: (0, 0)) for w in ws]
                + [pl.BlockSpec((tm, N), lambda i: (i, 0))])
    out_specs = [pl.BlockSpec((tm, N), lambda i: (i, 0))]
    out_shape = [jax.ShapeDtypeStruct((M, N), F32)]
    if token_major:
        out_specs.append(pl.BlockSpec((tm * N // LANES, LANES), lambda i: (i, 0)))
        out_shape.append(jax.ShapeDtypeStruct((M * N // LANES, LANES), F32))
    out = pl.pallas_call(
        functools.partial(_mm_res_kernel, n_lhs=n, token_major=token_major),
        grid=(M // tm,),
        in_specs=in_specs,
        out_specs=out_specs,
        out_shape=out_shape,
        compiler_params=_params("parallel"),
        name="mm_res",
    )(*lhs, *ws, res)
    return out if token_major else out[0]


def _bias_kernel(tab_ref, o_ref, *, offsets, rows_per_head):
    h = pl.program_id(0)
    R, C = o_ref.shape[-2], o_ref.shape[-1]
    r = lax.broadcasted_iota(I32, (R, C), 0)
    if rows_per_head != R:
        r = r % rows_per_head
    c = lax.broadcasted_iota(I32, (R, C), 1)
    max_exact = N_BUCKETS // 2
    far = tab_ref[N_BUCKETS - 1, h]
    for kind, off in enumerate(offsets):
        rel = off + r - c
        n = jnp.maximum(rel, 0)
        nf = jnp.maximum(n, max_exact).astype(F32)
        large = max_exact + (jnp.log(nf / max_exact) / math.log(MAX_DISTANCE / max_exact)
                             * (N_BUCKETS - max_exact)).astype(I32)
        bucket = jnp.where(n < max_exact, n, jnp.minimum(large, N_BUCKETS - 1))
        acc = jnp.zeros((R, C), F32)
        for b in range(N_BUCKETS - 1):
            acc = jnp.where(bucket == b, (tab_ref[b, h] - far) * LOG2E, acc)
        o_ref[kind] = jnp.where(rel >= 0, acc, NEG_BIG)


def bias_tiles(rel_bias, *, R, C, offsets, rows_per_head=None):
    rows_per_head = R if rows_per_head is None else rows_per_head
    return pl.pallas_call(
        functools.partial(_bias_kernel, offsets=tuple(offsets), rows_per_head=rows_per_head),
        grid=(DIFF_H,),
        in_specs=[pl.BlockSpec(memory_space=pltpu.SMEM)],
        out_specs=pl.BlockSpec((None, len(offsets), R, C), lambda h: (h, 0, 0, 0)),
        out_shape=jax.ShapeDtypeStruct((DIFF_H, len(offsets), R, C), F32),
        compiler_params=_params("arbitrary"),
        name="bias_tiles",
    )(rel_bias)


def _lambda_value(lamp, lambda_init):
    s1 = jnp.sum(lamp[0:1, :] * lamp[1:2, :], axis=-1, keepdims=True)
    s2 = jnp.sum(lamp[2:3, :] * lamp[3:4, :], axis=-1, keepdims=True)
    return jnp.exp(s1) - jnp.exp(s2) + lambda_init


def _diff_finish(o1, o2, lam, subln, lambda_init):
    o = o1 - lam * o2
    return _rms(o, subln, 1e-5) * (1.0 - lambda_init)


def _diff_attn_kernel(qi_tab, ki_tab, q_ref, k_ref, v_ref, bias_ref, lamp_ref, subln_ref, o_ref,
                      q2_sc, m_sc, acc_sc, *, T, RB, lambda_init):
    t = pl.program_id(2)
    qi = qi_tab[t]
    ki = ki_tab[t]

    @pl.when(ki == 0)
    def _init():
        q = q_ref[...] * (DIFF_DH ** -0.5 * LOG2E)
        lane = lax.broadcasted_iota(I32, q.shape, 1)
        q2_sc[0:T, :] = jnp.where(lane < DIFF_DH, q, 0.0).astype(BF16)
        q2_sc[T:2 * T, :] = jnp.where(lane >= DIFF_DH, q, 0.0).astype(BF16)
        m_sc[...] = jnp.full(m_sc.shape, NEG_BIG, F32)
        acc_sc[...] = jnp.zeros(acc_sc.shape, F32)

    def update(kind):
        kb = k_ref[...].astype(BF16)
        vb = jnp.concatenate([v_ref[...].astype(BF16), jnp.ones((T, DIFF_DV), BF16)], axis=1)
        for r0 in range(0, 2 * T, RB):
            rows = slice(r0, r0 + RB)
            s = lax.dot_general(q2_sc[rows, :], kb, NT_DIMS, preferred_element_type=F32)
            if kind is not None:
                s = s + bias_ref[kind, r0 % T:r0 % T + RB, :]
            cols = [s[:, c * LANES:(c + 1) * LANES] for c in range(T // LANES)]
            m_old = m_sc[rows, :]
            m_new = jnp.maximum(m_old, jnp.max(functools.reduce(jnp.maximum, cols), axis=-1, keepdims=True))
            alpha = jnp.exp2(m_old - m_new)
            p = jnp.concatenate([jnp.exp2(c - m_new) for c in cols], axis=1).astype(BF16)
            acc_sc[rows, :] = (jnp.concatenate([alpha, alpha], axis=1) * acc_sc[rows, :]
                               + jnp.dot(p, vb, preferred_element_type=F32))
            m_sc[rows, :] = m_new

    @pl.when(ki < qi - 1)
    def _far():
        update(None)

    @pl.when(ki == qi - 1)
    def _near():
        update(1)

    @pl.when(ki == qi)
    def _diag():
        update(0)
        on = acc_sc[:, 0:DIFF_DV] / acc_sc[:, DIFF_DV:2 * DIFF_DV]
        lam = _lambda_value(lamp_ref[...], lambda_init)
        o_ref[...] = _diff_finish(on[0:T], on[T:2 * T], lam, subln_ref[...], lambda_init).astype(o_ref.dtype)


def diff_attention_prompt(proj, bias, lamp, subln, *, T, lambda_init):
    B, L, _ = proj.shape
    nq = L // T
    pairs = [(qi, ki) for qi in range(nq) for ki in range(qi + 1)]
    qi_tab = jnp.asarray(np.array([p[0] for p in pairs], np.int32))
    ki_tab = jnp.asarray(np.array([p[1] for p in pairs], np.int32))
    grid_spec = pltpu.PrefetchScalarGridSpec(
        num_scalar_prefetch=2,
        grid=(B, DIFF_H, len(pairs)),
        in_specs=[
            pl.BlockSpec((None, T, DIFF_DV), lambda b, h, t, qt, kt: (b, qt[t], COL_Q + h)),
            pl.BlockSpec((None, T, DIFF_DV), lambda b, h, t, qt, kt: (b, kt[t], COL_K + h)),
            pl.BlockSpec((None, T, DIFF_DV), lambda b, h, t, qt, kt: (b, kt[t], COL_V + h)),
            pl.BlockSpec((None, 2, T, T), lambda b, h, t, qt, kt: (h, 0, 0, 0)),
            pl.BlockSpec((4, DIFF_DH), lambda b, h, t, qt, kt: (0, 0)),
            pl.BlockSpec((1, DIFF_DV), lambda b, h, t, qt, kt: (0, 0)),
        ],
        out_specs=pl.BlockSpec((None, T, DIFF_DV), lambda b, h, t, qt, kt: (b, qt[t], h)),
        scratch_shapes=[pltpu.VMEM((2 * T, DIFF_DV), BF16),
                        pltpu.VMEM((2 * T, LANES), F32),
                        pltpu.VMEM((2 * T, 2 * DIFF_DV), F32)],
    )
    return pl.pallas_call(
        functools.partial(_diff_attn_kernel, T=T, RB=min(T, 128), lambda_init=lambda_init),
        grid_spec=grid_spec,
        out_shape=jax.ShapeDtypeStruct((B, L, DIFF_W), BF16),
        compiler_params=_params("parallel", "parallel", "arbitrary"),
        name="diff_attn_prompt",
    )(qi_tab, ki_tab, proj, proj, proj, bias, lamp, subln.reshape(1, DIFF_DV))


def _diff_decode_kernel(pt_ref, proj_ref, bias_ref, lamp_ref, subln_ref, *rest, n_pages, page, lq, lambda_init):
    k_refs, v_refs = rest[:n_pages], rest[n_pages:2 * n_pages]
    o_ref = rest[2 * n_pages]
    lam = _lambda_value(lamp_ref[...], lambda_init)
    lane = lax.broadcasted_iota(I32, (lq, DIFF_DV), 1)
    pad = jnp.zeros((page - lq, DIFF_DV), F32)
    for h in range(DIFF_H):
        cols = slice(h * DIFF_DV, (h + 1) * DIFF_DV)
        q = proj_ref[:, cols] * (DIFF_DH ** -0.5 * LOG2E)
        qbd = jnp.concatenate([jnp.where(lane < DIFF_DH, q, 0.0), jnp.where(lane >= DIFF_DH, q, 0.0)],
                              axis=0).astype(BF16)
        k_tail = jnp.concatenate([proj_ref[:, DIFF_W + h * DIFF_DV:DIFF_W + (h + 1) * DIFF_DV], pad], 0)
        v_tail = jnp.concatenate([proj_ref[:, 2 * DIFF_W + h * DIFF_DV:2 * DIFF_W + (h + 1) * DIFF_DV], pad], 0)
        head_rows = pl.ds(h, page, stride=DIFF_H)
        parts = [lax.dot_general(qbd, k_refs[j][head_rows, :].astype(BF16), NT_DIMS, preferred_element_type=F32)
                 for j in range(n_pages)]
        parts.append(lax.dot_general(qbd, k_tail.astype(BF16), NT_DIMS, preferred_element_type=F32))
        s = jnp.concatenate(parts, axis=1) + bias_ref[h]
        p = jnp.exp2(s - jnp.max(s, axis=-1, keepdims=True))
        l = jnp.sum(p, axis=-1, keepdims=True)
        pb = p.astype(BF16)
        acc = jnp.dot(pb[:, n_pages * page:], v_tail.astype(BF16), preferred_element_type=F32)
        for j in range(n_pages):
            acc = acc + jnp.dot(pb[:, j * page:(j + 1) * page], v_refs[j][head_rows, :].astype(BF16),
                                preferred_element_type=F32)
        on = acc / l
        o = _diff_finish(on[0:lq], on[lq:2 * lq], lam, subln_ref[...], lambda_init)
        o_ref[:, cols] = o.astype(o_ref.dtype)


def diff_attention_decode(proj, cache_k, cache_v, page_table, bias, lamp, subln, *, layer, lambda_init):
    DB, lq, _ = proj.shape
    n_pages = page_table.shape[1]
    page = cache_k.shape[2] // DIFF_H

    def page_spec(j):
        return pl.BlockSpec((None, None, page * DIFF_H, DIFF_DV), lambda b, pt: (layer, pt[b, j], 0, 0))

    grid_spec = pltpu.PrefetchScalarGridSpec(
        num_scalar_prefetch=1,
        grid=(DB,),
        in_specs=([pl.BlockSpec((None, lq, 3 * DIFF_W), lambda b, pt: (b, 0, 0)),
                   pl.BlockSpec(bias.shape, lambda b, pt: (0, 0, 0)),
                   pl.BlockSpec((4, DIFF_DH), lambda b, pt: (0, 0)),
                   pl.BlockSpec((1, DIFF_DV), lambda b, pt: (0, 0))]
                  + [page_spec(j) for j in range(n_pages)]
                  + [page_spec(j) for j in range(n_pages)]),
        out_specs=pl.BlockSpec((None, lq, DIFF_W), lambda b, pt: (b, 0, 0)),
    )
    return pl.pallas_call(
        functools.partial(_diff_decode_kernel, n_pages=n_pages, page=page, lq=lq, lambda_init=lambda_init),
        grid_spec=grid_spec,
        out_shape=jax.ShapeDtypeStruct((DB, lq, DIFF_W), BF16),
        compiler_params=_params("parallel"),
        name="diff_attn_decode",
    )(page_table, proj, bias, lamp, subln.reshape(1, DIFF_DV),
      *([cache_k] * n_pages), *([cache_v] * n_pages))


def _gla_kernel(qk_ref, v_ref, gg_ref, alr_ref, wa_ref, ba_ref, gn_ref, ex_ref, s0_ref, o_ref, sout_ref, st_sc,
                *, C, SB, n_chunks, G):
    c = pl.program_id(1)

    @pl.when(c == 0)
    def _init():
        st_sc[...] = s0_ref[...]

    for g in range(G):
        _gla_chunk(qk_ref.at[g], v_ref.at[g], gg_ref.at[g], alr_ref.at[g], wa_ref, ba_ref, gn_ref, ex_ref,
                   o_ref.at[g], st_sc.at[g], C=C, SB=SB)

    @pl.when(c == n_chunks - 1)
    def _fin():
        sout_ref[...] = st_sc[...]


def _gla_chunk(qk_ref, v_ref, gg_ref, alr_ref, wa_ref, ba_ref, gn_ref, ex_ref, o_ref, st_sc, *, C, SB):
    z = jnp.dot(alr_ref[...], wa_ref[...], precision=lax.Precision.HIGHEST,
                preferred_element_type=F32) + ba_ref[...]
    logg = (jnp.minimum(z, 0.0) - jnp.log1p(jnp.exp(-jnp.abs(z)))) * (1.0 / GATE_NORM)
    row = lax.broadcasted_iota(I32, logg.shape, 0)
    b = logg
    d = 1
    while d < C:
        b = b + jnp.where(row >= d, pltpu.roll(b, d, 0), 0.0)
        d *= 2
    q_all = qk_ref[:, 0:GLA_H * GLA_DK] * (GLA_DK ** -0.5)
    k_all = qk_ref[:, GLA_H * GLA_DK:2 * GLA_H * GLA_DK]
    v_all = v_ref[...]
    n_sub = C // SB
    hk = lambda a, h: a[:, h * GLA_DK:(h + 1) * GLA_DK]
    hv = lambda a, h: a[:, h * GLA_DV:(h + 1) * GLA_DV]

    b_last = b[C - 1:C, :]
    q_in = (q_all * jnp.exp(b)).astype(BF16)
    k_dec = (k_all * jnp.exp(b_last - b)).astype(BF16)
    e_last = jnp.exp(b_last)
    q_off, k_off = [], []
    for i in range(1, n_sub):
        ref = b[i * SB:i * SB + 1, :]
        q_off.append((q_all[i * SB:(i + 1) * SB] * jnp.exp(b[i * SB:(i + 1) * SB] - ref)).astype(BF16))
        k_off.append((k_all * jnp.exp(jnp.minimum(ref - b, 0.0))).astype(BF16))
    col = lax.broadcasted_iota(I32, (SB, C), 1)

    outs = []
    for h in range(GLA_H):
        vb = hv(v_all, h).astype(BF16)
        st = st_sc[h]
        o = lax.dot_general(hk(q_in, h), st.astype(BF16), NT_DIMS, preferred_element_type=F32)
        if n_sub > 1:
            rows = [jnp.zeros((SB, C), F32)]
            for i in range(1, n_sub):
                a = lax.dot_general(hk(q_off[i - 1], h), hk(k_off[i - 1], h), NT_DIMS,
                                    preferred_element_type=F32)
                rows.append(jnp.where(col < i * SB, a, 0.0))
            a_off = jnp.concatenate(rows, axis=0)
            o = o + jnp.dot(a_off.astype(BF16), vb, preferred_element_type=F32)
        st_sc[h] = st * hk(e_last, h) + lax.dot_general(vb, hk(k_dec, h), TN_DIMS, preferred_element_type=F32)
        outs.append(o)

    rmod = row % SB
    xs = []
    for dlt in range(SB):
        kd, bd = (k_all, b) if dlt == 0 else (pltpu.roll(k_all, dlt, 0), pltpu.roll(b, dlt, 0))
        x = q_all * kd * jnp.exp(jnp.minimum(b - bd, 0.0))
        xs.append(jnp.where(rmod >= dlt, x, 0.0))
    w = jnp.dot(jnp.concatenate(xs, axis=0).astype(BF16), ex_ref[...], preferred_element_type=F32)
    o_diag = w[0:C] * v_all
    for dlt in range(1, SB):
        o_diag = o_diag + w[dlt * C:(dlt + 1) * C] * pltpu.roll(v_all, dlt, 0)

    for h in range(GLA_H):
        gate = _silu(hv(gg_ref[...], h))
        o = outs[h] + hv(o_diag, h)
        o_ref[:, h * GLA_DV:(h + 1) * GLA_DV] = (_rms(o, gn_ref[...], 1e-5) * gate).astype(o_ref.dtype)


def gla(proj, w_a2p, b_a, gla_g, s0t, *, C, SB, G):
    B, L, _ = proj.shape
    n_chunks = L // C
    W = GLA_W
    expand = (jnp.arange(GLA_H * GLA_DK, dtype=I32)[:, None] // GLA_DK
              == jnp.arange(W, dtype=I32)[None, :] // GLA_DV).astype(BF16)
    return pl.pallas_call(
        functools.partial(_gla_kernel, C=C, SB=SB, n_chunks=n_chunks, G=G),
        grid=(B // G, n_chunks),
        in_specs=[
            pl.BlockSpec((G, C, W), lambda b, c: (b, c, COL_QK_GLA)),
            pl.BlockSpec((G, C, W), lambda b, c: (b, c, COL_V_GLA)),
            pl.BlockSpec((G, C, W), lambda b, c: (b, c, COL_G_GLA)),
            pl.BlockSpec((G, C, LANES), lambda b, c: (b, c, COL_ALR)),
            pl.BlockSpec((LANES, GLA_H * GLA_DK), lambda b, c: (0, 0)),
            pl.BlockSpec((1, GLA_H * GLA_DK), lambda b, c: (0, 0)),
            pl.BlockSpec((1, GLA_DV), lambda b, c: (0, 0)),
            pl.BlockSpec((GLA_H * GLA_DK, W), lambda b, c: (0, 0)),
            pl.BlockSpec((G, GLA_H, GLA_DV, GLA_DK), lambda b, c: (b, 0, 0, 0)),
        ],
        out_specs=[pl.BlockSpec((G, C, W), lambda b, c: (b, c, 0)),
                   pl.BlockSpec((G, GLA_H, GLA_DV, GLA_DK), lambda b, c: (b, 0, 0, 0))],
        out_shape=[jax.ShapeDtypeStruct((B, L, W), BF16),
                   jax.ShapeDtypeStruct((B, GLA_H, GLA_DV, GLA_DK), F32)],
        scratch_shapes=[pltpu.VMEM((G, GLA_H, GLA_DV, GLA_DK), F32)],
        compiler_params=_params("parallel", "arbitrary"),
        name="gla",
    )(proj, proj, proj, proj, w_a2p, b_a.reshape(1, -1), gla_g.reshape(1, -1), expand, s0t)


def _xattn_kernel(q_ref, *refs):
    k_refs, v_refs, o_ref = refs[:MEM_H], refs[MEM_H:2 * MEM_H], refs[2 * MEM_H]
    for h in range(MEM_H):
        cols = slice(h * MEM_DH, (h + 1) * MEM_DH)
        s = lax.dot_general(q_ref[:, cols], k_refs[h][...].astype(BF16), NT_DIMS, preferred_element_type=F32)
        m = jnp.max(s, axis=-1, keepdims=True)
        p = jnp.exp(s - m)
        l = jnp.sum(p, axis=-1, keepdims=True)
        o = jnp.dot(p.astype(BF16), v_refs[h][...].astype(BF16), preferred_element_type=F32) / l
        o_ref[:, cols] = o.astype(o_ref.dtype)


def xattn_core(q, mk, mv, k_spec, v_spec, *, tm):
    Bx, Lx, W = q.shape
    return pl.pallas_call(
        _xattn_kernel,
        grid=(Bx, Lx // tm),
        in_specs=([pl.BlockSpec((None, tm, W), lambda b, i: (b, i, 0))]
                  + [k_spec(h) for h in range(MEM_H)] + [v_spec(h) for h in range(MEM_H)]),
        out_specs=pl.BlockSpec((None, tm, W), lambda b, i: (b, i, 0)),
        out_shape=jax.ShapeDtypeStruct((Bx, Lx, W), BF16),
        compiler_params=_params("parallel", "arbitrary"),
        name="xattn_core",
    )(q, *([mk] * MEM_H), *([mv] * MEM_H))


def _xattn_cached_kernel(q_ref, mk_hbm, mv_hbm, o_ref, k_buf, v_buf, sem, *, layer, n_steps, G):
    b = pl.program_id(0)
    slot = b % 2

    def copies(step, buf_slot):
        out = []
        for g in range(G):
            seq = step * G + g
            for h in range(MEM_H):
                out.append(pltpu.make_async_copy(mk_hbm.at[layer, seq, :, h, :], k_buf.at[buf_slot, g, h],
                                                 sem.at[0, buf_slot]))
                out.append(pltpu.make_async_copy(mv_hbm.at[layer, seq, :, h, :], v_buf.at[buf_slot, g, h],
                                                 sem.at[1, buf_slot]))
        return out

    @pl.when(b == 0)
    def _first():
        for c in copies(0, 0):
            c.start()

    @pl.when(b + 1 < n_steps)
    def _prefetch():
        for c in copies(b + 1, 1 - slot):
            c.start()

    for c in copies(b, slot):
        c.wait()
    for g in range(G):
        _xattn_kernel(q_ref.at[g], *[k_buf.at[slot, g, h] for h in range(MEM_H)],
                      *[v_buf.at[slot, g, h] for h in range(MEM_H)], o_ref.at[g])


def xattn_cached(q, mk, mv, *, layer, G):
    DB, LS, W = q.shape
    n_mem = mk.shape[2]
    buf = pltpu.VMEM((2, G, MEM_H, n_mem, MEM_DH), F32)
    return pl.pallas_call(
        functools.partial(_xattn_cached_kernel, layer=layer, n_steps=DB // G, G=G),
        grid=(DB // G,),
        in_specs=[pl.BlockSpec((G, LS, W), lambda b: (b, 0, 0)),
                  pl.BlockSpec(memory_space=pl.ANY), pl.BlockSpec(memory_space=pl.ANY)],
        out_specs=pl.BlockSpec((G, LS, W), lambda b: (b, 0, 0)),
        out_shape=jax.ShapeDtypeStruct((DB, LS, W), BF16),
        scratch_shapes=[buf, buf, pltpu.SemaphoreType.DMA((2, 2))],
        compiler_params=_params("arbitrary"),
        name="xattn_cached",
    )(q, mk, mv)


def _ffn_kernel(te_ref, tv_ref, x_ref, g_ref, wg_ref, wu_ref, wd_ref, sc_ref, o_ref, xn_sc, acc_sc,
                *, n_f, dense):
    i = pl.program_id(0)
    f = pl.program_id(1)

    @pl.when(f == 0)
    def _init():
        if dense:
            xn_sc[...] = _rms(x_ref[...], g_ref[...], 1e-6).astype(BF16)
        else:
            parts = _from_token_major(x_ref, xn_sc.shape[1])
            ms = sum(jnp.sum(p * p, axis=-1, keepdims=True) for p in parts) * (1.0 / xn_sc.shape[1])
            inv = lax.rsqrt(ms + 1e-6)
            for c, p in enumerate(parts):
                cols = slice(c * LANES, (c + 1) * LANES)
                xn_sc[:, cols] = ((p * inv) * g_ref[:, cols]).astype(BF16)
        acc_sc[...] = jnp.zeros(acc_sc.shape, F32)

    @pl.when(tv_ref[i] != 0)
    def _compute():
        xn = xn_sc[...]
        g = jnp.dot(xn, wg_ref[...], preferred_element_type=F32)
        u = jnp.dot(xn, wu_ref[...], preferred_element_type=F32)
        a = (_silu(g) * u).astype(BF16)
        acc_sc[...] += jnp.dot(a, wd_ref[...], preferred_element_type=F32)

    @pl.when(f == n_f - 1)
    def _fin():
        if dense:
            o_ref[...] = x_ref[...] + acc_sc[...]
        else:
            _to_token_major(o_ref, acc_sc[...] * sc_ref[...])


def ffn(x, g, w_gu, w_d, tile_expert, tile_valid, row_scale, *, tm, tf, dense):
    D = w_gu.shape[1]
    Mp = row_scale.shape[0]
    F = w_d.shape[1]
    n_f = F // tf
    last = n_f - 1

    def fblk(i, f, tv):
        return f * tv[i] + last * (1 - tv[i])

    if dense:
        x_spec = pl.BlockSpec((tm, D), lambda i, f, te, tv: (i, 0))
    else:
        x_spec = pl.BlockSpec((tm * D // LANES, LANES), lambda i, f, te, tv: (i, 0))
    grid_spec = pltpu.PrefetchScalarGridSpec(
        num_scalar_prefetch=2,
        grid=(Mp // tm, n_f),
        in_specs=[
            x_spec,
            pl.BlockSpec((1, D), lambda i, f, te, tv: (0, 0)),
            pl.BlockSpec((None, D, tf), lambda i, f, te, tv: (te[i], 0, fblk(i, f, tv))),
            pl.BlockSpec((None, D, tf), lambda i, f, te, tv: (te[i], 0, n_f + fblk(i, f, tv))),
            pl.BlockSpec((None, tf, D), lambda i, f, te, tv: (te[i], fblk(i, f, tv), 0)),
            pl.BlockSpec((tm, 1), lambda i, f, te, tv: (i, 0)),
        ],
        out_specs=x_spec,
        scratch_shapes=[pltpu.VMEM((tm, D), BF16), pltpu.VMEM((tm, D), F32)],
    )
    return pl.pallas_call(
        functools.partial(_ffn_kernel, n_f=n_f, dense=dense),
        grid_spec=grid_spec,
        out_shape=jax.ShapeDtypeStruct(x.shape, F32),
        compiler_params=_params("parallel", "arbitrary"),
        name="ffn_dense" if dense else "ffn_grouped",
    )(tile_expert, tile_valid, x, g.reshape(1, D), w_gu, w_gu, w_d, row_scale)


def _router_kernel(x_ref, g_ref, wr_ref, idx_ref, gate_ref):
    xn = _rms(x_ref[...], g_ref[...], 1e-6)
    logits = jnp.dot(xn, wr_ref[...], precision=lax.Precision.HIGHEST, preferred_element_type=F32)
    lane = lax.broadcasted_iota(I32, logits.shape, 1)
    real = lane < N_EXPERTS
    logits = jnp.where(real, logits, NEG_BIG)
    e = jnp.exp(logits - jnp.max(logits, axis=-1, keepdims=True))
    probs = jnp.where(real, e / jnp.sum(e, axis=-1, keepdims=True), -1.0)
    v1 = jnp.max(probs, axis=-1, keepdims=True)
    i1 = jnp.min(jnp.where(probs == v1, lane, LANES), axis=-1, keepdims=True)
    rest = jnp.where(lane == i1, -1.0, probs)
    v2 = jnp.max(rest, axis=-1, keepdims=True)
    i2 = jnp.min(jnp.where(rest == v2, lane, LANES), axis=-1, keepdims=True)
    den = v1 + v2
    idx_ref[...] = jnp.where(lane == 0, i1, jnp.where(lane == 1, i2, 0))
    gate_ref[...] = jnp.where(lane == 0, v1 / den, jnp.where(lane == 1, v2 / den, 0.0))


def router(x, g, w_router_pad, *, tm):
    M, D = x.shape
    return pl.pallas_call(
        _router_kernel,
        grid=(M // tm,),
        in_specs=[pl.BlockSpec((tm, D), lambda i: (i, 0)),
                  pl.BlockSpec((1, D), lambda i: (0, 0)),
                  pl.BlockSpec((D, LANES), lambda i: (0, 0))],
        out_specs=[pl.BlockSpec((tm, LANES), lambda i: (i, 0)),
                   pl.BlockSpec((tm, LANES), lambda i: (i, 0))],
        out_shape=[jax.ShapeDtypeStruct((M, LANES), I32), jax.ShapeDtypeStruct((M, LANES), F32)],
        compiler_params=_params("parallel"),
        name="router",
    )(x, g.reshape(1, D), w_router_pad)


def _row_copy(src_hbm, dst_ref, src_row, dst_row, sem):
    return pltpu.make_async_copy(src_hbm.at[pl.ds(pl.multiple_of(src_row * SUBLANES, SUBLANES), SUBLANES), :],
                                 dst_ref.at[pl.ds(pl.multiple_of(dst_row * SUBLANES, SUBLANES), SUBLANES), :], sem)


def _gather_kernel(idx_ref, src_hbm, o_ref, sem, *, R):
    def start(j, carry):
        r = 2 * j
        _row_copy(src_hbm, o_ref, idx_ref[0, r], r, sem).start(priority=0)
        _row_copy(src_hbm, o_ref, idx_ref[0, r + 1], r + 1, sem).start(priority=1)
        return carry

    lax.fori_loop(0, R // 2, start, 0, unroll=4)
    pltpu.make_async_copy(src_hbm.at[pl.ds(0, R * SUBLANES), :], o_ref, sem).wait()


def gather_rows(src, idx, *, R):
    Mp = idx.shape[0]
    return pl.pallas_call(
        functools.partial(_gather_kernel, R=R),
        grid=(Mp // R,),
        in_specs=[pl.BlockSpec((None, 1, R), lambda i: (i, 0, 0), memory_space=pltpu.SMEM),
                  pl.BlockSpec(memory_space=pl.ANY)],
        out_specs=pl.BlockSpec((R * SUBLANES, LANES), lambda i: (i, 0)),
        out_shape=jax.ShapeDtypeStruct((Mp * SUBLANES, LANES), src.dtype),
        scratch_shapes=[pltpu.SemaphoreType.DMA(())],
        compiler_params=_params("arbitrary"),
        name="gather_rows",
    )(idx.reshape(Mp // R, 1, R), src)


def _combine_kernel(pos_ref, h_ref, y_hbm, g_ref, o_ref, a_sc, b_sc, sem, *, R, final_norm):
    def start(r, carry):
        _row_copy(y_hbm, a_sc, pos_ref[0, 2 * r], r, sem.at[0]).start(priority=0)
        _row_copy(y_hbm, b_sc, pos_ref[0, 2 * r + 1], r, sem.at[1]).start(priority=1)
        return carry

    lax.fori_loop(0, R, start, 0, unroll=4)
    pltpu.make_async_copy(y_hbm.at[pl.ds(0, R * SUBLANES), :], a_sc, sem.at[0]).wait()
    pltpu.make_async_copy(y_hbm.at[pl.ds(0, R * SUBLANES), :], b_sc, sem.at[1]).wait()
    n = h_ref.shape[1]
    moe = jnp.concatenate([a + b for a, b in zip(_from_token_major(a_sc, n), _from_token_major(b_sc, n))], axis=1)
    out = h_ref[...] + moe
    if final_norm:
        out = _rms(out, g_ref[...], 1e-6)
    o_ref[...] = out


def combine(h, y, pos, g, *, R, final_norm):
    M, D = h.shape
    return pl.pallas_call(
        functools.partial(_combine_kernel, R=R, final_norm=final_norm),
        grid=(M // R,),
        in_specs=[pl.BlockSpec((None, 1, 2 * R), lambda i: (i, 0, 0), memory_space=pltpu.SMEM),
                  pl.BlockSpec((R, D), lambda i: (i, 0)),
                  pl.BlockSpec(memory_space=pl.ANY),
                  pl.BlockSpec((1, D), lambda i: (0, 0))],
        out_specs=pl.BlockSpec((R, D), lambda i: (i, 0)),
        out_shape=jax.ShapeDtypeStruct((M, D), F32),
        scratch_shapes=[pltpu.VMEM((R * SUBLANES, LANES), F32), pltpu.VMEM((R * SUBLANES, LANES), F32),
                        pltpu.SemaphoreType.DMA((2,))],
        compiler_params=_params("arbitrary"),
        name="combine",
    )(pos.reshape(M // R, 1, 2 * R), h, y, g.reshape(1, D))


def _rms_only_kernel(x_ref, g_ref, o_ref):
    o_ref[...] = _rms(x_ref[...], g_ref[...], 1e-6)


def rms_only(x, g, *, tm):
    M, D = x.shape
    return pl.pallas_call(
        _rms_only_kernel,
        grid=(M // tm,),
        in_specs=[pl.BlockSpec((tm, D), lambda i: (i, 0)), pl.BlockSpec((1, D), lambda i: (0, 0))],
        out_specs=pl.BlockSpec((tm, D), lambda i: (i, 0)),
        out_shape=jax.ShapeDtypeStruct((M, D), F32),
        compiler_params=_params("parallel"),
        name="rms_only",
    )(x, g.reshape(1, D))


def moe_ffn(h, h_tok, g_norm, w_router_pad, w_gu, w_d, g_final, *, tm, r_gather, r_combine, final_norm):
    M, D = h.shape
    idx_p, gate_p = router(h, g_norm, w_router_pad, tm=min(512, M))
    expert = idx_p[:, :TOP_K].reshape(-1)
    gate = gate_p[:, :TOP_K].reshape(-1)
    n_pairs = TOP_K * M
    n_tiles = n_pairs // tm + N_EXPERTS
    Mp = n_tiles * tm
    onehot = (expert[:, None] == jnp.arange(N_EXPERTS, dtype=I32)[None, :]).astype(I32)
    csum = jnp.cumsum(onehot, axis=0)
    counts = csum[-1]
    tiles_per = (counts + tm - 1) // tm
    tile_end = jnp.cumsum(tiles_per)
    tile_start = tile_end - tiles_per
    group_start = jnp.cumsum(counts) - counts
    rank = jnp.sum(onehot * (csum - 1), axis=1)
    slot_of_pair = jnp.sum(onehot * tile_start[None, :], axis=1) * tm + rank
    _, sorted_pair, sorted_gate = lax.sort((expert, jnp.arange(n_pairs, dtype=I32), gate), num_keys=1,
                                           is_stable=True)
    tile_ids = jnp.arange(n_tiles, dtype=I32)
    tile_valid = (tile_ids < tile_end[-1]).astype(I32)
    tile_expert = jnp.minimum(jnp.sum((tile_ids[:, None] >= tile_end[None, :]).astype(I32), axis=1),
                              N_EXPERTS - 1)
    within = (tile_ids - tile_start[tile_expert]) * tm
    pos = within[:, None] + jnp.arange(tm, dtype=I32)[None, :]
    live = (tile_valid[:, None] != 0) & (pos < counts[tile_expert][:, None])
    src = jnp.clip(group_start[tile_expert][:, None] + pos, 0, n_pairs - 1).reshape(Mp)
    live = live.reshape(Mp)
    token_of_slot = jnp.where(live, sorted_pair[src] // TOP_K, 0)
    scale_of_slot = jnp.where(live, sorted_gate[src], 0.0)
    last_expert = tile_expert[jnp.maximum(tile_end[-1] - 1, 0)]
    tile_expert = jnp.where(tile_valid != 0, tile_expert, last_expert)

    x_sorted = gather_rows(h_tok, token_of_slot, R=r_gather)
    y_sorted = ffn(x_sorted, g_norm, w_gu, w_d, tile_expert, tile_valid, scale_of_slot.reshape(Mp, 1),
                   tm=tm, tf=w_d.shape[1] // 2, dense=False)
    return combine(h, y_sorted, slot_of_pair, g_final, R=r_combine, final_norm=final_norm)


def kernel(x_prompt, x_sample, mem_prompt, cache_attn_k, cache_attn_v, cache_mem_k, cache_mem_v, state_gla,
           page_table, rel_bias, norm_mix, w_in, w_gla_a2, b_gla_a, gla_norm, diff_subln, lambda_q1, lambda_k1,
           lambda_q2, lambda_k2, w_out, norm_mem, norm_memkv, w_mq, w_mkv, w_mo, norm_ffn, w_ffn_gu,
           w_ffn_down, w_router, w_exp_gu, w_exp_down, norm_final):
    B, L, D = x_prompt.shape
    DB, LS, _ = x_sample.shape
    depth = w_in.shape[0]
    n_mem = mem_prompt.shape[1]
    n_pages, page = page_table.shape[1], cache_attn_k.shape[2]
    past_len = n_pages * page
    M, MS = B * L, DB * LS
    T_ATT = 512
    GLA_C = 64

    bias_p = bias_tiles(rel_bias, R=T_ATT, C=T_ATT, offsets=(0, T_ATT))
    bias_s = bias_tiles(rel_bias, R=2 * LS, C=past_len + page, offsets=(past_len,),
                        rows_per_head=LS).reshape(DIFF_H, 2 * LS, past_len + page)
    cache_k = cache_attn_k.reshape(depth, -1, page * DIFF_H, DIFF_DV)
    cache_v = cache_attn_v.reshape(depth, -1, page * DIFF_H, DIFF_DV)
    kv_heads = [(DIFF_W, DIFF_H, DIFF_DV), (2 * DIFF_W, DIFF_H, DIFF_DV)]
    mem_heads = [(0, MEM_H, MEM_DH), (MEM_H * MEM_DH, MEM_H, MEM_DH)]
    w_router_pad = jnp.pad(w_router, ((0, 0), (0, 0), (0, LANES - N_EXPERTS)))

    hp = x_prompt.reshape(M, D)
    hs = x_sample.reshape(MS, D)
    ps, pmk, pmv, ss = [], [], [], []
    kv_p = kv_s = None
    for l in range(depth):
        lambda_init = 0.8 - 0.6 * math.exp(-0.3 * l)
        w_in_p = jnp.pad(w_in[l], ((0, 0), (0, N_IN_PAD - N_IN))).astype(BF16)
        w_a2p = jnp.pad(w_gla_a2[l], ((0, LANES - GLA_LR), (0, 0)))
        lamp = jnp.stack([lambda_q1[l], lambda_k1[l], lambda_q2[l], lambda_k2[l]])
        w_out_a = w_out[l, :DIFF_W].astype(BF16)
        w_out_g = w_out[l, DIFF_W:].astype(BF16)
        w_mq_b, w_mo_b, w_mkv_b = w_mq[l].astype(BF16), w_mo[l].astype(BF16), w_mkv[l].astype(BF16)
        last = l == depth - 1

        proj_p, *kv_p = rms_matmul_heads(hp, norm_mix[l], w_in_p, tm=512, keep_full=True, head_outs=kv_heads,
                                         layer=l, depth=depth, stacks=kv_p)
        proj_s, *kv_s = rms_matmul_heads(hs, norm_mix[l], w_in_p, tm=512, keep_full=True, head_outs=kv_heads,
                                         layer=l, depth=depth, stacks=kv_s)
        proj_p = proj_p.reshape(B, L, N_IN_PAD)
        proj_s = proj_s.reshape(DB, LS, N_IN_PAD)
        att_p = diff_attention_prompt(proj_p, bias_p, lamp, diff_subln[l], T=T_ATT, lambda_init=lambda_init)
        att_s = diff_attention_decode(proj_s, cache_k, cache_v, page_table, bias_s, lamp, diff_subln[l],
                                      layer=l, lambda_init=lambda_init)
        zero_state = jnp.zeros((B, GLA_H, GLA_DV, GLA_DK), F32)
        gla_p, st_p = gla(proj_p, w_a2p, b_gla_a[l], gla_norm[l], zero_state, C=GLA_C, SB=16, G=B)
        gla_s, st_s = gla(proj_s, w_a2p, b_gla_a[l], gla_norm[l], jnp.swapaxes(state_gla[l], -1, -2),
                          C=LS, SB=LS, G=8)
        hp = mm_res([att_p.reshape(M, DIFF_W), gla_p.reshape(M, GLA_W)], [w_out_a, w_out_g], hp, tm=512)
        hs = mm_res([att_s.reshape(MS, DIFF_W), gla_s.reshape(MS, GLA_W)], [w_out_a, w_out_g], hs, tm=512)

        mkv_p, mk_p, mv_p = rms_matmul_heads(mem_prompt.reshape(B * n_mem, D), norm_memkv[l], w_mkv_b, tm=512,
                                             keep_full=True, head_outs=mem_heads)
        mkv_p = mkv_p.reshape(B, n_mem, 2 * MEM_H * MEM_DH)
        mk_p = mk_p.reshape(B, n_mem, MEM_H, MEM_DH)
        mv_p = mv_p.reshape(B, n_mem, MEM_H, MEM_DH)
        q_p = rms_matmul(hp, norm_mem[l], w_mq_b, tm=512, out_dtype=BF16, scale=MEM_DH ** -0.5)
        q_s = rms_matmul(hs, norm_mem[l], w_mq_b, tm=512, out_dtype=BF16, scale=MEM_DH ** -0.5)
        xo_p = xattn_core(
            q_p.reshape(B, L, D), mkv_p, mkv_p,
            lambda h: pl.BlockSpec((None, n_mem, MEM_DH), lambda b, i: (b, 0, h)),
            lambda h: pl.BlockSpec((None, n_mem, MEM_DH), lambda b, i: (b, 0, MEM_H + h)), tm=512)
        xo_s = xattn_cached(q_s.reshape(DB, LS, D), cache_mem_k, cache_mem_v, layer=l, G=2)
        moe_layer = l % 2 == 1
        hp = mm_res([xo_p.reshape(M, D)], [w_mo_b], hp, tm=512, token_major=moe_layer)
        hs = mm_res([xo_s.reshape(MS, D)], [w_mo_b], hs, tm=512, token_major=moe_layer)
        if moe_layer:
            (hp, hp_tok), (hs, hs_tok) = hp, hs

        if l % 2 == 0:
            w_gu = w_ffn_gu[l // 2].astype(BF16)[None]
            w_d = w_ffn_down[l // 2].astype(BF16)[None]
            for_dense = lambda h, tm: ffn(
                h, norm_ffn[l], w_gu, w_d, jnp.zeros((h.shape[0] // tm,), I32),
                jnp.ones((h.shape[0] // tm,), I32), jnp.ones((h.shape[0], 1), F32),
                tm=tm, tf=w_d.shape[1] // 2, dense=True)
            hp, hs = for_dense(hp, 512), for_dense(hs, 512)
            if last:
                hp, hs = rms_only(hp, norm_final, tm=512), rms_only(hs, norm_final, tm=512)
        else:
            w_gu = w_exp_gu[l // 2].astype(BF16)
            w_d = w_exp_down[l // 2].astype(BF16)
            hp = moe_ffn(hp, hp_tok, norm_ffn[l], w_router_pad[l // 2], w_gu, w_d, norm_final,
                         tm=512, r_gather=1024, r_combine=512, final_norm=last)
            hs = moe_ffn(hs, hs_tok, norm_ffn[l], w_router_pad[l // 2], w_gu, w_d, norm_final,
                         tm=256, r_gather=1024, r_combine=512, final_norm=last)

        ps.append(jnp.swapaxes(st_p, -1, -2))
        pmk.append(mk_p)
        pmv.append(mv_p)
        ss.append(jnp.swapaxes(st_s, -1, -2))

    pk, pv = (a.reshape(depth, B, L, DIFF_H, DIFF_DV) for a in kv_p)
    sk, sv = (a.reshape(depth, DB, LS, DIFF_H, DIFF_DV) for a in kv_s)
    return (hp.reshape(B, L, D), hs.reshape(DB, LS, D), pk, pv, jnp.stack(ps),
            jnp.stack(pmk), jnp.stack(pmv), sk, sv, jnp.stack(ss))
```

```python
import functools
import math

import numpy as np
import jax
import jax.numpy as jnp
from jax import lax
from jax.experimental import pallas as pl
from jax.experimental.pallas import tpu as pltpu

F32 = jnp.float32
BF16 = jnp.bfloat16
I32 = jnp.int32

LANES = 128
SUBLANES = 8
VMEM_LIMIT_BYTES = 56 * 1024 * 1024

D_MODEL = 1024
DIFF_H = 4
DIFF_DV = 128
DIFF_DH = 64
DIFF_W = DIFF_H * DIFF_DV
GLA_H = 4
GLA_DK = 64
GLA_DV = 128
GLA_W = GLA_H * GLA_DV
GLA_LR = 16
GATE_NORM = 16.0
N_BUCKETS = 32
MAX_DISTANCE = 128
MEM_H = 4
MEM_DH = 256
N_EXPERTS = 8
TOP_K = 2
N_IN = 3 * DIFF_W + 2 * GLA_H * GLA_DK + 2 * GLA_W + GLA_LR
N_IN_PAD = 3200
NEG_BIG = -1e30
LOG2E = math.log2(math.e)

COL_Q = 0
COL_K = 4
COL_V = 8
COL_QK_GLA = 3
COL_V_GLA = 4
COL_G_GLA = 5
COL_ALR = 24

NT_DIMS = (((1,), (1,)), ((), ()))
TN_DIMS = (((0,), (0,)), ((), ()))


def _params(*sem):
    return pltpu.CompilerParams(dimension_semantics=sem, vmem_limit_bytes=VMEM_LIMIT_BYTES)


def _rms(x, g, eps):
    ms = jnp.mean(x * x, axis=-1, keepdims=True)
    return (x * lax.rsqrt(ms + eps)) * g


def _silu(x):
    return x / (1.0 + jnp.exp(-x))


def _rms_matmul_kernel(x_ref, g_ref, w_ref, o_ref, *, eps, scale):
    xn = _rms(x_ref[...], g_ref[...], eps).astype(BF16)
    y = jnp.dot(xn, w_ref[...], preferred_element_type=F32)
    if scale != 1.0:
        y = y * scale
    o_ref[...] = y.astype(o_ref.dtype)


def rms_matmul(x, g, w, *, tm, out_dtype=F32, scale=1.0, eps=1e-6):
    M, K = x.shape
    N = w.shape[1]
    return pl.pallas_call(
        functools.partial(_rms_matmul_kernel, eps=eps, scale=scale),
        grid=(M // tm,),
        in_specs=[pl.BlockSpec((tm, K), lambda i: (i, 0)),
                  pl.BlockSpec((1, K), lambda i: (0, 0)),
                  pl.BlockSpec((K, N), lambda i: (0, 0))],
        out_specs=pl.BlockSpec((tm, N), lambda i: (i, 0)),
        out_shape=jax.ShapeDtypeStruct((M, N), out_dtype),
        compiler_params=_params("parallel"),
        name="rms_matmul",
    )(x, g.reshape(1, K), w)


def _rms_matmul_heads_kernel(x_ref, g_ref, w_ref, *refs, eps, keep_full, head_outs, n_alias, layer, depth):
    o_refs = refs[n_alias:]
    xn = _rms(x_ref[...], g_ref[...], eps).astype(BF16)
    y = jnp.dot(xn, w_ref[...], preferred_element_type=F32)
    if keep_full:
        o_refs[0][...] = y
    for o_ref, (col0, n_heads, width) in zip(o_refs[1 if keep_full else 0:], head_outs):
        if n_alias == 0:
            for other in range(depth):
                if other != layer:
                    o_ref[other] = jnp.zeros(o_ref.shape[1:], F32)
            o_ref = o_ref.at[layer]
        for h in range(n_heads):
            o_ref[:, h, :] = y[:, col0 + h * width:col0 + (h + 1) * width]


def rms_matmul_heads(x, g, w, *, tm, keep_full, head_outs, layer=0, depth=1, stacks=None, eps=1e-6):
    M, K = x.shape
    N = w.shape[1]
    stacks = [] if stacks is None else list(stacks)
    if stacks:
        out_specs = [pl.BlockSpec((None, tm, nh, wd), lambda i: (layer, i, 0, 0)) for _, nh, wd in head_outs]
    else:
        out_specs = [pl.BlockSpec((depth, tm, nh, wd), lambda i: (0, i, 0, 0)) for _, nh, wd in head_outs]
    out_shape = [jax.ShapeDtypeStruct((depth, M, nh, wd), F32) for _, nh, wd in head_outs]
    if keep_full:
        out_specs.insert(0, pl.BlockSpec((tm, N), lambda i: (i, 0)))
        out_shape.insert(0, jax.ShapeDtypeStruct((M, N), F32))
    first_stack_out = 1 if keep_full else 0
    return pl.pallas_call(
        functools.partial(_rms_matmul_heads_kernel, eps=eps, keep_full=keep_full, head_outs=tuple(head_outs),
                          n_alias=len(stacks), layer=layer, depth=depth),
        grid=(M // tm,),
        in_specs=[pl.BlockSpec((tm, K), lambda i: (i, 0)),
                  pl.BlockSpec((1, K), lambda i: (0, 0)),
                  pl.BlockSpec((K, N), lambda i: (0, 0))] + [pl.BlockSpec(memory_space=pl.ANY)] * len(stacks),
        out_specs=out_specs,
        out_shape=out_shape,
        input_output_aliases={3 + j: first_stack_out + j for j in range(len(stacks))},
        compiler_params=_params("parallel"),
        name="rms_matmul_heads",
    )(x, g.reshape(1, K), w, *stacks)


def _to_token_major(o_ref, y):
    rows, n = y.shape
    for c in range(n // LANES):
        o_ref[pl.ds(c, rows, stride=n // LANES), :] = y[:, c * LANES:(c + 1) * LANES]


def _from_token_major(x_ref, n):
    rows = x_ref.shape[0] // (n // LANES)
    return [x_ref[pl.ds(c, rows, stride=n // LANES), :] for c in range(n // LANES)]


def _mm_res_kernel(*refs, n_lhs, token_major):
    a_refs, w_refs = refs[:n_lhs], refs[n_lhs:2 * n_lhs]
    res_ref, o_ref = refs[2 * n_lhs], refs[2 * n_lhs + 1]
    acc = res_ref[...]
    for a_ref, w_ref in zip(a_refs, w_refs):
        acc = acc + jnp.dot(a_ref[...], w_ref[...], preferred_element_type=F32)
    o_ref[...] = acc
    if token_major:
        _to_token_major(refs[2 * n_lhs + 2], acc)


def mm_res(lhs, ws, res, *, tm, token_major=False):
    M, N = res.shape
    n = len(lhs)
    in_specs = ([pl.BlockSpec((tm, a.shape[1]), lambda i: (i, 0)) for a in lhs]
                + [pl.BlockSpec(w.shape, lambda i: (0, 0)) for w in ws]
                + [pl.BlockSpec((tm, N), lambda i: (i, 0))])
    out_specs = [pl.BlockSpec((tm, N), lambda i: (i, 0))]
    out_shape = [jax.ShapeDtypeStruct((M, N), F32)]
    if token_major:
        out_specs.append(pl.BlockSpec((tm * N // LANES, LANES), lambda i: (i, 0)))
        out_shape.append(jax.ShapeDtypeStruct((M * N // LANES, LANES), F32))
    out = pl.pallas_call(
        functools.partial(_mm_res_kernel, n_lhs=n, token_major=token_major),
        grid=(M // tm,),
        in_specs=in_specs,
        out_specs=out_specs,
        out_shape=out_shape,
        compiler_params=_params("parallel"),
        name="mm_res",
    )(*lhs, *ws, res)
    return out if token_major else out[0]


def _bias_kernel(tab_ref, o_ref, *, offsets, rows_per_head):
    h = pl.program_id(0)
    R, C = o_ref.shape[-2], o_ref.shape[-1]
    r = lax.broadcasted_iota(I32, (R, C), 0)
    if rows_per_head != R:
        r = r % rows_per_head
    c = lax.broadcasted_iota(I32, (R, C), 1)
    max_exact = N_BUCKETS // 2
    far = tab_ref[N_BUCKETS - 1, h]
    for kind, off in enumerate(offsets):
        rel = off + r - c
        n = jnp.maximum(rel, 0)
        nf = jnp.maximum(n, max_exact).astype(F32)
        large = max_exact + (jnp.log(nf / max_exact) / math.log(MAX_DISTANCE / max_exact)
                             * (N_BUCKETS - max_exact)).astype(I32)
        bucket = jnp.where(n < max_exact, n, jnp.minimum(large, N_BUCKETS - 1))
        acc = jnp.zeros((R, C), F32)
        for b in range(N_BUCKETS - 1):
            acc = jnp.where(bucket == b, (tab_ref[b, h] - far) * LOG2E, acc)
        o_ref[kind] = jnp.where(rel >= 0, acc, NEG_BIG)


def bias_tiles(rel_bias, *, R, C, offsets, rows_per_head=None):
    rows_per_head = R if rows_per_head is None else rows_per_head
    return pl.pallas_call(
        functools.partial(_bias_kernel, offsets=tuple(offsets), rows_per_head=rows_per_head),
        grid=(DIFF_H,),
        in_specs=[pl.BlockSpec(memory_space=pltpu.SMEM)],
        out_specs=pl.BlockSpec((None, len(offsets), R, C), lambda h: (h, 0, 0, 0)),
        out_shape=jax.ShapeDtypeStruct((DIFF_H, len(offsets), R, C), F32),
        compiler_params=_params("arbitrary"),
        name="bias_tiles",
    )(rel_bias)


def _lambda_value(lamp, lambda_init):
    s1 = jnp.sum(lamp[0:1, :] * lamp[1:2, :], axis=-1, keepdims=True)
    s2 = jnp.sum(lamp[2:3, :] * lamp[3:4, :], axis=-1, keepdims=True)
    return jnp.exp(s1) - jnp.exp(s2) + lambda_init


def _diff_finish(o1, o2, lam, subln, lambda_init):
    o = o1 - lam * o2
    return _rms(o, subln, 1e-5) * (1.0 - lambda_init)


def _diff_attn_kernel(qi_tab, ki_tab, q_ref, k_ref, v_ref, bias_ref, lamp_ref, subln_ref, o_ref,
                      q2_sc, m_sc, acc_sc, *, TQ, TK, RB, lambda_init):
    t = pl.program_id(2)
    qi = qi_tab[t]
    ki = ki_tab[t]
    n = TQ // TK
    kind_now = ki - qi * n + 1

    @pl.when(ki == 0)
    def _init():
        q = q_ref[...] * (DIFF_DH ** -0.5 * LOG2E)
        lane = lax.broadcasted_iota(I32, q.shape, 1)
        q2_sc[0:TQ, :] = jnp.where(lane < DIFF_DH, q, 0.0).astype(BF16)
        q2_sc[TQ:2 * TQ, :] = jnp.where(lane >= DIFF_DH, q, 0.0).astype(BF16)
        m_sc[...] = jnp.full(m_sc.shape, NEG_BIG, F32)
        acc_sc[...] = jnp.zeros(acc_sc.shape, F32)

    def update(kind):
        kb = k_ref[...].astype(BF16)
        vb = jnp.concatenate([v_ref[...].astype(BF16), jnp.ones((TK, DIFF_DV), BF16)], axis=1)
        for r0 in range(0, 2 * TQ, RB):
            rows = slice(r0, r0 + RB)
            q_row = r0 % TQ
            n_k = TK if kind is None else max(0, min(TK, TK - kind * TK + q_row + RB))
            if n_k == 0:
                continue
            s = lax.dot_general(q2_sc[rows, :], kb[0:n_k], NT_DIMS, preferred_element_type=F32)
            if kind is not None:
                s = s + bias_ref[kind, q_row:q_row + RB, 0:n_k]
            cols = [s[:, c * LANES:(c + 1) * LANES] for c in range(n_k // LANES)]
            m_old = m_sc[rows, :]
            m_new = jnp.maximum(m_old, jnp.max(functools.reduce(jnp.maximum, cols), axis=-1, keepdims=True))
            alpha = jnp.exp2(m_old - m_new)
            p = jnp.concatenate([jnp.exp2(c - m_new) for c in cols], axis=1).astype(BF16)
            acc_sc[rows, :] = (jnp.concatenate([alpha, alpha], axis=1) * acc_sc[rows, :]
                               + jnp.dot(p, vb[0:n_k], preferred_element_type=F32))
            m_sc[rows, :] = m_new

    @pl.when(kind_now < 0)
    def _far():
        update(None)

    for kind in range(n + 1):
        @pl.when(kind_now == kind)
        def _near(kind=kind):
            update(kind)
            if kind == n:
                on = acc_sc[:, 0:DIFF_DV] / acc_sc[:, DIFF_DV:2 * DIFF_DV]
                lam = _lambda_value(lamp_ref[...], lambda_init)
                o_ref[...] = _diff_finish(on[0:TQ], on[TQ:2 * TQ], lam, subln_ref[...],
                                          lambda_init).astype(o_ref.dtype)


def diff_attention_prompt(proj, bias, lamp, subln, *, TQ, TK, lambda_init):
    B, L, _ = proj.shape
    n = TQ // TK
    pairs = [(qi, ki) for qi in range(L // TQ) for ki in range((qi + 1) * n)]
    qi_tab = jnp.asarray(np.array([p[0] for p in pairs], np.int32))
    ki_tab = jnp.asarray(np.array([p[1] for p in pairs], np.int32))
    grid_spec = pltpu.PrefetchScalarGridSpec(
        num_scalar_prefetch=2,
        grid=(B, DIFF_H, len(pairs)),
        in_specs=[
            pl.BlockSpec((None, TQ, DIFF_DV), lambda b, h, t, qt, kt: (b, qt[t], COL_Q + h)),
            pl.BlockSpec((None, TK, DIFF_DV), lambda b, h, t, qt, kt: (b, kt[t], COL_K + h)),
            pl.BlockSpec((None, TK, DIFF_DV), lambda b, h, t, qt, kt: (b, kt[t], COL_V + h)),
            pl.BlockSpec((None, n + 1, TQ, TK), lambda b, h, t, qt, kt: (h, 0, 0, 0)),
            pl.BlockSpec((4, DIFF_DH), lambda b, h, t, qt, kt: (0, 0)),
            pl.BlockSpec((1, DIFF_DV), lambda b, h, t, qt, kt: (0, 0)),
        ],
        out_specs=pl.BlockSpec((None, TQ, DIFF_DV), lambda b, h, t, qt, kt: (b, qt[t], h)),
        scratch_shapes=[pltpu.VMEM((2 * TQ, DIFF_DV), BF16),
                        pltpu.VMEM((2 * TQ, LANES), F32),
                        pltpu.VMEM((2 * TQ, 2 * DIFF_DV), F32)],
    )
    return pl.pallas_call(
        functools.partial(_diff_attn_kernel, TQ=TQ, TK=TK, RB=LANES, lambda_init=lambda_init),
        grid_spec=grid_spec,
        out_shape=jax.ShapeDtypeStruct((B, L, DIFF_W), BF16),
        compiler_params=_params("parallel", "parallel", "arbitrary"),
        name="diff_attn_prompt",
    )(qi_tab, ki_tab, proj, proj, proj, bias, lamp, subln.reshape(1, DIFF_DV))


def _diff_decode_kernel(pt_ref, proj_ref, bias_ref, lamp_ref, subln_ref, *rest, n_pages, page, lq, lambda_init):
    k_refs, v_refs = rest[:n_pages], rest[n_pages:2 * n_pages]
    o_ref = rest[2 * n_pages]
    lam = _lambda_value(lamp_ref[...], lambda_init)
    lane = lax.broadcasted_iota(I32, (lq, DIFF_DV), 1)
    pad = jnp.zeros((page - lq, DIFF_DV), F32)
    for h in range(DIFF_H):
        cols = slice(h * DIFF_DV, (h + 1) * DIFF_DV)
        q = proj_ref[:, cols] * (DIFF_DH ** -0.5 * LOG2E)
        qbd = jnp.concatenate([jnp.where(lane < DIFF_DH, q, 0.0), jnp.where(lane >= DIFF_DH, q, 0.0)],
                              axis=0).astype(BF16)
        k_tail = jnp.concatenate([proj_ref[:, DIFF_W + h * DIFF_DV:DIFF_W + (h + 1) * DIFF_DV], pad], 0)
        v_tail = jnp.concatenate([proj_ref[:, 2 * DIFF_W + h * DIFF_DV:2 * DIFF_W + (h + 1) * DIFF_DV], pad], 0)
        head_rows = pl.ds(h, page, stride=DIFF_H)
        parts = [lax.dot_general(qbd, k_refs[j][head_rows, :].astype(BF16), NT_DIMS, preferred_element_type=F32)
                 for j in range(n_pages)]
        parts.append(lax.dot_general(qbd, k_tail.astype(BF16), NT_DIMS, preferred_element_type=F32))
        s = jnp.concatenate(parts, axis=1) + bias_ref[h]
        p = jnp.exp2(s - jnp.max(s, axis=-1, keepdims=True))
        l = jnp.sum(p, axis=-1, keepdims=True)
        pb = p.astype(BF16)
        acc = jnp.dot(pb[:, n_pages * page:], v_tail.astype(BF16), preferred_element_type=F32)
        for j in range(n_pages):
            acc = acc + jnp.dot(pb[:, j * page:(j + 1) * page], v_refs[j][head_rows, :].astype(BF16),
                                preferred_element_type=F32)
        on = acc / l
        o = _diff_finish(on[0:lq], on[lq:2 * lq], lam, subln_ref[...], lambda_init)
        o_ref[:, cols] = o.astype(o_ref.dtype)


def diff_attention_decode(proj, cache_k, cache_v, page_table, bias, lamp, subln, *, layer, lambda_init):
    DB, lq, _ = proj.shape
    n_pages = page_table.shape[1]
    page = cache_k.shape[2] // DIFF_H

    def page_spec(j):
        return pl.BlockSpec((None, None, page * DIFF_H, DIFF_DV), lambda b, pt: (layer, pt[b, j], 0, 0))

    grid_spec = pltpu.PrefetchScalarGridSpec(
        num_scalar_prefetch=1,
        grid=(DB,),
        in_specs=([pl.BlockSpec((None, lq, 3 * DIFF_W), lambda b, pt: (b, 0, 0)),
                   pl.BlockSpec(bias.shape, lambda b, pt: (0, 0, 0)),
                   pl.BlockSpec((4, DIFF_DH), lambda b, pt: (0, 0)),
                   pl.BlockSpec((1, DIFF_DV), lambda b, pt: (0, 0))]
                  + [page_spec(j) for j in range(n_pages)]
                  + [page_spec(j) for j in range(n_pages)]),
        out_specs=pl.BlockSpec((None, lq, DIFF_W), lambda b, pt: (b, 0, 0)),
    )
    return pl.pallas_call(
        functools.partial(_diff_decode_kernel, n_pages=n_pages, page=page, lq=lq, lambda_init=lambda_init),
        grid_spec=grid_spec,
        out_shape=jax.ShapeDtypeStruct((DB, lq, DIFF_W), BF16),
        compiler_params=_params("parallel"),
        name="diff_attn_decode",
    )(page_table, proj, bias, lamp, subln.reshape(1, DIFF_DV),
      *([cache_k] * n_pages), *([cache_v] * n_pages))


def _gla_kernel(qk_ref, v_ref, gg_ref, alr_ref, wa_ref, ba_ref, gn_ref, ex_ref, s0_ref, o_ref, sout_ref, st_sc,
                *, C, SB, n_chunks, G):
    c = pl.program_id(1)

    @pl.when(c == 0)
    def _init():
        st_sc[...] = s0_ref[...]

    for g in range(G):
        _gla_chunk(qk_ref.at[g], v_ref.at[g], gg_ref.at[g], alr_ref.at[g], wa_ref, ba_ref, gn_ref, ex_ref,
                   o_ref.at[g], st_sc.at[g], C=C, SB=SB)

    @pl.when(c == n_chunks - 1)
    def _fin():
        sout_ref[...] = st_sc[...]


def _gla_chunk(qk_ref, v_ref, gg_ref, alr_ref, wa_ref, ba_ref, gn_ref, ex_ref, o_ref, st_sc, *, C, SB):
    z = jnp.dot(alr_ref[...], wa_ref[...], precision=lax.Precision.HIGHEST,
                preferred_element_type=F32) + ba_ref[...]
    logg = (jnp.minimum(z, 0.0) - jnp.log1p(jnp.exp(-jnp.abs(z)))) * (1.0 / GATE_NORM)
    row = lax.broadcasted_iota(I32, logg.shape, 0)
    b = logg
    d = 1
    while d < C:
        b = b + jnp.where(row >= d, pltpu.roll(b, d, 0), 0.0)
        d *= 2
    q_all = qk_ref[:, 0:GLA_H * GLA_DK] * (GLA_DK ** -0.5)
    k_all = qk_ref[:, GLA_H * GLA_DK:2 * GLA_H * GLA_DK]
    v_all = v_ref[...]
    n_sub = C // SB
    hk = lambda a, h: a[:, h * GLA_DK:(h + 1) * GLA_DK]
    hv = lambda a, h: a[:, h * GLA_DV:(h + 1) * GLA_DV]

    b_last = b[C - 1:C, :]
    q_in = (q_all * jnp.exp(b)).astype(BF16)
    k_dec = (k_all * jnp.exp(b_last - b)).astype(BF16)
    e_last = jnp.exp(b_last)
    q_off, k_off = [], []
    for i in range(1, n_sub):
        ref = b[i * SB:i * SB + 1, :]
        q_off.append((q_all[i * SB:(i + 1) * SB] * jnp.exp(b[i * SB:(i + 1) * SB] - ref)).astype(BF16))
        k_off.append((k_all * jnp.exp(jnp.minimum(ref - b, 0.0))).astype(BF16))
    col = lax.broadcasted_iota(I32, (SB, C), 1)

    outs = []
    for h in range(GLA_H):
        vb = hv(v_all, h).astype(BF16)
        st = st_sc[h]
        o = lax.dot_general(hk(q_in, h), st.astype(BF16), NT_DIMS, preferred_element_type=F32)
        if n_sub > 1:
            rows = [jnp.zeros((SB, C), F32)]
            for i in range(1, n_sub):
                a = lax.dot_general(hk(q_off[i - 1], h), hk(k_off[i - 1], h), NT_DIMS,
                                    preferred_element_type=F32)
                rows.append(jnp.where(col < i * SB, a, 0.0))
            a_off = jnp.concatenate(rows, axis=0)
            o = o + jnp.dot(a_off.astype(BF16), vb, preferred_element_type=F32)
        st_sc[h] = st * hk(e_last, h) + lax.dot_general(vb, hk(k_dec, h), TN_DIMS, preferred_element_type=F32)
        outs.append(o)

    rmod = row % SB
    xs = []
    for dlt in range(SB):
        kd, bd = (k_all, b) if dlt == 0 else (pltpu.roll(k_all, dlt, 0), pltpu.roll(b, dlt, 0))
        x = q_all * kd * jnp.exp(jnp.minimum(b - bd, 0.0))
        xs.append(jnp.where(rmod >= dlt, x, 0.0))
    w = jnp.dot(jnp.concatenate(xs, axis=0).astype(BF16), ex_ref[...], preferred_element_type=F32)
    o_diag = w[0:C] * v_all
    for dlt in range(1, SB):
        o_diag = o_diag + w[dlt * C:(dlt + 1) * C] * pltpu.roll(v_all, dlt, 0)

    for h in range(GLA_H):
        gate = _silu(hv(gg_ref[...], h))
        o = outs[h] + hv(o_diag, h)
        o_ref[:, h * GLA_DV:(h + 1) * GLA_DV] = (_rms(o, gn_ref[...], 1e-5) * gate).astype(o_ref.dtype)


def gla(proj, w_a2p, b_a, gla_g, s0t, *, C, SB, G):
    B, L, _ = proj.shape
    n_chunks = L // C
    W = GLA_W
    expand = (jnp.arange(GLA_H * GLA_DK, dtype=I32)[:, None] // GLA_DK
              == jnp.arange(W, dtype=I32)[None, :] // GLA_DV).astype(BF16)
    return pl.pallas_call(
        functools.partial(_gla_kernel, C=C, SB=SB, n_chunks=n_chunks, G=G),
        grid=(B // G, n_chunks),
        in_specs=[
            pl.BlockSpec((G, C, W), lambda b, c: (b, c, COL_QK_GLA)),
            pl.BlockSpec((G, C, W), lambda b, c: (b, c, COL_V_GLA)),
            pl.BlockSpec((G, C, W), lambda b, c: (b, c, COL_G_GLA)),
            pl.BlockSpec((G, C, LANES), lambda b, c: (b, c, COL_ALR)),
            pl.BlockSpec((LANES, GLA_H * GLA_DK), lambda b, c: (0, 0)),
            pl.BlockSpec((1, GLA_H * GLA_DK), lambda b, c: (0, 0)),
            pl.BlockSpec((1, GLA_DV), lambda b, c: (0, 0)),
            pl.BlockSpec((GLA_H * GLA_DK, W), lambda b, c: (0, 0)),
            pl.BlockSpec((G, GLA_H, GLA_DV, GLA_DK), lambda b, c: (b, 0, 0, 0)),
        ],
        out_specs=[pl.BlockSpec((G, C, W), lambda b, c: (b, c, 0)),
                   pl.BlockSpec((G, GLA_H, GLA_DV, GLA_DK), lambda b, c: (b, 0, 0, 0))],
        out_shape=[jax.ShapeDtypeStruct((B, L, W), BF16),
                   jax.ShapeDtypeStruct((B, GLA_H, GLA_DV, GLA_DK), F32)],
        scratch_shapes=[pltpu.VMEM((G, GLA_H, GLA_DV, GLA_DK), F32)],
        compiler_params=_params("parallel", "arbitrary"),
        name="gla",
    )(proj, proj, proj, proj, w_a2p, b_a.reshape(1, -1), gla_g.reshape(1, -1), expand, s0t)


def _xattn_kernel(q_ref, *refs):
    k_refs, v_refs, o_ref = refs[:MEM_H], refs[MEM_H:2 * MEM_H], refs[2 * MEM_H]
    for h in range(MEM_H):
        cols = slice(h * MEM_DH, (h + 1) * MEM_DH)
        s = lax.dot_general(q_ref[:, cols], k_refs[h][...].astype(BF16), NT_DIMS, preferred_element_type=F32)
        m = jnp.max(s, axis=-1, keepdims=True)
        p = jnp.exp(s - m)
        l = jnp.sum(p, axis=-1, keepdims=True)
        o = jnp.dot(p.astype(BF16), v_refs[h][...].astype(BF16), preferred_element_type=F32) / l
        o_ref[:, cols] = o.astype(o_ref.dtype)


def xattn_core(q, mk, mv, k_spec, v_spec, *, tm):
    Bx, Lx, W = q.shape
    return pl.pallas_call(
        _xattn_kernel,
        grid=(Bx, Lx // tm),
        in_specs=([pl.BlockSpec((None, tm, W), lambda b, i: (b, i, 0))]
                  + [k_spec(h) for h in range(MEM_H)] + [v_spec(h) for h in range(MEM_H)]),
        out_specs=pl.BlockSpec((None, tm, W), lambda b, i: (b, i, 0)),
        out_shape=jax.ShapeDtypeStruct((Bx, Lx, W), BF16),
        compiler_params=_params("parallel", "arbitrary"),
        name="xattn_core",
    )(q, *([mk] * MEM_H), *([mv] * MEM_H))


def _xattn_cached_kernel(q_ref, mk_hbm, mv_hbm, o_ref, k_buf, v_buf, sem, *, layer, n_steps, G):
    b = pl.program_id(0)
    slot = b % 2

    def copies(step, buf_slot):
        out = []
        for g in range(G):
            seq = step * G + g
            for h in range(MEM_H):
                out.append(pltpu.make_async_copy(mk_hbm.at[layer, seq, :, h, :], k_buf.at[buf_slot, g, h],
                                                 sem.at[0, buf_slot]))
                out.append(pltpu.make_async_copy(mv_hbm.at[layer, seq, :, h, :], v_buf.at[buf_slot, g, h],
                                                 sem.at[1, buf_slot]))
        return out

    @pl.when(b == 0)
    def _first():
        for c in copies(0, 0):
            c.start()

    @pl.when(b + 1 < n_steps)
    def _prefetch():
        for c in copies(b + 1, 1 - slot):
            c.start()

    for c in copies(b, slot):
        c.wait()
    for g in range(G):
        _xattn_kernel(q_ref.at[g], *[k_buf.at[slot, g, h] for h in range(MEM_H)],
                      *[v_buf.at[slot, g, h] for h in range(MEM_H)], o_ref.at[g])


def xattn_cached(q, mk, mv, *, layer, G):
    DB, LS, W = q.shape
    n_mem = mk.shape[2]
    buf = pltpu.VMEM((2, G, MEM_H, n_mem, MEM_DH), F32)
    return pl.pallas_call(
        functools.partial(_xattn_cached_kernel, layer=layer, n_steps=DB // G, G=G),
        grid=(DB // G,),
        in_specs=[pl.BlockSpec((G, LS, W), lambda b: (b, 0, 0)),
                  pl.BlockSpec(memory_space=pl.ANY), pl.BlockSpec(memory_space=pl.ANY)],
        out_specs=pl.BlockSpec((G, LS, W), lambda b: (b, 0, 0)),
        out_shape=jax.ShapeDtypeStruct((DB, LS, W), BF16),
        scratch_shapes=[buf, buf, pltpu.SemaphoreType.DMA((2, 2))],
        compiler_params=_params("arbitrary"),
        name="xattn_cached",
    )(q, mk, mv)


def _ffn_kernel(te_ref, tv_ref, x_ref, g_ref, wg_ref, wu_ref, wd_ref, sc_ref, o_ref, xn_sc, acc_sc,
                *, n_f, dense):
    i = pl.program_id(0)
    f = pl.program_id(1)

    @pl.when(f == 0)
    def _init():
        if dense:
            xn_sc[...] = _rms(x_ref[...], g_ref[...], 1e-6).astype(BF16)
        else:
            parts = _from_token_major(x_ref, xn_sc.shape[1])
            ms = sum(jnp.sum(p * p, axis=-1, keepdims=True) for p in parts) * (1.0 / xn_sc.shape[1])
            inv = lax.rsqrt(ms + 1e-6)
            for c, p in enumerate(parts):
                cols = slice(c * LANES, (c + 1) * LANES)
                xn_sc[:, cols] = ((p * inv) * g_ref[:, cols]).astype(BF16)
        acc_sc[...] = jnp.zeros(acc_sc.shape, F32)

    @pl.when(tv_ref[i] != 0)
    def _compute():
        xn = xn_sc[...]
        g = jnp.dot(xn, wg_ref[...], preferred_element_type=F32)
        u = jnp.dot(xn, wu_ref[...], preferred_element_type=F32)
        a = (_silu(g) * u).astype(BF16)
        acc_sc[...] += jnp.dot(a, wd_ref[...], preferred_element_type=F32)

    @pl.when(f == n_f - 1)
    def _fin():
        if dense:
            o_ref[...] = x_ref[...] + acc_sc[...]
        else:
            _to_token_major(o_ref, acc_sc[...] * sc_ref[...])


def ffn(x, g, w_gu, w_d, tile_expert, tile_valid, row_scale, *, tm, tf, dense):
    D = w_gu.shape[1]
    Mp = row_scale.shape[0]
    F = w_d.shape[1]
    n_f = F // tf
    last = n_f - 1

    def fblk(i, f, tv):
        return f * tv[i] + last * (1 - tv[i])

    if dense:
        x_spec = pl.BlockSpec((tm, D), lambda i, f, te, tv: (i, 0))
    else:
        x_spec = pl.BlockSpec((tm * D // LANES, LANES), lambda i, f, te, tv: (i, 0))
    grid_spec = pltpu.PrefetchScalarGridSpec(
        num_scalar_prefetch=2,
        grid=(Mp // tm, n_f),
        in_specs=[
            x_spec,
            pl.BlockSpec((1, D), lambda i, f, te, tv: (0, 0)),
            pl.BlockSpec((None, D, tf), lambda i, f, te, tv: (te[i], 0, fblk(i, f, tv))),
            pl.BlockSpec((None, D, tf), lambda i, f, te, tv: (te[i], 0, n_f + fblk(i, f, tv))),
            pl.BlockSpec((None, tf, D), lambda i, f, te, tv: (te[i], fblk(i, f, tv), 0)),
            pl.BlockSpec((tm, 1), lambda i, f, te, tv: (i, 0)),
        ],
        out_specs=x_spec,
        scratch_shapes=[pltpu.VMEM((tm, D), BF16), pltpu.VMEM((tm, D), F32)],
    )
    return pl.pallas_call(
        functools.partial(_ffn_kernel, n_f=n_f, dense=dense),
        grid_spec=grid_spec,
        out_shape=jax.ShapeDtypeStruct(x.shape, F32),
        compiler_params=_params("parallel", "arbitrary"),
        name="ffn_dense" if dense else "ffn_grouped",
    )(tile_expert, tile_valid, x, g.reshape(1, D), w_gu, w_gu, w_d, row_scale)


def _router_kernel(x_ref, g_ref, wr_ref, idx_ref, gate_ref):
    xn = _rms(x_ref[...], g_ref[...], 1e-6)
    logits = jnp.dot(xn, wr_ref[...], precision=lax.Precision.HIGHEST, preferred_element_type=F32)
    lane = lax.broadcasted_iota(I32, logits.shape, 1)
    real = lane < N_EXPERTS
    logits = jnp.where(real, logits, NEG_BIG)
    e = jnp.exp(logits - jnp.max(logits, axis=-1, keepdims=True))
    probs = jnp.where(real, e / jnp.sum(e, axis=-1, keepdims=True), -1.0)
    v1 = jnp.max(probs, axis=-1, keepdims=True)
    i1 = jnp.min(jnp.where(probs == v1, lane, LANES), axis=-1, keepdims=True)
    rest = jnp.where(lane == i1, -1.0, probs)
    v2 = jnp.max(rest, axis=-1, keepdims=True)
    i2 = jnp.min(jnp.where(rest == v2, lane, LANES), axis=-1, keepdims=True)
    den = v1 + v2
    idx_ref[...] = jnp.where(lane == 0, i1, jnp.where(lane == 1, i2, 0))
    gate_ref[...] = jnp.where(lane == 0, v1 / den, jnp.where(lane == 1, v2 / den, 0.0))


def router(x, g, w_router_pad, *, tm):
    M, D = x.shape
    return pl.pallas_call(
        _router_kernel,
        grid=(M // tm,),
        in_specs=[pl.BlockSpec((tm, D), lambda i: (i, 0)),
                  pl.BlockSpec((1, D), lambda i: (0, 0)),
                  pl.BlockSpec((D, LANES), lambda i: (0, 0))],
        out_specs=[pl.BlockSpec((tm, LANES), lambda i: (i, 0)),
                   pl.BlockSpec((tm, LANES), lambda i: (i, 0))],
        out_shape=[jax.ShapeDtypeStruct((M, LANES), I32), jax.ShapeDtypeStruct((M, LANES), F32)],
        compiler_params=_params("parallel"),
        name="router",
    )(x, g.reshape(1, D), w_router_pad)


def _row_copy(src_hbm, dst_ref, src_row, dst_row, sem):
    return pltpu.make_async_copy(src_hbm.at[pl.ds(pl.multiple_of(src_row * SUBLANES, SUBLANES), SUBLANES), :],
                                 dst_ref.at[pl.ds(pl.multiple_of(dst_row * SUBLANES, SUBLANES), SUBLANES), :], sem)


def _gather_kernel(idx_ref, src_hbm, o_ref, sem, *, R):
    def start(j, carry):
        r = 2 * j
        _row_copy(src_hbm, o_ref, idx_ref[0, r], r, sem).start(priority=0)
        _row_copy(src_hbm, o_ref, idx_ref[0, r + 1], r + 1, sem).start(priority=1)
        return carry

    lax.fori_loop(0, R // 2, start, 0, unroll=4)
    pltpu.make_async_copy(src_hbm.at[pl.ds(0, R * SUBLANES), :], o_ref, sem).wait()


def gather_rows(src, idx, *, R):
    Mp = idx.shape[0]
    return pl.pallas_call(
        functools.partial(_gather_kernel, R=R),
        grid=(Mp // R,),
        in_specs=[pl.BlockSpec((None, 1, R), lambda i: (i, 0, 0), memory_space=pltpu.SMEM),
                  pl.BlockSpec(memory_space=pl.ANY)],
        out_specs=pl.BlockSpec((R * SUBLANES, LANES), lambda i: (i, 0)),
        out_shape=jax.ShapeDtypeStruct((Mp * SUBLANES, LANES), src.dtype),
        scratch_shapes=[pltpu.SemaphoreType.DMA(())],
        compiler_params=_params("arbitrary"),
        name="gather_rows",
    )(idx.reshape(Mp // R, 1, R), src)


def _combine_kernel(pos_ref, h_ref, y_hbm, g_ref, o_ref, a_sc, b_sc, sem, *, R, final_norm):
    def start(r, carry):
        _row_copy(y_hbm, a_sc, pos_ref[0, 2 * r], r, sem.at[0]).start(priority=0)
        _row_copy(y_hbm, b_sc, pos_ref[0, 2 * r + 1], r, sem.at[1]).start(priority=1)
        return carry

    lax.fori_loop(0, R, start, 0, unroll=4)
    pltpu.make_async_copy(y_hbm.at[pl.ds(0, R * SUBLANES), :], a_sc, sem.at[0]).wait()
    pltpu.make_async_copy(y_hbm.at[pl.ds(0, R * SUBLANES), :], b_sc, sem.at[1]).wait()
    n = h_ref.shape[1]
    moe = jnp.concatenate([a + b for a, b in zip(_from_token_major(a_sc, n), _from_token_major(b_sc, n))], axis=1)
    out = h_ref[...] + moe
    if final_norm:
        out = _rms(out, g_ref[...], 1e-6)
    o_ref[...] = out


def combine(h, y, pos, g, *, R, final_norm):
    M, D = h.shape
    return pl.pallas_call(
        functools.partial(_combine_kernel, R=R, final_norm=final_norm),
        grid=(M // R,),
        in_specs=[pl.BlockSpec((None, 1, 2 * R), lambda i: (i, 0, 0), memory_space=pltpu.SMEM),
                  pl.BlockSpec((R, D), lambda i: (i, 0)),
                  pl.BlockSpec(memory_space=pl.ANY),
                  pl.BlockSpec((1, D), lambda i: (0, 0))],
        out_specs=pl.BlockSpec((R, D), lambda i: (i, 0)),
        out_shape=jax.ShapeDtypeStruct((M, D), F32),
        scratch_shapes=[pltpu.VMEM((R * SUBLANES, LANES), F32), pltpu.VMEM((R * SUBLANES, LANES), F32),
                        pltpu.SemaphoreType.DMA((2,))],
        compiler_params=_params("arbitrary"),
        name="combine",
    )(pos.reshape(M // R, 1, 2 * R), h, y, g.reshape(1, D))


def _rms_only_kernel(x_ref, g_ref, o_ref):
    o_ref[...] = _rms(x_ref[...], g_ref[...], 1e-6)


def rms_only(x, g, *, tm):
    M, D = x.shape
    return pl.pallas_call(
        _rms_only_kernel,
        grid=(M // tm,),
        in_specs=[pl.BlockSpec((tm, D), lambda i: (i, 0)), pl.BlockSpec((1, D), lambda i: (0, 0))],
        out_specs=pl.BlockSpec((tm, D), lambda i: (i, 0)),
        out_shape=jax.ShapeDtypeStruct((M, D), F32),
        compiler_params=_params("parallel"),
        name="rms_only",
    )(x, g.reshape(1, D))


def moe_ffn(h, h_tok, g_norm, w_router_pad, w_gu, w_d, g_final, *, tm, r_gather, r_combine, final_norm):
    M, D = h.shape
    idx_p, gate_p = router(h, g_norm, w_router_pad, tm=min(512, M))
    expert = idx_p[:, :TOP_K].reshape(-1)
    gate = gate_p[:, :TOP_K].reshape(-1)
    n_pairs = TOP_K * M
    n_tiles = n_pairs // tm + N_EXPERTS
    Mp = n_tiles * tm
    onehot = (expert[:, None] == jnp.arange(N_EXPERTS, dtype=I32)[None, :]).astype(I32)
    csum = jnp.cumsum(onehot, axis=0)
    counts = csum[-1]
    tiles_per = (counts + tm - 1) // tm
    tile_end = jnp.cumsum(tiles_per)
    tile_start = tile_end - tiles_per
    group_start = jnp.cumsum(counts) - counts
    rank = jnp.sum(onehot * (csum - 1), axis=1)
    slot_of_pair = jnp.sum(onehot * tile_start[None, :], axis=1) * tm + rank
    _, sorted_pair, sorted_gate = lax.sort((expert, jnp.arange(n_pairs, dtype=I32), gate), num_keys=1,
                                           is_stable=True)
    tile_ids = jnp.arange(n_tiles, dtype=I32)
    tile_valid = (tile_ids < tile_end[-1]).astype(I32)
    tile_expert = jnp.minimum(jnp.sum((tile_ids[:, None] >= tile_end[None, :]).astype(I32), axis=1),
                              N_EXPERTS - 1)
    within = (tile_ids - tile_start[tile_expert]) * tm
    pos = within[:, None] + jnp.arange(tm, dtype=I32)[None, :]
    live = (tile_valid[:, None] != 0) & (pos < counts[tile_expert][:, None])
    src = jnp.clip(group_start[tile_expert][:, None] + pos, 0, n_pairs - 1).reshape(Mp)
    live = live.reshape(Mp)
    token_of_slot = jnp.where(live, sorted_pair[src] // TOP_K, jnp.arange(Mp, dtype=I32) % M)
    scale_of_slot = jnp.where(live, sorted_gate[src], 0.0)
    last_expert = tile_expert[jnp.maximum(tile_end[-1] - 1, 0)]
    tile_expert = jnp.where(tile_valid != 0, tile_expert, last_expert)

    x_sorted = gather_rows(h_tok, token_of_slot, R=r_gather)
    y_sorted = ffn(x_sorted, g_norm, w_gu, w_d, tile_expert, tile_valid, scale_of_slot.reshape(Mp, 1),
                   tm=tm, tf=w_d.shape[1] // 2, dense=False)
    return combine(h, y_sorted, slot_of_pair, g_final, R=r_combine, final_norm=final_norm)


def kernel(x_prompt, x_sample, mem_prompt, cache_attn_k, cache_attn_v, cache_mem_k, cache_mem_v, state_gla,
           page_table, rel_bias, norm_mix, w_in, w_gla_a2, b_gla_a, gla_norm, diff_subln, lambda_q1, lambda_k1,
           lambda_q2, lambda_k2, w_out, norm_mem, norm_memkv, w_mq, w_mkv, w_mo, norm_ffn, w_ffn_gu,
           w_ffn_down, w_router, w_exp_gu, w_exp_down, norm_final):
    B, L, D = x_prompt.shape
    DB, LS, _ = x_sample.shape
    depth = w_in.shape[0]
    n_mem = mem_prompt.shape[1]
    n_pages, page = page_table.shape[1], cache_attn_k.shape[2]
    past_len = n_pages * page
    M, MS = B * L, DB * LS
    TQ_ATT, TK_ATT = 1024, 512
    GLA_C = 64

    bias_p = bias_tiles(rel_bias, R=TQ_ATT, C=TK_ATT,
                        offsets=[TK_ATT - kind * TK_ATT for kind in range(TQ_ATT // TK_ATT + 1)])
    bias_s = bias_tiles(rel_bias, R=2 * LS, C=past_len + page, offsets=(past_len,),
                        rows_per_head=LS).reshape(DIFF_H, 2 * LS, past_len + page)
    cache_k = cache_attn_k.reshape(depth, -1, page * DIFF_H, DIFF_DV)
    cache_v = cache_attn_v.reshape(depth, -1, page * DIFF_H, DIFF_DV)
    kv_heads = [(DIFF_W, DIFF_H, DIFF_DV), (2 * DIFF_W, DIFF_H, DIFF_DV)]
    mem_heads = [(0, MEM_H, MEM_DH), (MEM_H * MEM_DH, MEM_H, MEM_DH)]
    w_router_pad = jnp.pad(w_router, ((0, 0), (0, 0), (0, LANES - N_EXPERTS)))

    hp = x_prompt.reshape(M, D)
    hs = x_sample.reshape(MS, D)
    ps, pmk, pmv, ss = [], [], [], []
    kv_p = kv_s = None
    for l in range(depth):
        lambda_init = 0.8 - 0.6 * math.exp(-0.3 * l)
        w_in_p = jnp.pad(w_in[l], ((0, 0), (0, N_IN_PAD - N_IN))).astype(BF16)
        w_a2p = jnp.pad(w_gla_a2[l], ((0, LANES - GLA_LR), (0, 0)))
        lamp = jnp.stack([lambda_q1[l], lambda_k1[l], lambda_q2[l], lambda_k2[l]])
        w_out_a = w_out[l, :DIFF_W].astype(BF16)
        w_out_g = w_out[l, DIFF_W:].astype(BF16)
        w_mq_b, w_mo_b, w_mkv_b = w_mq[l].astype(BF16), w_mo[l].astype(BF16), w_mkv[l].astype(BF16)
        last = l == depth - 1

        proj_p, *kv_p = rms_matmul_heads(hp, norm_mix[l], w_in_p, tm=512, keep_full=True, head_outs=kv_heads,
                                         layer=l, depth=depth, stacks=kv_p)
        proj_s, *kv_s = rms_matmul_heads(hs, norm_mix[l], w_in_p, tm=512, keep_full=True, head_outs=kv_heads,
                                         layer=l, depth=depth, stacks=kv_s)
        proj_p = proj_p.reshape(B, L, N_IN_PAD)
        proj_s = proj_s.reshape(DB, LS, N_IN_PAD)
        att_p = diff_attention_prompt(proj_p, bias_p, lamp, diff_subln[l], TQ=TQ_ATT, TK=TK_ATT,
                                      lambda_init=lambda_init)
        att_s = diff_attention_decode(proj_s, cache_k, cache_v, page_table, bias_s, lamp, diff_subln[l],
                                      layer=l, lambda_init=lambda_init)
        zero_state = jnp.zeros((B, GLA_H, GLA_DV, GLA_DK), F32)
        gla_p, st_p = gla(proj_p, w_a2p, b_gla_a[l], gla_norm[l], zero_state, C=GLA_C, SB=16, G=B)
        gla_s, st_s = gla(proj_s, w_a2p, b_gla_a[l], gla_norm[l], jnp.swapaxes(state_gla[l], -1, -2),
                          C=LS, SB=LS, G=8)
        hp = mm_res([att_p.reshape(M, DIFF_W), gla_p.reshape(M, GLA_W)], [w_out_a, w_out_g], hp, tm=512)
        hs = mm_res([att_s.reshape(MS, DIFF_W), gla_s.reshape(MS, GLA_W)], [w_out_a, w_out_g], hs, tm=512)

        mkv_p, mk_p, mv_p = rms_matmul_heads(mem_prompt.reshape(B * n_mem, D), norm_memkv[l], w_mkv_b, tm=512,
                                             keep_full=True, head_outs=mem_heads)
        mkv_p = mkv_p.reshape(B, n_mem, 2 * MEM_H * MEM_DH)
        mk_p = mk_p.reshape(B, n_mem, MEM_H, MEM_DH)
        mv_p = mv_p.reshape(B, n_mem, MEM_H, MEM_DH)
        q_p = rms_matmul(hp, norm_mem[l], w_mq_b, tm=512, out_dtype=BF16, scale=MEM_DH ** -0.5)
        q_s = rms_matmul(hs, norm_mem[l], w_mq_b, tm=512, out_dtype=BF16, scale=MEM_DH ** -0.5)
        xo_p = xattn_core(
            q_p.reshape(B, L, D), mkv_p, mkv_p,
            lambda h: pl.BlockSpec((None, n_mem, MEM_DH), lambda b, i: (b, 0, h)),
            lambda h: pl.BlockSpec((None, n_mem, MEM_DH), lambda b, i: (b, 0, MEM_H + h)), tm=512)
        xo_s = xattn_cached(q_s.reshape(DB, LS, D), cache_mem_k, cache_mem_v, layer=l, G=2)
        moe_layer = l % 2 == 1
        hp = mm_res([xo_p.reshape(M, D)], [w_mo_b], hp, tm=512, token_major=moe_layer)
        hs = mm_res([xo_s.reshape(MS, D)], [w_mo_b], hs, tm=512, token_major=moe_layer)
        if moe_layer:
            (hp, hp_tok), (hs, hs_tok) = hp, hs

        if l % 2 == 0:
            w_gu = w_ffn_gu[l // 2].astype(BF16)[None]
            w_d = w_ffn_down[l // 2].astype(BF16)[None]
            for_dense = lambda h, tm: ffn(
                h, norm_ffn[l], w_gu, w_d, jnp.zeros((h.shape[0] // tm,), I32),
                jnp.ones((h.shape[0] // tm,), I32), jnp.ones((h.shape[0], 1), F32),
                tm=tm, tf=w_d.shape[1] // 2, dense=True)
            hp, hs = for_dense(hp, 512), for_dense(hs, 512)
            if last:
                hp, hs = rms_only(hp, norm_final, tm=512), rms_only(hs, norm_final, tm=512)
        else:
            w_gu = w_exp_gu[l // 2].astype(BF16)
            w_d = w_exp_down[l // 2].astype(BF16)
            hp = moe_ffn(hp, hp_tok, norm_ffn[l], w_router_pad[l // 2], w_gu, w_d, norm_final,
                         tm=512, r_gather=1024, r_combine=512, final_norm=last)
            hs = moe_ffn(hs, hs_tok, norm_ffn[l], w_router_pad[l // 2], w_gu, w_d, norm_final,
                         tm=256, r_gather=1024, r_combine=512, final_norm=last)

        ps.append(jnp.swapaxes(st_p, -1, -2))
        pmk.append(mk_p)
        pmv.append(mv_p)
        ss.append(jnp.swapaxes(st_s, -1, -2))

    pk, pv = (a.reshape(depth, B, L, DIFF_H, DIFF_DV) for a in kv_p)
    sk, sv = (a.reshape(depth, DB, LS, DIFF_H, DIFF_DV) for a in kv_s)
    return (hp.reshape(B, L, D), hs.reshape(DB, LS, D), pk, pv, jnp.stack(ps),
            jnp.stack(pmk), jnp.stack(pmv), sk, sv, jnp.stack(ss))
```

```python
import functools
import math

import numpy as np
import jax
import jax.numpy as jnp
from jax import lax
from jax.experimental import pallas as pl
from jax.experimental.pallas import tpu as pltpu

F32 = jnp.float32
BF16 = jnp.bfloat16
I32 = jnp.int32

LANES = 128
SUBLANES = 8
VMEM_LIMIT_BYTES = 56 * 1024 * 1024

D_MODEL = 1024
DIFF_H = 4
DIFF_DV = 128
DIFF_DH = 64
DIFF_W = DIFF_H * DIFF_DV
GLA_H = 4
GLA_DK = 64
GLA_DV = 128
GLA_W = GLA_H * GLA_DV
GLA_LR = 16
GATE_NORM = 16.0
N_BUCKETS = 32
MAX_DISTANCE = 128
MEM_H = 4
MEM_DH = 256
N_EXPERTS = 8
TOP_K = 2
N_IN = 3 * DIFF_W + 2 * GLA_H * GLA_DK + 2 * GLA_W + GLA_LR
N_IN_PAD = 3200
NEG_BIG = -1e30
LOG2E = math.log2(math.e)

COL_Q = 0
COL_K = 4
COL_V = 8
COL_QK_GLA = 3
COL_V_GLA = 4
COL_G_GLA = 5
COL_ALR = 24

NT_DIMS = (((1,), (1,)), ((), ()))
TN_DIMS = (((0,), (0,)), ((), ()))


def _params(*sem):
    return pltpu.CompilerParams(dimension_semantics=sem, vmem_limit_bytes=VMEM_LIMIT_BYTES)


def _rms(x, g, eps):
    ms = jnp.mean(x * x, axis=-1, keepdims=True)
    return (x * lax.rsqrt(ms + eps)) * g


def _silu(x):
    return x / (1.0 + jnp.exp(-x))


def _rms_matmul_kernel(x_ref, g_ref, w_ref, o_ref, *, eps, scale):
    xn = _rms(x_ref[...], g_ref[...], eps).astype(BF16)
    y = jnp.dot(xn, w_ref[...], preferred_element_type=F32)
    if scale != 1.0:
        y = y * scale
    o_ref[...] = y.astype(o_ref.dtype)


def rms_matmul(x, g, w, *, tm, out_dtype=F32, scale=1.0, eps=1e-6):
    M, K = x.shape
    N = w.shape[1]
    return pl.pallas_call(
        functools.partial(_rms_matmul_kernel, eps=eps, scale=scale),
        grid=(M // tm,),
        in_specs=[pl.BlockSpec((tm, K), lambda i: (i, 0)),
                  pl.BlockSpec((1, K), lambda i: (0, 0)),
                  pl.BlockSpec((K, N), lambda i: (0, 0))],
        out_specs=pl.BlockSpec((tm, N), lambda i: (i, 0)),
        out_shape=jax.ShapeDtypeStruct((M, N), out_dtype),
        compiler_params=_params("parallel"),
        name="rms_matmul",
    )(x, g.reshape(1, K), w)


def _rms_matmul_heads_kernel(x_ref, g_ref, w_ref, *refs, eps, keep_full, head_outs, n_alias, layer, depth):
    o_refs = refs[n_alias:]
    xn = _rms(x_ref[...], g_ref[...], eps).astype(BF16)
    y = jnp.dot(xn, w_ref[...], preferred_element_type=F32)
    if keep_full:
        o_refs[0][...] = y
    for o_ref, (col0, n_heads, width) in zip(o_refs[1 if keep_full else 0:], head_outs):
        if n_alias == 0:
            for other in range(depth):
                if other != layer:
                    o_ref[other] = jnp.zeros(o_ref.shape[1:], F32)
            o_ref = o_ref.at[layer]
        for h in range(n_heads):
            o_ref[:, h, :] = y[:, col0 + h * width:col0 + (h + 1) * width]


def rms_matmul_heads(x, g, w, *, tm, keep_full, head_outs, layer=0, depth=1, stacks=None, eps=1e-6):
    M, K = x.shape
    N = w.shape[1]
    stacks = [] if stacks is None else list(stacks)
    if stacks:
        out_specs = [pl.BlockSpec((None, tm, nh, wd), lambda i: (layer, i, 0, 0)) for _, nh, wd in head_outs]
    else:
        out_specs = [pl.BlockSpec((depth, tm, nh, wd), lambda i: (0, i, 0, 0)) for _, nh, wd in head_outs]
    out_shape = [jax.ShapeDtypeStruct((depth, M, nh, wd), F32) for _, nh, wd in head_outs]
    if keep_full:
        out_specs.insert(0, pl.BlockSpec((tm, N), lambda i: (i, 0)))
        out_shape.insert(0, jax.ShapeDtypeStruct((M, N), F32))
    first_stack_out = 1 if keep_full else 0
    return pl.pallas_call(
        functools.partial(_rms_matmul_heads_kernel, eps=eps, keep_full=keep_full, head_outs=tuple(head_outs),
                          n_alias=len(stacks), layer=layer, depth=depth),
        grid=(M // tm,),
        in_specs=[pl.BlockSpec((tm, K), lambda i: (i, 0)),
                  pl.BlockSpec((1, K), lambda i: (0, 0)),
                  pl.BlockSpec((K, N), lambda i: (0, 0))] + [pl.BlockSpec(memory_space=pl.ANY)] * len(stacks),
        out_specs=out_specs,
        out_shape=out_shape,
        input_output_aliases={3 + j: first_stack_out + j for j in range(len(stacks))},
        compiler_params=_params("parallel"),
        name="rms_matmul_heads",
    )(x, g.reshape(1, K), w, *stacks)


def _to_token_major(o_ref, y):
    rows, n = y.shape
    for c in range(n // LANES):
        o_ref[pl.ds(c, rows, stride=n // LANES), :] = y[:, c * LANES:(c + 1) * LANES]


def _from_token_major(x_ref, n):
    rows = x_ref.shape[0] // (n // LANES)
    return [x_ref[pl.ds(c, rows, stride=n // LANES), :] for c in range(n // LANES)]


def _mm_res_kernel(*refs, n_lhs, token_major):
    a_refs, w_refs = refs[:n_lhs], refs[n_lhs:2 * n_lhs]
    res_ref, o_ref = refs[2 * n_lhs], refs[2 * n_lhs + 1]
    acc = res_ref[...]
    for a_ref, w_ref in zip(a_refs, w_refs):
        acc = acc + jnp.dot(a_ref[...], w_ref[...], preferred_element_type=F32)
    o_ref[...] = acc
    if token_major:
        _to_token_major(refs[2 * n_lhs + 2], acc)


def mm_res(lhs, ws, res, *, tm, token_major=False):
    M, N = res.shape
    n = len(lhs)
    in_specs = ([pl.BlockSpec((tm, a.shape[1]), lambda i: (i, 0)) for a in lhs]
                + [pl.BlockSpec(w.shape, lambda i: (0, 0)) for w in ws]
                + [pl.BlockSpec((tm, N), lambda i: (i, 0))])
    out_specs = [pl.BlockSpec((tm, N), lambda i: (i, 0))]
    out_shape = [jax.ShapeDtypeStruct((M, N), F32)]
    if token_major:
        out_specs.append(pl.BlockSpec((tm * N // LANES, LANES), lambda i: (i, 0)))
        out_shape.append(jax.ShapeDtypeStruct((M * N // LANES, LANES), F32))
    out = pl.pallas_call(
        functools.partial(_mm_res_kernel, n_lhs=n, token_major=token_major),
        grid=(M // tm,),
        in_specs=in_specs,
        out_specs=out_specs,
        out_shape=out_shape,
        compiler_params=_params("parallel"),
        name="mm_res",
    )(*lhs, *ws, res)
    return out if token_major else out[0]


def _bias_kernel(tab_ref, o_ref, *, offsets, rows_per_head):
    h = pl.program_id(0)
    R, C = o_ref.shape[-2], o_ref.shape[-1]
    max_exact = N_BUCKETS // 2
    far = tab_ref[N_BUCKETS - 1, h]
    rb = min(R, rows_per_head, LANES)
    for kind, off in enumerate(offsets):
        for r0 in range(0, R, rb):
            for c0 in range(0, C, LANES):
                lo = off + r0 % rows_per_head - (c0 + LANES - 1)
                hi = off + r0 % rows_per_head + rb - 1 - c0
                blk = (kind, slice(r0, r0 + rb), slice(c0, c0 + LANES))
                if lo >= MAX_DISTANCE:
                    o_ref[blk] = jnp.zeros((rb, LANES), F32)
                    continue
                if hi < 0:
                    o_ref[blk] = jnp.full((rb, LANES), NEG_BIG, F32)
                    continue
                r = lax.broadcasted_iota(I32, (rb, LANES), 0)
                c = lax.broadcasted_iota(I32, (rb, LANES), 1)
                rel = (off + r0 % rows_per_head - c0) + r - c
                n = jnp.maximum(rel, 0)
                nf = jnp.maximum(n, max_exact).astype(F32)
                large = max_exact + (jnp.log(nf / max_exact) / math.log(MAX_DISTANCE / max_exact)
                                     * (N_BUCKETS - max_exact)).astype(I32)
                bucket = jnp.where(n < max_exact, n, jnp.minimum(large, N_BUCKETS - 1))
                acc = jnp.zeros((rb, LANES), F32)
                for b in range(N_BUCKETS - 1):
                    acc = jnp.where(bucket == b, (tab_ref[b, h] - far) * LOG2E, acc)
                o_ref[blk] = jnp.where(rel >= 0, acc, NEG_BIG)


def bias_tiles(rel_bias, *, R, C, offsets, rows_per_head=None):
    rows_per_head = R if rows_per_head is None else rows_per_head
    return pl.pallas_call(
        functools.partial(_bias_kernel, offsets=tuple(offsets), rows_per_head=rows_per_head),
        grid=(DIFF_H,),
        in_specs=[pl.BlockSpec(memory_space=pltpu.SMEM)],
        out_specs=pl.BlockSpec((None, len(offsets), R, C), lambda h: (h, 0, 0, 0)),
        out_shape=jax.ShapeDtypeStruct((DIFF_H, len(offsets), R, C), F32),
        compiler_params=_params("arbitrary"),
        name="bias_tiles",
    )(rel_bias)


def _lambda_value(lamp, lambda_init):
    s1 = jnp.sum(lamp[0:1, :] * lamp[1:2, :], axis=-1, keepdims=True)
    s2 = jnp.sum(lamp[2:3, :] * lamp[3:4, :], axis=-1, keepdims=True)
    return jnp.exp(s1) - jnp.exp(s2) + lambda_init


def _diff_finish(o1, o2, lam, subln, lambda_init):
    o = o1 - lam * o2
    return _rms(o, subln, 1e-5) * (1.0 - lambda_init)


def _diff_attn_kernel(qi_tab, ki_tab, q_ref, k_ref, v_ref, bias_ref, lamp_ref, subln_ref, o_ref,
                      q2_sc, m_sc, acc_sc, *, TQ, TK, RB, lambda_init):
    t = pl.program_id(2)
    qi = qi_tab[t]
    ki = ki_tab[t]
    n = TQ // TK
    kind_now = ki - qi * n + 1

    @pl.when(ki == 0)
    def _init():
        q = q_ref[...] * (DIFF_DH ** -0.5 * LOG2E)
        lane = lax.broadcasted_iota(I32, q.shape, 1)
        q2_sc[0:TQ, :] = jnp.where(lane < DIFF_DH, q, 0.0).astype(BF16)
        q2_sc[TQ:2 * TQ, :] = jnp.where(lane >= DIFF_DH, q, 0.0).astype(BF16)
        m_sc[...] = jnp.full(m_sc.shape, NEG_BIG, F32)
        acc_sc[...] = jnp.zeros(acc_sc.shape, F32)

    def update(kind):
        kb = k_ref[...].astype(BF16)
        vb = jnp.concatenate([v_ref[...].astype(BF16), jnp.ones((TK, DIFF_DV), BF16)], axis=1)
        for r0 in range(0, 2 * TQ, RB):
            rows = slice(r0, r0 + RB)
            q_row = r0 % TQ
            n_k = TK if kind is None else max(0, min(TK, TK - kind * TK + q_row + RB))
            if n_k == 0:
                continue
            s = lax.dot_general(q2_sc[rows, :], kb[0:n_k], NT_DIMS, preferred_element_type=F32)
            if kind is not None:
                s = s + bias_ref[kind, q_row:q_row + RB, 0:n_k]
            cols = [s[:, c * LANES:(c + 1) * LANES] for c in range(n_k // LANES)]
            m_old = m_sc[rows, :]
            m_new = jnp.maximum(m_old, jnp.max(functools.reduce(jnp.maximum, cols), axis=-1, keepdims=True))
            alpha = jnp.exp2(m_old - m_new)
            p = jnp.concatenate([jnp.exp2(c - m_new) for c in cols], axis=1).astype(BF16)
            acc_sc[rows, :] = (jnp.concatenate([alpha, alpha], axis=1) * acc_sc[rows, :]
                               + jnp.dot(p, vb[0:n_k], preferred_element_type=F32))
            m_sc[rows, :] = m_new

    @pl.when(kind_now < 0)
    def _far():
        update(None)

    for kind in range(n + 1):
        @pl.when(kind_now == kind)
        def _near(kind=kind):
            update(kind)
            if kind == n:
                on = acc_sc[:, 0:DIFF_DV] / acc_sc[:, DIFF_DV:2 * DIFF_DV]
                lam = _lambda_value(lamp_ref[...], lambda_init)
                o_ref[...] = _diff_finish(on[0:TQ], on[TQ:2 * TQ], lam, subln_ref[...],
                                          lambda_init).astype(o_ref.dtype)


def diff_attention_prompt(proj, bias, lamp, subln, *, TQ, TK, lambda_init):
    B, L, _ = proj.shape
    n = TQ // TK
    pairs = [(qi, ki) for qi in range(L // TQ) for ki in range((qi + 1) * n)]
    qi_tab = jnp.asarray(np.array([p[0] for p in pairs], np.int32))
    ki_tab = jnp.asarray(np.array([p[1] for p in pairs], np.int32))
    grid_spec = pltpu.PrefetchScalarGridSpec(
        num_scalar_prefetch=2,
        grid=(B, DIFF_H, len(pairs)),
        in_specs=[
            pl.BlockSpec((None, TQ, DIFF_DV), lambda b, h, t, qt, kt: (b, qt[t], COL_Q + h)),
            pl.BlockSpec((None, TK, DIFF_DV), lambda b, h, t, qt, kt: (b, kt[t], COL_K + h)),
            pl.BlockSpec((None, TK, DIFF_DV), lambda b, h, t, qt, kt: (b, kt[t], COL_V + h)),
            pl.BlockSpec((None, n + 1, TQ, TK), lambda b, h, t, qt, kt: (h, 0, 0, 0)),
            pl.BlockSpec((4, DIFF_DH), lambda b, h, t, qt, kt: (0, 0)),
            pl.BlockSpec((1, DIFF_DV), lambda b, h, t, qt, kt: (0, 0)),
        ],
        out_specs=pl.BlockSpec((None, TQ, DIFF_DV), lambda b, h, t, qt, kt: (b, qt[t], h)),
        scratch_shapes=[pltpu.VMEM((2 * TQ, DIFF_DV), BF16),
                        pltpu.VMEM((2 * TQ, LANES), F32),
                        pltpu.VMEM((2 * TQ, 2 * DIFF_DV), F32)],
    )
    return pl.pallas_call(
        functools.partial(_diff_attn_kernel, TQ=TQ, TK=TK, RB=LANES, lambda_init=lambda_init),
        grid_spec=grid_spec,
        out_shape=jax.ShapeDtypeStruct((B, L, DIFF_W), BF16),
        compiler_params=_params("parallel", "parallel", "arbitrary"),
        name="diff_attn_prompt",
    )(qi_tab, ki_tab, proj, proj, proj, bias, lamp, subln.reshape(1, DIFF_DV))


def _diff_decode_kernel(pt_ref, proj_ref, bias_ref, lamp_ref, subln_ref, ck_hbm, cv_hbm, o_ref, k_buf, v_buf, sem,
                        *, layer, n_seq, n_pages, page, lq, lambda_init):
    b = pl.program_id(0)
    slot = b % 2

    def start_pages(seq, buf_slot):
        for j in range(n_pages):
            pg = pt_ref[seq, j]
            pltpu.make_async_copy(ck_hbm.at[layer, pg], k_buf.at[buf_slot, j], sem.at[0, buf_slot]).start()
            pltpu.make_async_copy(cv_hbm.at[layer, pg], v_buf.at[buf_slot, j], sem.at[1, buf_slot]).start()

    @pl.when(b == 0)
    def _first():
        start_pages(0, 0)

    @pl.when(b + 1 < n_seq)
    def _prefetch():
        start_pages(b + 1, 1 - slot)

    pltpu.make_async_copy(ck_hbm.at[layer, pl.ds(0, n_pages)], k_buf.at[slot], sem.at[0, slot]).wait()
    pltpu.make_async_copy(cv_hbm.at[layer, pl.ds(0, n_pages)], v_buf.at[slot], sem.at[1, slot]).wait()
    k_refs = [k_buf.at[slot, j] for j in range(n_pages)]
    v_refs = [v_buf.at[slot, j] for j in range(n_pages)]
    lam = _lambda_value(lamp_ref[...], lambda_init)
    lane = lax.broadcasted_iota(I32, (lq, DIFF_DV), 1)
    pad = jnp.zeros((page - lq, DIFF_DV), F32)
    for h in range(DIFF_H):
        cols = slice(h * DIFF_DV, (h + 1) * DIFF_DV)
        q = proj_ref[:, cols] * (DIFF_DH ** -0.5 * LOG2E)
        qbd = jnp.concatenate([jnp.where(lane < DIFF_DH, q, 0.0), jnp.where(lane >= DIFF_DH, q, 0.0)],
                              axis=0).astype(BF16)
        k_tail = jnp.concatenate([proj_ref[:, DIFF_W + h * DIFF_DV:DIFF_W + (h + 1) * DIFF_DV], pad], 0)
        v_tail = jnp.concatenate([proj_ref[:, 2 * DIFF_W + h * DIFF_DV:2 * DIFF_W + (h + 1) * DIFF_DV], pad], 0)
        head_rows = pl.ds(h, page, stride=DIFF_H)
        parts = [lax.dot_general(qbd, k_refs[j][head_rows, :].astype(BF16), NT_DIMS, preferred_element_type=F32)
                 for j in range(n_pages)]
        parts.append(lax.dot_general(qbd, k_tail.astype(BF16), NT_DIMS, preferred_element_type=F32))
        s = jnp.concatenate(parts, axis=1) + bias_ref[h]
        p = jnp.exp2(s - jnp.max(s, axis=-1, keepdims=True))
        l = jnp.sum(p, axis=-1, keepdims=True)
        pb = p.astype(BF16)
        acc = jnp.dot(pb[:, n_pages * page:], v_tail.astype(BF16), preferred_element_type=F32)
        for j in range(n_pages):
            acc = acc + jnp.dot(pb[:, j * page:(j + 1) * page], v_refs[j][head_rows, :].astype(BF16),
                                preferred_element_type=F32)
        on = acc / l
        o = _diff_finish(on[0:lq], on[lq:2 * lq], lam, subln_ref[...], lambda_init)
        o_ref[:, cols] = o.astype(o_ref.dtype)


def diff_attention_decode(proj, cache_k, cache_v, page_table, bias, lamp, subln, *, layer, lambda_init):
    DB, lq, _ = proj.shape
    n_pages = page_table.shape[1]
    page = cache_k.shape[2] // DIFF_H

    page_buf = pltpu.VMEM((2, n_pages, page * DIFF_H, DIFF_DV), F32)
    grid_spec = pltpu.PrefetchScalarGridSpec(
        num_scalar_prefetch=1,
        grid=(DB,),
        in_specs=[pl.BlockSpec((None, lq, 3 * DIFF_W), lambda b, pt: (b, 0, 0)),
                  pl.BlockSpec(bias.shape, lambda b, pt: (0, 0, 0)),
                  pl.BlockSpec((4, DIFF_DH), lambda b, pt: (0, 0)),
                  pl.BlockSpec((1, DIFF_DV), lambda b, pt: (0, 0)),
                  pl.BlockSpec(memory_space=pl.ANY), pl.BlockSpec(memory_space=pl.ANY)],
        out_specs=pl.BlockSpec((None, lq, DIFF_W), lambda b, pt: (b, 0, 0)),
        scratch_shapes=[page_buf, page_buf, pltpu.SemaphoreType.DMA((2, 2))],
    )
    return pl.pallas_call(
        functools.partial(_diff_decode_kernel, layer=layer, n_seq=DB, n_pages=n_pages, page=page, lq=lq,
                          lambda_init=lambda_init),
        grid_spec=grid_spec,
        out_shape=jax.ShapeDtypeStruct((DB, lq, DIFF_W), BF16),
        compiler_params=_params("arbitrary"),
        name="diff_attn_decode",
    )(page_table, proj, bias, lamp, subln.reshape(1, DIFF_DV), cache_k, cache_v)


def _gla_kernel(qk_ref, v_ref, gg_ref, alr_ref, wa_ref, ba_ref, gn_ref, ex_ref, s0_ref, *refs,
                C, SB, n_chunks, G, layer, depth, n_alias):
    o_ref, sout_ref, st_sc = refs[n_alias:]
    c = pl.program_id(1)

    @pl.when(c == 0)
    def _init():
        for g in range(G):
            for h in range(GLA_H):
                st_sc[g, h] = s0_ref[g, h].T

    for g in range(G):
        _gla_chunk(qk_ref.at[g], v_ref.at[g], gg_ref.at[g], alr_ref.at[g], wa_ref, ba_ref, gn_ref, ex_ref,
                   o_ref.at[g], st_sc.at[g], C=C, SB=SB)

    @pl.when(c == n_chunks - 1)
    def _fin():
        out = sout_ref
        if n_alias == 0:
            for other in range(depth):
                if other != layer:
                    sout_ref[other] = jnp.zeros(sout_ref.shape[1:], F32)
            out = sout_ref.at[layer]
        for g in range(G):
            for h in range(GLA_H):
                out[g, h] = st_sc[g, h].T


def _gla_chunk(qk_ref, v_ref, gg_ref, alr_ref, wa_ref, ba_ref, gn_ref, ex_ref, o_ref, st_sc, *, C, SB):
    z = jnp.dot(alr_ref[...], wa_ref[...], precision=lax.Precision.HIGHEST,
                preferred_element_type=F32) + ba_ref[...]
    logg = (jnp.minimum(z, 0.0) - jnp.log1p(jnp.exp(-jnp.abs(z)))) * (1.0 / GATE_NORM)
    row = lax.broadcasted_iota(I32, logg.shape, 0)
    b = logg
    d = 1
    while d < C:
        b = b + jnp.where(row >= d, pltpu.roll(b, d, 0), 0.0)
        d *= 2
    q_all = qk_ref[:, 0:GLA_H * GLA_DK] * (GLA_DK ** -0.5)
    k_all = qk_ref[:, GLA_H * GLA_DK:2 * GLA_H * GLA_DK]
    v_all = v_ref[...]
    n_sub = C // SB
    hk = lambda a, h: a[:, h * GLA_DK:(h + 1) * GLA_DK]
    hv = lambda a, h: a[:, h * GLA_DV:(h + 1) * GLA_DV]

    b_last = b[C - 1:C, :]
    q_in = (q_all * jnp.exp(b)).astype(BF16)
    k_dec = (k_all * jnp.exp(b_last - b)).astype(BF16)
    e_last = jnp.exp(b_last)
    q_off, k_off = [], []
    for i in range(1, n_sub):
        ref = b[i * SB:i * SB + 1, :]
        q_off.append((q_all[i * SB:(i + 1) * SB] * jnp.exp(b[i * SB:(i + 1) * SB] - ref)).astype(BF16))
        k_off.append((k_all * jnp.exp(jnp.minimum(ref - b, 0.0))).astype(BF16))
    col = lax.broadcasted_iota(I32, (SB, C), 1)

    outs = []
    for h in range(GLA_H):
        vb = hv(v_all, h).astype(BF16)
        st = st_sc[h]
        o = lax.dot_general(hk(q_in, h), st.astype(BF16), NT_DIMS, preferred_element_type=F32)
        if n_sub > 1:
            rows = [jnp.zeros((SB, C), F32)]
            for i in range(1, n_sub):
                a = lax.dot_general(hk(q_off[i - 1], h), hk(k_off[i - 1], h), NT_DIMS,
                                    preferred_element_type=F32)
                rows.append(jnp.where(col < i * SB, a, 0.0))
            a_off = jnp.concatenate(rows, axis=0)
            o = o + jnp.dot(a_off.astype(BF16), vb, preferred_element_type=F32)
        st_sc[h] = st * hk(e_last, h) + lax.dot_general(vb, hk(k_dec, h), TN_DIMS, preferred_element_type=F32)
        outs.append(o)

    rmod = row % SB
    xs = []
    for dlt in range(SB):
        kd, bd = (k_all, b) if dlt == 0 else (pltpu.roll(k_all, dlt, 0), pltpu.roll(b, dlt, 0))
        x = q_all * kd * jnp.exp(jnp.minimum(b - bd, 0.0))
        xs.append(jnp.where(rmod >= dlt, x, 0.0))
    w = jnp.dot(jnp.concatenate(xs, axis=0).astype(BF16), ex_ref[...], preferred_element_type=F32)
    o_diag = w[0:C] * v_all
    for dlt in range(1, SB):
        o_diag = o_diag + w[dlt * C:(dlt + 1) * C] * pltpu.roll(v_all, dlt, 0)

    for h in range(GLA_H):
        gate = _silu(hv(gg_ref[...], h))
        o = outs[h] + hv(o_diag, h)
        o_ref[:, h * GLA_DV:(h + 1) * GLA_DV] = (_rms(o, gn_ref[...], 1e-5) * gate).astype(o_ref.dtype)


def gla(proj, w_a2p, b_a, gla_g, s0, *, C, SB, G, layer, depth, stack=None):
    B, L, _ = proj.shape
    n_chunks = L // C
    W = GLA_W
    expand = (jnp.arange(GLA_H * GLA_DK, dtype=I32)[:, None] // GLA_DK
              == jnp.arange(W, dtype=I32)[None, :] // GLA_DV).astype(BF16)
    state = (GLA_H, GLA_DK, GLA_DV)
    if stack is None:
        stacks, state_spec = [], pl.BlockSpec((depth, G) + state, lambda b, c: (0, b, 0, 0, 0))
    else:
        stacks, state_spec = [stack], pl.BlockSpec((None, G) + state, lambda b, c: (layer, b, 0, 0, 0))
    return pl.pallas_call(
        functools.partial(_gla_kernel, C=C, SB=SB, n_chunks=n_chunks, G=G, layer=layer, depth=depth,
                          n_alias=len(stacks)),
        grid=(B // G, n_chunks),
        in_specs=[
            pl.BlockSpec((G, C, W), lambda b, c: (b, c, COL_QK_GLA)),
            pl.BlockSpec((G, C, W), lambda b, c: (b, c, COL_V_GLA)),
            pl.BlockSpec((G, C, W), lambda b, c: (b, c, COL_G_GLA)),
            pl.BlockSpec((G, C, LANES), lambda b, c: (b, c, COL_ALR)),
            pl.BlockSpec((LANES, GLA_H * GLA_DK), lambda b, c: (0, 0)),
            pl.BlockSpec((1, GLA_H * GLA_DK), lambda b, c: (0, 0)),
            pl.BlockSpec((1, GLA_DV), lambda b, c: (0, 0)),
            pl.BlockSpec((GLA_H * GLA_DK, W), lambda b, c: (0, 0)),
            pl.BlockSpec((G,) + state, lambda b, c: (b, 0, 0, 0)),
        ] + [pl.BlockSpec(memory_space=pl.ANY)] * len(stacks),
        out_specs=[pl.BlockSpec((G, C, W), lambda b, c: (b, c, 0)), state_spec],
        out_shape=[jax.ShapeDtypeStruct((B, L, W), BF16),
                   jax.ShapeDtypeStruct((depth, B) + state, F32)],
        input_output_aliases={9: 1} if stacks else {},
        scratch_shapes=[pltpu.VMEM((G, GLA_H, GLA_DV, GLA_DK), F32)],
        compiler_params=_params("parallel", "arbitrary"),
        name="gla",
    )(proj, proj, proj, proj, w_a2p, b_a.reshape(1, -1), gla_g.reshape(1, -1), expand, s0, *stacks)


def _xattn_kernel(q_ref, *refs):
    k_refs, v_refs, o_ref = refs[:MEM_H], refs[MEM_H:2 * MEM_H], refs[2 * MEM_H]
    for h in range(MEM_H):
        cols = slice(h * MEM_DH, (h + 1) * MEM_DH)
        s = lax.dot_general(q_ref[:, cols], k_refs[h][...].astype(BF16), NT_DIMS, preferred_element_type=F32)
        m = jnp.max(s, axis=-1, keepdims=True)
        p = jnp.exp(s - m)
        l = jnp.sum(p, axis=-1, keepdims=True)
        o = jnp.dot(p.astype(BF16), v_refs[h][...].astype(BF16), preferred_element_type=F32) / l
        o_ref[:, cols] = o.astype(o_ref.dtype)


def xattn_core(q, mk, mv, k_spec, v_spec, *, tm):
    Bx, Lx, W = q.shape
    return pl.pallas_call(
        _xattn_kernel,
        grid=(Bx, Lx // tm),
        in_specs=([pl.BlockSpec((None, tm, W), lambda b, i: (b, i, 0))]
                  + [k_spec(h) for h in range(MEM_H)] + [v_spec(h) for h in range(MEM_H)]),
        out_specs=pl.BlockSpec((None, tm, W), lambda b, i: (b, i, 0)),
        out_shape=jax.ShapeDtypeStruct((Bx, Lx, W), BF16),
        compiler_params=_params("parallel", "arbitrary"),
        name="xattn_core",
    )(q, *([mk] * MEM_H), *([mv] * MEM_H))


def _xattn_cached_kernel(q_ref, mk_hbm, mv_hbm, o_ref, k_buf, v_buf, sem, *, layer, n_steps, G):
    b = pl.program_id(0)
    slot = b % 2

    def copies(step, buf_slot):
        out = []
        for g in range(G):
            seq = step * G + g
            for h in range(MEM_H):
                out.append(pltpu.make_async_copy(mk_hbm.at[layer, seq, :, h, :], k_buf.at[buf_slot, g, h],
                                                 sem.at[0, buf_slot]))
                out.append(pltpu.make_async_copy(mv_hbm.at[layer, seq, :, h, :], v_buf.at[buf_slot, g, h],
                                                 sem.at[1, buf_slot]))
        return out

    @pl.when(b == 0)
    def _first():
        for c in copies(0, 0):
            c.start()

    @pl.when(b + 1 < n_steps)
    def _prefetch():
        for c in copies(b + 1, 1 - slot):
            c.start()

    for c in copies(b, slot):
        c.wait()
    for g in range(G):
        _xattn_kernel(q_ref.at[g], *[k_buf.at[slot, g, h] for h in range(MEM_H)],
                      *[v_buf.at[slot, g, h] for h in range(MEM_H)], o_ref.at[g])


def xattn_cached(q, mk, mv, *, layer, G):
    DB, LS, W = q.shape
    n_mem = mk.shape[2]
    buf = pltpu.VMEM((2, G, MEM_H, n_mem, MEM_DH), F32)
    return pl.pallas_call(
        functools.partial(_xattn_cached_kernel, layer=layer, n_steps=DB // G, G=G),
        grid=(DB // G,),
        in_specs=[pl.BlockSpec((G, LS, W), lambda b: (b, 0, 0)),
                  pl.BlockSpec(memory_space=pl.ANY), pl.BlockSpec(memory_space=pl.ANY)],
        out_specs=pl.BlockSpec((G, LS, W), lambda b: (b, 0, 0)),
        out_shape=jax.ShapeDtypeStruct((DB, LS, W), BF16),
        scratch_shapes=[buf, buf, pltpu.SemaphoreType.DMA((2, 2))],
        compiler_params=_params("arbitrary"),
        name="xattn_cached",
    )(q, mk, mv)


def _ffn_kernel(te_ref, tv_ref, x_ref, g_ref, wg_ref, wu_ref, wd_ref, sc_ref, o_ref, xn_sc, acc_sc,
                *, n_f, dense):
    i = pl.program_id(0)
    f = pl.program_id(1)

    @pl.when(f == 0)
    def _init():
        if dense:
            xn_sc[...] = _rms(x_ref[...], g_ref[...], 1e-6).astype(BF16)
        else:
            parts = _from_token_major(x_ref, xn_sc.shape[1])
            ms = sum(jnp.sum(p * p, axis=-1, keepdims=True) for p in parts) * (1.0 / xn_sc.shape[1])
            inv = lax.rsqrt(ms + 1e-6)
            for c, p in enumerate(parts):
                cols = slice(c * LANES, (c + 1) * LANES)
                xn_sc[:, cols] = ((p * inv) * g_ref[:, cols]).astype(BF16)
        acc_sc[...] = jnp.zeros(acc_sc.shape, F32)

    @pl.when(tv_ref[i] != 0)
    def _compute():
        xn = xn_sc[...]
        g = jnp.dot(xn, wg_ref[...], preferred_element_type=F32)
        u = jnp.dot(xn, wu_ref[...], preferred_element_type=F32)
        a = (_silu(g) * u).astype(BF16)
        acc_sc[...] += jnp.dot(a, wd_ref[...], preferred_element_type=F32)

    @pl.when(f == n_f - 1)
    def _fin():
        if dense:
            o_ref[...] = x_ref[...] + acc_sc[...]
        else:
            _to_token_major(o_ref, acc_sc[...] * sc_ref[...])


def ffn(x, g, w_gu, w_d, tile_expert, tile_valid, row_scale, *, tm, tf, dense):
    D = w_gu.shape[1]
    Mp = row_scale.shape[0]
    F = w_d.shape[1]
    n_f = F // tf
    last = n_f - 1

    def fblk(i, f, tv):
        return f * tv[i] + last * (1 - tv[i])

    if dense:
        x_spec = pl.BlockSpec((tm, D), lambda i, f, te, tv: (i, 0))
    else:
        x_spec = pl.BlockSpec((tm * D // LANES, LANES), lambda i, f, te, tv: (i, 0))
    grid_spec = pltpu.PrefetchScalarGridSpec(
        num_scalar_prefetch=2,
        grid=(Mp // tm, n_f),
        in_specs=[
            x_spec,
            pl.BlockSpec((1, D), lambda i, f, te, tv: (0, 0)),
            pl.BlockSpec((None, D, tf), lambda i, f, te, tv: (te[i], 0, fblk(i, f, tv))),
            pl.BlockSpec((None, D, tf), lambda i, f, te, tv: (te[i], 0, n_f + fblk(i, f, tv))),
            pl.BlockSpec((None, tf, D), lambda i, f, te, tv: (te[i], fblk(i, f, tv), 0)),
            pl.BlockSpec((tm, 1), lambda i, f, te, tv: (i, 0)),
        ],
        out_specs=x_spec,
        scratch_shapes=[pltpu.VMEM((tm, D), BF16), pltpu.VMEM((tm, D), F32)],
    )
    return pl.pallas_call(
        functools.partial(_ffn_kernel, n_f=n_f, dense=dense),
        grid_spec=grid_spec,
        out_shape=jax.ShapeDtypeStruct(x.shape, F32),
        compiler_params=_params("parallel", "arbitrary"),
        name="ffn_dense" if dense else "ffn_grouped",
    )(tile_expert, tile_valid, x, g.reshape(1, D), w_gu, w_gu, w_d, row_scale)


def _router_kernel(x_ref, g_ref, wr_ref, idx_ref, gate_ref):
    xn = _rms(x_ref[...], g_ref[...], 1e-6)
    logits = jnp.dot(xn, wr_ref[...], precision=lax.Precision.HIGHEST, preferred_element_type=F32)
    lane = lax.broadcasted_iota(I32, logits.shape, 1)
    real = lane < N_EXPERTS
    logits = jnp.where(real, logits, NEG_BIG)
    e = jnp.exp(logits - jnp.max(logits, axis=-1, keepdims=True))
    probs = jnp.where(real, e / jnp.sum(e, axis=-1, keepdims=True), -1.0)
    v1 = jnp.max(probs, axis=-1, keepdims=True)
    i1 = jnp.min(jnp.where(probs == v1, lane, LANES), axis=-1, keepdims=True)
    rest = jnp.where(lane == i1, -1.0, probs)
    v2 = jnp.max(rest, axis=-1, keepdims=True)
    i2 = jnp.min(jnp.where(rest == v2, lane, LANES), axis=-1, keepdims=True)
    den = v1 + v2
    idx_ref[...] = jnp.where(lane == 0, i1, jnp.where(lane == 1, i2, 0))
    gate_ref[...] = jnp.where(lane == 0, v1 / den, jnp.where(lane == 1, v2 / den, 0.0))


def router(x, g, w_router_pad, *, tm):
    M, D = x.shape
    return pl.pallas_call(
        _router_kernel,
        grid=(M // tm,),
        in_specs=[pl.BlockSpec((tm, D), lambda i: (i, 0)),
                  pl.BlockSpec((1, D), lambda i: (0, 0)),
                  pl.BlockSpec((D, LANES), lambda i: (0, 0))],
        out_specs=[pl.BlockSpec((tm, LANES), lambda i: (i, 0)),
                   pl.BlockSpec((tm, LANES), lambda i: (i, 0))],
        out_shape=[jax.ShapeDtypeStruct((M, LANES), I32), jax.ShapeDtypeStruct((M, LANES), F32)],
        compiler_params=_params("parallel"),
        name="router",
    )(x, g.reshape(1, D), w_router_pad)


def _row_copy(src_hbm, dst_ref, src_row, dst_row, sem):
    return pltpu.make_async_copy(src_hbm.at[pl.ds(pl.multiple_of(src_row * SUBLANES, SUBLANES), SUBLANES), :],
                                 dst_ref.at[pl.ds(pl.multiple_of(dst_row * SUBLANES, SUBLANES), SUBLANES), :], sem)


def _gather_kernel(idx_ref, src_hbm, o_ref, sem, *, R):
    def start(j, carry):
        r = 2 * j
        _row_copy(src_hbm, o_ref, idx_ref[0, r], r, sem).start(priority=0)
        _row_copy(src_hbm, o_ref, idx_ref[0, r + 1], r + 1, sem).start(priority=1)
        return carry

    lax.fori_loop(0, R // 2, start, 0, unroll=4)
    pltpu.make_async_copy(src_hbm.at[pl.ds(0, R * SUBLANES), :], o_ref, sem).wait()


def gather_rows(src, idx, *, R):
    Mp = idx.shape[0]
    return pl.pallas_call(
        functools.partial(_gather_kernel, R=R),
        grid=(Mp // R,),
        in_specs=[pl.BlockSpec((None, 1, R), lambda i: (i, 0, 0), memory_space=pltpu.SMEM),
                  pl.BlockSpec(memory_space=pl.ANY)],
        out_specs=pl.BlockSpec((R * SUBLANES, LANES), lambda i: (i, 0)),
        out_shape=jax.ShapeDtypeStruct((Mp * SUBLANES, LANES), src.dtype),
        scratch_shapes=[pltpu.SemaphoreType.DMA(())],
        compiler_params=_params("arbitrary"),
        name="gather_rows",
    )(idx.reshape(Mp // R, 1, R), src)


def _combine_kernel(pos_ref, h_ref, y_hbm, g_ref, o_ref, a_sc, b_sc, sem, *, R, final_norm):
    def start(r, carry):
        _row_copy(y_hbm, a_sc, pos_ref[0, 2 * r], r, sem.at[0]).start(priority=0)
        _row_copy(y_hbm, b_sc, pos_ref[0, 2 * r + 1], r, sem.at[1]).start(priority=1)
        return carry

    lax.fori_loop(0, R, start, 0, unroll=4)
    pltpu.make_async_copy(y_hbm.at[pl.ds(0, R * SUBLANES), :], a_sc, sem.at[0]).wait()
    pltpu.make_async_copy(y_hbm.at[pl.ds(0, R * SUBLANES), :], b_sc, sem.at[1]).wait()
    n = h_ref.shape[1]
    moe = jnp.concatenate([a + b for a, b in zip(_from_token_major(a_sc, n), _from_token_major(b_sc, n))], axis=1)
    out = h_ref[...] + moe
    if final_norm:
        out = _rms(out, g_ref[...], 1e-6)
    o_ref[...] = out


def combine(h, y, pos, g, *, R, final_norm):
    M, D = h.shape
    return pl.pallas_call(
        functools.partial(_combine_kernel, R=R, final_norm=final_norm),
        grid=(M // R,),
        in_specs=[pl.BlockSpec((None, 1, 2 * R), lambda i: (i, 0, 0), memory_space=pltpu.SMEM),
                  pl.BlockSpec((R, D), lambda i: (i, 0)),
                  pl.BlockSpec(memory_space=pl.ANY),
                  pl.BlockSpec((1, D), lambda i: (0, 0))],
        out_specs=pl.BlockSpec((R, D), lambda i: (i, 0)),
        out_shape=jax.ShapeDtypeStruct((M, D), F32),
        scratch_shapes=[pltpu.VMEM((R * SUBLANES, LANES), F32), pltpu.VMEM((R * SUBLANES, LANES), F32),
                        pltpu.SemaphoreType.DMA((2,))],
        compiler_params=_params("arbitrary"),
        name="combine",
    )(pos.reshape(M // R, 1, 2 * R), h, y, g.reshape(1, D))


def _rms_only_kernel(x_ref, g_ref, o_ref):
    o_ref[...] = _rms(x_ref[...], g_ref[...], 1e-6)


def rms_only(x, g, *, tm):
    M, D = x.shape
    return pl.pallas_call(
        _rms_only_kernel,
        grid=(M // tm,),
        in_specs=[pl.BlockSpec((tm, D), lambda i: (i, 0)), pl.BlockSpec((1, D), lambda i: (0, 0))],
        out_specs=pl.BlockSpec((tm, D), lambda i: (i, 0)),
        out_shape=jax.ShapeDtypeStruct((M, D), F32),
        compiler_params=_params("parallel"),
        name="rms_only",
    )(x, g.reshape(1, D))


def moe_ffn(h, h_tok, g_norm, w_router_pad, w_gu, w_d, g_final, *, tm, r_gather, r_combine, final_norm):
    M, D = h.shape
    idx_p, gate_p = router(h, g_norm, w_router_pad, tm=min(512, M))
    expert = idx_p[:, :TOP_K].reshape(-1)
    gate = gate_p[:, :TOP_K].reshape(-1)
    n_pairs = TOP_K * M
    n_tiles = n_pairs // tm + N_EXPERTS
    Mp = n_tiles * tm
    onehot = (expert[:, None] == jnp.arange(N_EXPERTS, dtype=I32)[None, :]).astype(I32)
    csum = jnp.cumsum(onehot, axis=0)
    counts = csum[-1]
    tiles_per = (counts + tm - 1) // tm
    tile_end = jnp.cumsum(tiles_per)
    tile_start = tile_end - tiles_per
    group_start = jnp.cumsum(counts) - counts
    rank = jnp.sum(onehot * (csum - 1), axis=1)
    slot_of_pair = jnp.sum(onehot * tile_start[None, :], axis=1) * tm + rank
    _, sorted_pair, sorted_gate = lax.sort((expert, jnp.arange(n_pairs, dtype=I32), gate), num_keys=1,
                                           is_stable=True)
    tile_ids = jnp.arange(n_tiles, dtype=I32)
    tile_valid = (tile_ids < tile_end[-1]).astype(I32)
    tile_expert = jnp.minimum(jnp.sum((tile_ids[:, None] >= tile_end[None, :]).astype(I32), axis=1),
                              N_EXPERTS - 1)
    within = (tile_ids - tile_start[tile_expert]) * tm
    pos = within[:, None] + jnp.arange(tm, dtype=I32)[None, :]
    live = (tile_valid[:, None] != 0) & (pos < counts[tile_expert][:, None])
    src = jnp.clip(group_start[tile_expert][:, None] + pos, 0, n_pairs - 1).reshape(Mp)
    live = live.reshape(Mp)
    packed = jnp.stack([sorted_pair, lax.bitcast_convert_type(sorted_gate, I32)], axis=1)[src]
    token_of_slot = jnp.where(live, packed[:, 0] // TOP_K, jnp.arange(Mp, dtype=I32) % M)
    scale_of_slot = jnp.where(live, lax.bitcast_convert_type(packed[:, 1], F32), 0.0)
    last_expert = tile_expert[jnp.maximum(tile_end[-1] - 1, 0)]
    tile_expert = jnp.where(tile_valid != 0, tile_expert, last_expert)

    x_sorted = gather_rows(h_tok, token_of_slot, R=r_gather)
    y_sorted = ffn(x_sorted, g_norm, w_gu, w_d, tile_expert, tile_valid, scale_of_slot.reshape(Mp, 1),
                   tm=tm, tf=w_d.shape[1] // 2, dense=False)
    return combine(h, y_sorted, slot_of_pair, g_final, R=r_combine, final_norm=final_norm)


def kernel(x_prompt, x_sample, mem_prompt, cache_attn_k, cache_attn_v, cache_mem_k, cache_mem_v, state_gla,
           page_table, rel_bias, norm_mix, w_in, w_gla_a2, b_gla_a, gla_norm, diff_subln, lambda_q1, lambda_k1,
           lambda_q2, lambda_k2, w_out, norm_mem, norm_memkv, w_mq, w_mkv, w_mo, norm_ffn, w_ffn_gu,
           w_ffn_down, w_router, w_exp_gu, w_exp_down, norm_final):
    B, L, D = x_prompt.shape
    DB, LS, _ = x_sample.shape
    depth = w_in.shape[0]
    n_mem = mem_prompt.shape[1]
    n_pages, page = page_table.shape[1], cache_attn_k.shape[2]
    past_len = n_pages * page
    M, MS = B * L, DB * LS
    TQ_ATT, TK_ATT = 1024, 512
    GLA_C = 64

    bias_p = bias_tiles(rel_bias, R=TQ_ATT, C=TK_ATT,
                        offsets=[TK_ATT - kind * TK_ATT for kind in range(TQ_ATT // TK_ATT + 1)])
    bias_s = bias_tiles(rel_bias, R=2 * LS, C=past_len + page, offsets=(past_len,),
                        rows_per_head=LS).reshape(DIFF_H, 2 * LS, past_len + page)
    cache_k = cache_attn_k.reshape(depth, -1, page * DIFF_H, DIFF_DV)
    cache_v = cache_attn_v.reshape(depth, -1, page * DIFF_H, DIFF_DV)
    kv_heads = [(DIFF_W, DIFF_H, DIFF_DV), (2 * DIFF_W, DIFF_H, DIFF_DV)]
    mem_heads = [(0, MEM_H, MEM_DH), (MEM_H * MEM_DH, MEM_H, MEM_DH)]
    w_router_pad = jnp.pad(w_router, ((0, 0), (0, 0), (0, LANES - N_EXPERTS)))

    hp = x_prompt.reshape(M, D)
    hs = x_sample.reshape(MS, D)
    pmk, pmv = [], []
    ps = ss = None
    kv_p = kv_s = None
    for l in range(depth):
        lambda_init = 0.8 - 0.6 * math.exp(-0.3 * l)
        w_in_p = jnp.pad(w_in[l], ((0, 0), (0, N_IN_PAD - N_IN))).astype(BF16)
        w_a2p = jnp.pad(w_gla_a2[l], ((0, LANES - GLA_LR), (0, 0)))
        lamp = jnp.stack([lambda_q1[l], lambda_k1[l], lambda_q2[l], lambda_k2[l]])
        w_out_a = w_out[l, :DIFF_W].astype(BF16)
        w_out_g = w_out[l, DIFF_W:].astype(BF16)
        w_mq_b, w_mo_b, w_mkv_b = w_mq[l].astype(BF16), w_mo[l].astype(BF16), w_mkv[l].astype(BF16)
        last = l == depth - 1

        proj_p, *kv_p = rms_matmul_heads(hp, norm_mix[l], w_in_p, tm=512, keep_full=True, head_outs=kv_heads,
                                         layer=l, depth=depth, stacks=kv_p)
        proj_s, *kv_s = rms_matmul_heads(hs, norm_mix[l], w_in_p, tm=512, keep_full=True, head_outs=kv_heads,
                                         layer=l, depth=depth, stacks=kv_s)
        proj_p = proj_p.reshape(B, L, N_IN_PAD)
        proj_s = proj_s.reshape(DB, LS, N_IN_PAD)
        att_p = diff_attention_prompt(proj_p, bias_p, lamp, diff_subln[l], TQ=TQ_ATT, TK=TK_ATT,
                                      lambda_init=lambda_init)
        att_s = diff_attention_decode(proj_s, cache_k, cache_v, page_table, bias_s, lamp, diff_subln[l],
                                      layer=l, lambda_init=lambda_init)
        zero_state = jnp.zeros((B, GLA_H, GLA_DK, GLA_DV), F32)
        gla_p, ps = gla(proj_p, w_a2p, b_gla_a[l], gla_norm[l], zero_state, C=GLA_C, SB=16, G=B,
                        layer=l, depth=depth, stack=ps)
        gla_s, ss = gla(proj_s, w_a2p, b_gla_a[l], gla_norm[l], state_gla[l], C=LS, SB=LS, G=8,
                        layer=l, depth=depth, stack=ss)
        hp = mm_res([att_p.reshape(M, DIFF_W), gla_p.reshape(M, GLA_W)], [w_out_a, w_out_g], hp, tm=512)
        hs = mm_res([att_s.reshape(MS, DIFF_W), gla_s.reshape(MS, GLA_W)], [w_out_a, w_out_g], hs, tm=512)

        mkv_p, mk_p, mv_p = rms_matmul_heads(mem_prompt.reshape(B * n_mem, D), norm_memkv[l], w_mkv_b, tm=512,
                                             keep_full=True, head_outs=mem_heads)
        mkv_p = mkv_p.reshape(B, n_mem, 2 * MEM_H * MEM_DH)
        mk_p = mk_p.reshape(B, n_mem, MEM_H, MEM_DH)
        mv_p = mv_p.reshape(B, n_mem, MEM_H, MEM_DH)
        q_p = rms_matmul(hp, norm_mem[l], w_mq_b, tm=512, out_dtype=BF16, scale=MEM_DH ** -0.5)
        q_s = rms_matmul(hs, norm_mem[l], w_mq_b, tm=512, out_dtype=BF16, scale=MEM_DH ** -0.5)
        xo_p = xattn_core(
            q_p.reshape(B, L, D), mkv_p, mkv_p,
            lambda h: pl.BlockSpec((None, n_mem, MEM_DH), lambda b, i: (b, 0, h)),
            lambda h: pl.BlockSpec((None, n_mem, MEM_DH), lambda b, i: (b, 0, MEM_H + h)), tm=512)
        xo_s = xattn_cached(q_s.reshape(DB, LS, D), cache_mem_k, cache_mem_v, layer=l, G=2)
        moe_layer = l % 2 == 1
        hp = mm_res([xo_p.reshape(M, D)], [w_mo_b], hp, tm=512, token_major=moe_layer)
        hs = mm_res([xo_s.reshape(MS, D)], [w_mo_b], hs, tm=512, token_major=moe_layer)
        if moe_layer:
            (hp, hp_tok), (hs, hs_tok) = hp, hs

        if l % 2 == 0:
            w_gu = w_ffn_gu[l // 2].astype(BF16)[None]
            w_d = w_ffn_down[l // 2].astype(BF16)[None]
            for_dense = lambda h, tm: ffn(
                h, norm_ffn[l], w_gu, w_d, jnp.zeros((h.shape[0] // tm,), I32),
                jnp.ones((h.shape[0] // tm,), I32), jnp.ones((h.shape[0], 1), F32),
                tm=tm, tf=w_d.shape[1] // 2, dense=True)
            hp, hs = for_dense(hp, 512), for_dense(hs, 512)
            if last:
                hp, hs = rms_only(hp, norm_final, tm=512), rms_only(hs, norm_final, tm=512)
        else:
            w_gu = w_exp_gu[l // 2].astype(BF16)
            w_d = w_exp_down[l // 2].astype(BF16)
            hp = moe_ffn(hp, hp_tok, norm_ffn[l], w_router_pad[l // 2], w_gu, w_d, norm_final,
                         tm=512, r_gather=1024, r_combine=512, final_norm=last)
            hs = moe_ffn(hs, hs_tok, norm_ffn[l], w_router_pad[l // 2], w_gu, w_d, norm_final,
                         tm=256, r_gather=1024, r_combine=512, final_norm=last)

        pmk.append(mk_p)
        pmv.append(mv_p)

    pk, pv = (a.reshape(depth, B, L, DIFF_H, DIFF_DV) for a in kv_p)
    sk, sv = (a.reshape(depth, DB, LS, DIFF_H, DIFF_DV) for a in kv_s)
    return (hp.reshape(B, L, D), hs.reshape(DB, LS, D), pk, pv, ps,
            jnp.stack(pmk), jnp.stack(pmv), sk, sv, ss)
```

```python
import functools
import math

import numpy as np
import jax
import jax.numpy as jnp
from jax import lax
from jax.experimental import pallas as pl
from jax.experimental.pallas import tpu as pltpu

F32 = jnp.float32
BF16 = jnp.bfloat16
I32 = jnp.int32

LANES = 128
SUBLANES = 8
VMEM_LIMIT_BYTES = 56 * 1024 * 1024

D_MODEL = 1024
DIFF_H = 4
DIFF_DV = 128
DIFF_DH = 64
DIFF_W = DIFF_H * DIFF_DV
GLA_H = 4
GLA_DK = 64
GLA_DV = 128
GLA_W = GLA_H * GLA_DV
GLA_LR = 16
GATE_NORM = 16.0
N_BUCKETS = 32
MAX_DISTANCE = 128
MEM_H = 4
MEM_DH = 256
N_EXPERTS = 8
TOP_K = 2
N_IN = 3 * DIFF_W + 2 * GLA_H * GLA_DK + 2 * GLA_W + GLA_LR
N_IN_PAD = 3200
NEG_BIG = -1e30
LOG2E = math.log2(math.e)

COL_Q = 0
COL_K = 4
COL_V = 8
COL_QK_GLA = 3
COL_V_GLA = 4
COL_G_GLA = 5
COL_ALR = 24

NT_DIMS = (((1,), (1,)), ((), ()))
TN_DIMS = (((0,), (0,)), ((), ()))


def _params(*sem):
    return pltpu.CompilerParams(dimension_semantics=sem, vmem_limit_bytes=VMEM_LIMIT_BYTES)


def _rms(x, g, eps):
    ms = jnp.mean(x * x, axis=-1, keepdims=True)
    return (x * lax.rsqrt(ms + eps)) * g


def _silu(x):
    return x / (1.0 + jnp.exp(-x))


def _rms_matmul_kernel(x_ref, g_ref, w_ref, o_ref, *, eps, scale):
    xn = _rms(x_ref[...], g_ref[...], eps).astype(BF16)
    y = jnp.dot(xn, w_ref[...], preferred_element_type=F32)
    if scale != 1.0:
        y = y * scale
    o_ref[...] = y.astype(o_ref.dtype)


def rms_matmul(x, g, w, *, tm, out_dtype=F32, scale=1.0, eps=1e-6):
    M, K = x.shape
    N = w.shape[1]
    return pl.pallas_call(
        functools.partial(_rms_matmul_kernel, eps=eps, scale=scale),
        grid=(M // tm,),
        in_specs=[pl.BlockSpec((tm, K), lambda i: (i, 0)),
                  pl.BlockSpec((1, K), lambda i: (0, 0)),
                  pl.BlockSpec((K, N), lambda i: (0, 0))],
        out_specs=pl.BlockSpec((tm, N), lambda i: (i, 0)),
        out_shape=jax.ShapeDtypeStruct((M, N), out_dtype),
        compiler_params=_params("parallel"),
        name="rms_matmul",
    )(x, g.reshape(1, K), w)


def _rms_matmul_heads_kernel(x_ref, g_ref, w_ref, *refs, eps, keep_full, head_outs, n_alias, layer, depth):
    o_refs = refs[n_alias:]
    xn = _rms(x_ref[...], g_ref[...], eps).astype(BF16)
    y = jnp.dot(xn, w_ref[...], preferred_element_type=F32)
    if keep_full:
        o_refs[0][...] = y
    for o_ref, (col0, n_heads, width) in zip(o_refs[1 if keep_full else 0:], head_outs):
        if n_alias == 0:
            for other in range(depth):
                if other != layer:
                    o_ref[other] = jnp.zeros(o_ref.shape[1:], F32)
            o_ref = o_ref.at[layer]
        for h in range(n_heads):
            o_ref[:, h, :] = y[:, col0 + h * width:col0 + (h + 1) * width]


def rms_matmul_heads(x, g, w, *, tm, keep_full, head_outs, layer=0, depth=1, stacks=None, eps=1e-6):
    M, K = x.shape
    N = w.shape[1]
    stacks = [] if stacks is None else list(stacks)
    if stacks:
        out_specs = [pl.BlockSpec((None, tm, nh, wd), lambda i: (layer, i, 0, 0)) for _, nh, wd in head_outs]
    else:
        out_specs = [pl.BlockSpec((depth, tm, nh, wd), lambda i: (0, i, 0, 0)) for _, nh, wd in head_outs]
    out_shape = [jax.ShapeDtypeStruct((depth, M, nh, wd), F32) for _, nh, wd in head_outs]
    if keep_full:
        out_specs.insert(0, pl.BlockSpec((tm, N), lambda i: (i, 0)))
        out_shape.insert(0, jax.ShapeDtypeStruct((M, N), F32))
    first_stack_out = 1 if keep_full else 0
    return pl.pallas_call(
        functools.partial(_rms_matmul_heads_kernel, eps=eps, keep_full=keep_full, head_outs=tuple(head_outs),
                          n_alias=len(stacks), layer=layer, depth=depth),
        grid=(M // tm,),
        in_specs=[pl.BlockSpec((tm, K), lambda i: (i, 0)),
                  pl.BlockSpec((1, K), lambda i: (0, 0)),
                  pl.BlockSpec((K, N), lambda i: (0, 0))] + [pl.BlockSpec(memory_space=pl.ANY)] * len(stacks),
        out_specs=out_specs,
        out_shape=out_shape,
        input_output_aliases={3 + j: first_stack_out + j for j in range(len(stacks))},
        compiler_params=_params("parallel"),
        name="rms_matmul_heads",
    )(x, g.reshape(1, K), w, *stacks)


def _to_token_major(o_ref, y):
    rows, n = y.shape
    for c in range(n // LANES):
        o_ref[pl.ds(c, rows, stride=n // LANES), :] = y[:, c * LANES:(c + 1) * LANES]


def _from_token_major(x_ref, n):
    rows = x_ref.shape[0] // (n // LANES)
    return [x_ref[pl.ds(c, rows, stride=n // LANES), :] for c in range(n // LANES)]


def _mm_res_kernel(*refs, n_lhs, token_major):
    a_refs, w_refs = refs[:n_lhs], refs[n_lhs:2 * n_lhs]
    res_ref, o_ref = refs[2 * n_lhs], refs[2 * n_lhs + 1]
    acc = res_ref[...]
    for a_ref, w_ref in zip(a_refs, w_refs):
        acc = acc + jnp.dot(a_ref[...], w_ref[...], preferred_element_type=F32)
    o_ref[...] = acc
    if token_major:
        _to_token_major(refs[2 * n_lhs + 2], acc)


def mm_res(lhs, ws, res, *, tm, token_major=False):
    M, N = res.shape
    n = len(lhs)
    in_specs = ([pl.BlockSpec((tm, a.shape[1]), lambda i: (i, 0)) for a in lhs]
                + [pl.BlockSpec(w.shape, lambda i: (0, 0)) for w in ws]
                + [pl.BlockSpec((tm, N), lambda i: (i, 0))])
    out_specs = [pl.BlockSpec((tm, N), lambda i: (i, 0))]
    out_shape = [jax.ShapeDtypeStruct((M, N), F32)]
    if token_major:
        out_specs.append(pl.BlockSpec((tm * N // LANES, LANES), lambda i: (i, 0)))
        out_shape.append(jax.ShapeDtypeStruct((M * N // LANES, LANES), F32))
    out = pl.pallas_call(
        functools.partial(_mm_res_kernel, n_lhs=n, token_major=token_major),
        grid=(M // tm,),
        in_specs=in_specs,
        out_specs=out_specs,
        out_shape=out_shape,
        compiler_params=_params("parallel"),
        name="mm_res",
    )(*lhs, *ws, res)
    return out if token_major else out[0]


def _bias_kernel(tab_ref, o_ref, *, offsets, rows_per_head):
    h = pl.program_id(0)
    R, C = o_ref.shape[-2], o_ref.shape[-1]
    max_exact = N_BUCKETS // 2
    far = tab_ref[N_BUCKETS - 1, h]
    rb = min(R, rows_per_head, LANES)
    for kind, off in enumerate(offsets):
        for r0 in range(0, R, rb):
            for c0 in range(0, C, LANES):
                lo = off + r0 % rows_per_head - (c0 + LANES - 1)
                hi = off + r0 % rows_per_head + rb - 1 - c0
                blk = (kind, slice(r0, r0 + rb), slice(c0, c0 + LANES))
                if lo >= MAX_DISTANCE:
                    o_ref[blk] = jnp.zeros((rb, LANES), F32)
                    continue
                if hi < 0:
                    o_ref[blk] = jnp.full((rb, LANES), NEG_BIG, F32)
                    continue
                r = lax.broadcasted_iota(I32, (rb, LANES), 0)
                c = lax.broadcasted_iota(I32, (rb, LANES), 1)
                rel = (off + r0 % rows_per_head - c0) + r - c
                n = jnp.maximum(rel, 0)
                nf = jnp.maximum(n, max_exact).astype(F32)
                large = max_exact + (jnp.log(nf / max_exact) / math.log(MAX_DISTANCE / max_exact)
                                     * (N_BUCKETS - max_exact)).astype(I32)
                bucket = jnp.where(n < max_exact, n, jnp.minimum(large, N_BUCKETS - 1))
                acc = jnp.zeros((rb, LANES), F32)
                for b in range(N_BUCKETS - 1):
                    acc = jnp.where(bucket == b, (tab_ref[b, h] - far) * LOG2E, acc)
                o_ref[blk] = jnp.where(rel >= 0, acc, NEG_BIG)


def bias_tiles(rel_bias, *, R, C, offsets, rows_per_head=None):
    rows_per_head = R if rows_per_head is None else rows_per_head
    return pl.pallas_call(
        functools.partial(_bias_kernel, offsets=tuple(offsets), rows_per_head=rows_per_head),
        grid=(DIFF_H,),
        in_specs=[pl.BlockSpec(memory_space=pltpu.SMEM)],
        out_specs=pl.BlockSpec((None, len(offsets), R, C), lambda h: (h, 0, 0, 0)),
        out_shape=jax.ShapeDtypeStruct((DIFF_H, len(offsets), R, C), F32),
        compiler_params=_params("arbitrary"),
        name="bias_tiles",
    )(rel_bias)


def _lambda_value(lamp, lambda_init):
    s1 = jnp.sum(lamp[0:1, :] * lamp[1:2, :], axis=-1, keepdims=True)
    s2 = jnp.sum(lamp[2:3, :] * lamp[3:4, :], axis=-1, keepdims=True)
    return jnp.exp(s1) - jnp.exp(s2) + lambda_init


def _diff_finish(o1, o2, lam, subln, lambda_init):
    o = o1 - lam * o2
    return _rms(o, subln, 1e-5) * (1.0 - lambda_init)


def _diff_attn_kernel(qi_tab, ki_tab, q_ref, k_ref, v_ref, bias_ref, lamp_ref, subln_ref, o_ref,
                      q2_sc, m_sc, acc_sc, *, TQ, TK, RB, lambda_init):
    t = pl.program_id(2)
    qi = qi_tab[t]
    ki = ki_tab[t]
    n = TQ // TK
    kind_now = ki - qi * n + 1

    @pl.when(ki == 0)
    def _init():
        q = q_ref[...] * (DIFF_DH ** -0.5 * LOG2E)
        lane = lax.broadcasted_iota(I32, q.shape, 1)
        q2_sc[0:TQ, :] = jnp.where(lane < DIFF_DH, q, 0.0).astype(BF16)
        q2_sc[TQ:2 * TQ, :] = jnp.where(lane >= DIFF_DH, q, 0.0).astype(BF16)
        m_sc[...] = jnp.full(m_sc.shape, NEG_BIG, F32)
        acc_sc[...] = jnp.zeros(acc_sc.shape, F32)

    def update(kind):
        kb = k_ref[...].astype(BF16)
        vb = jnp.concatenate([v_ref[...].astype(BF16), jnp.ones((TK, DIFF_DV), BF16)], axis=1)
        for r0 in range(0, 2 * TQ, RB):
            rows = slice(r0, r0 + RB)
            q_row = r0 % TQ
            n_k = TK if kind is None else max(0, min(TK, TK - kind * TK + q_row + RB))
            if n_k == 0:
                continue
            s = lax.dot_general(q2_sc[rows, :], kb[0:n_k], NT_DIMS, preferred_element_type=F32)
            if kind is not None:
                s = s + bias_ref[kind, q_row:q_row + RB, 0:n_k]
            cols = [s[:, c * LANES:(c + 1) * LANES] for c in range(n_k // LANES)]
            m_old = m_sc[rows, :]
            m_new = jnp.maximum(m_old, jnp.max(functools.reduce(jnp.maximum, cols), axis=-1, keepdims=True))
            alpha = jnp.exp2(m_old - m_new)
            p = jnp.concatenate([jnp.exp2(c - m_new) for c in cols], axis=1).astype(BF16)
            acc_sc[rows, :] = (jnp.concatenate([alpha, alpha], axis=1) * acc_sc[rows, :]
                               + jnp.dot(p, vb[0:n_k], preferred_element_type=F32))
            m_sc[rows, :] = m_new

    @pl.when(kind_now < 0)
    def _far():
        update(None)

    for kind in range(n + 1):
        @pl.when(kind_now == kind)
        def _near(kind=kind):
            update(kind)
            if kind == n:
                on = acc_sc[:, 0:DIFF_DV] / acc_sc[:, DIFF_DV:2 * DIFF_DV]
                lam = _lambda_value(lamp_ref[...], lambda_init)
                o_ref[...] = _diff_finish(on[0:TQ], on[TQ:2 * TQ], lam, subln_ref[...],
                                          lambda_init).astype(o_ref.dtype)


def diff_attention_prompt(proj, bias, lamp, subln, *, TQ, TK, lambda_init):
    B, L, _ = proj.shape
    n = TQ // TK
    pairs = [(qi, ki) for qi in range(L // TQ) for ki in range((qi + 1) * n)]
    qi_tab = jnp.asarray(np.array([p[0] for p in pairs], np.int32))
    ki_tab = jnp.asarray(np.array([p[1] for p in pairs], np.int32))
    grid_spec = pltpu.PrefetchScalarGridSpec(
        num_scalar_prefetch=2,
        grid=(B, DIFF_H, len(pairs)),
        in_specs=[
            pl.BlockSpec((None, TQ, DIFF_DV), lambda b, h, t, qt, kt: (b, qt[t], COL_Q + h)),
            pl.BlockSpec((None, TK, DIFF_DV), lambda b, h, t, qt, kt: (b, kt[t], COL_K + h)),
            pl.BlockSpec((None, TK, DIFF_DV), lambda b, h, t, qt, kt: (b, kt[t], COL_V + h)),
            pl.BlockSpec((None, n + 1, TQ, TK), lambda b, h, t, qt, kt: (h, 0, 0, 0)),
            pl.BlockSpec((4, DIFF_DH), lambda b, h, t, qt, kt: (0, 0)),
            pl.BlockSpec((1, DIFF_DV), lambda b, h, t, qt, kt: (0, 0)),
        ],
        out_specs=pl.BlockSpec((None, TQ, DIFF_DV), lambda b, h, t, qt, kt: (b, qt[t], h)),
        scratch_shapes=[pltpu.VMEM((2 * TQ, DIFF_DV), BF16),
                        pltpu.VMEM((2 * TQ, LANES), F32),
                        pltpu.VMEM((2 * TQ, 2 * DIFF_DV), F32)],
    )
    return pl.pallas_call(
        functools.partial(_diff_attn_kernel, TQ=TQ, TK=TK, RB=LANES, lambda_init=lambda_init),
        grid_spec=grid_spec,
        out_shape=jax.ShapeDtypeStruct((B, L, DIFF_W), BF16),
        compiler_params=_params("parallel", "parallel", "arbitrary"),
        name="diff_attn_prompt",
    )(qi_tab, ki_tab, proj, proj, proj, bias, lamp, subln.reshape(1, DIFF_DV))


def _diff_decode_kernel(pt_ref, proj_ref, bias_ref, lamp_ref, subln_ref, ck_hbm, cv_hbm, o_ref, k_buf, v_buf, sem,
                        *, layer, n_seq, n_pages, page, lq, lambda_init):
    b = pl.program_id(0)
    slot = b % 2

    def start_pages(seq, buf_slot):
        for j in range(n_pages):
            pg = pt_ref[seq, j]
            pltpu.make_async_copy(ck_hbm.at[layer, pg], k_buf.at[buf_slot, j], sem.at[0, buf_slot]).start()
            pltpu.make_async_copy(cv_hbm.at[layer, pg], v_buf.at[buf_slot, j], sem.at[1, buf_slot]).start()

    @pl.when(b == 0)
    def _first():
        start_pages(0, 0)

    @pl.when(b + 1 < n_seq)
    def _prefetch():
        start_pages(b + 1, 1 - slot)

    pltpu.make_async_copy(ck_hbm.at[layer, pl.ds(0, n_pages)], k_buf.at[slot], sem.at[0, slot]).wait()
    pltpu.make_async_copy(cv_hbm.at[layer, pl.ds(0, n_pages)], v_buf.at[slot], sem.at[1, slot]).wait()
    k_refs = [k_buf.at[slot, j] for j in range(n_pages)]
    v_refs = [v_buf.at[slot, j] for j in range(n_pages)]
    lam = _lambda_value(lamp_ref[...], lambda_init)
    lane = lax.broadcasted_iota(I32, (lq, DIFF_DV), 1)
    pad = jnp.zeros((page - lq, DIFF_DV), F32)
    head_rows = [pl.ds(h, page, stride=DIFF_H) for h in range(DIFF_H)]
    head_cols = [slice(h * DIFF_DV, (h + 1) * DIFF_DV) for h in range(DIFF_H)]
    scores = []
    for h in range(DIFF_H):
        q = proj_ref[:, head_cols[h]] * (DIFF_DH ** -0.5 * LOG2E)
        qbd = jnp.concatenate([jnp.where(lane < DIFF_DH, q, 0.0), jnp.where(lane >= DIFF_DH, q, 0.0)],
                              axis=0).astype(BF16)
        k_tail = jnp.concatenate([proj_ref[:, DIFF_W + h * DIFF_DV:DIFF_W + (h + 1) * DIFF_DV], pad], 0)
        parts = [lax.dot_general(qbd, k_refs[j][head_rows[h], :].astype(BF16), NT_DIMS,
                                 preferred_element_type=F32) for j in range(n_pages)]
        parts.append(lax.dot_general(qbd, k_tail.astype(BF16), NT_DIMS, preferred_element_type=F32))
        scores.append(jnp.concatenate(parts, axis=1) + bias_ref[h])
    s = jnp.concatenate(scores, axis=0)
    p = jnp.exp2(s - jnp.max(s, axis=-1, keepdims=True))
    l = jnp.sum(p, axis=-1, keepdims=True)
    pb = p.astype(BF16)
    for h in range(DIFF_H):
        hr = slice(2 * lq * h, 2 * lq * (h + 1))
        v_tail = jnp.concatenate([proj_ref[:, 2 * DIFF_W + h * DIFF_DV:2 * DIFF_W + (h + 1) * DIFF_DV], pad], 0)
        acc = jnp.dot(pb[hr, n_pages * page:], v_tail.astype(BF16), preferred_element_type=F32)
        for j in range(n_pages):
            acc = acc + jnp.dot(pb[hr, j * page:(j + 1) * page], v_refs[j][head_rows[h], :].astype(BF16),
                                preferred_element_type=F32)
        on = acc / l[hr]
        o = _diff_finish(on[0:lq], on[lq:2 * lq], lam, subln_ref[...], lambda_init)
        o_ref[:, head_cols[h]] = o.astype(o_ref.dtype)


def diff_attention_decode(proj, cache_k, cache_v, page_table, bias, lamp, subln, *, layer, lambda_init):
    DB, lq, _ = proj.shape
    n_pages = page_table.shape[1]
    page = cache_k.shape[2] // DIFF_H

    page_buf = pltpu.VMEM((2, n_pages, page * DIFF_H, DIFF_DV), F32)
    grid_spec = pltpu.PrefetchScalarGridSpec(
        num_scalar_prefetch=1,
        grid=(DB,),
        in_specs=[pl.BlockSpec((None, lq, 3 * DIFF_W), lambda b, pt: (b, 0, 0)),
                  pl.BlockSpec(bias.shape, lambda b, pt: (0, 0, 0)),
                  pl.BlockSpec((4, DIFF_DH), lambda b, pt: (0, 0)),
                  pl.BlockSpec((1, DIFF_DV), lambda b, pt: (0, 0)),
                  pl.BlockSpec(memory_space=pl.ANY), pl.BlockSpec(memory_space=pl.ANY)],
        out_specs=pl.BlockSpec((None, lq, DIFF_W), lambda b, pt: (b, 0, 0)),
        scratch_shapes=[page_buf, page_buf, pltpu.SemaphoreType.DMA((2, 2))],
    )
    return pl.pallas_call(
        functools.partial(_diff_decode_kernel, layer=layer, n_seq=DB, n_pages=n_pages, page=page, lq=lq,
                          lambda_init=lambda_init),
        grid_spec=grid_spec,
        out_shape=jax.ShapeDtypeStruct((DB, lq, DIFF_W), BF16),
        compiler_params=_params("arbitrary"),
        name="diff_attn_decode",
    )(page_table, proj, bias, lamp, subln.reshape(1, DIFF_DV), cache_k, cache_v)


def _gla_kernel(qk_ref, v_ref, gg_ref, alr_ref, wa_ref, ba_ref, gn_ref, ex_ref, s0_ref, *refs,
                C, SB, n_chunks, G, layer, depth, n_alias):
    o_ref, sout_ref, st_sc = refs[n_alias:]
    c = pl.program_id(1)

    @pl.when(c == 0)
    def _init():
        for g in range(G):
            for h in range(GLA_H):
                st_sc[g, h] = s0_ref[g, h].T

    for g in range(G):
        _gla_chunk(qk_ref.at[g], v_ref.at[g], gg_ref.at[g], alr_ref.at[g], wa_ref, ba_ref, gn_ref, ex_ref,
                   o_ref.at[g], st_sc.at[g], C=C, SB=SB)

    @pl.when(c == n_chunks - 1)
    def _fin():
        out = sout_ref
        if n_alias == 0:
            for other in range(depth):
                if other != layer:
                    sout_ref[other] = jnp.zeros(sout_ref.shape[1:], F32)
            out = sout_ref.at[layer]
        for g in range(G):
            for h in range(GLA_H):
                out[g, h] = st_sc[g, h].T


def _gla_chunk(qk_ref, v_ref, gg_ref, alr_ref, wa_ref, ba_ref, gn_ref, ex_ref, o_ref, st_sc, *, C, SB):
    z = jnp.dot(alr_ref[...], wa_ref[...], precision=lax.Precision.HIGHEST,
                preferred_element_type=F32) + ba_ref[...]
    logg = (jnp.minimum(z, 0.0) - jnp.log1p(jnp.exp(-jnp.abs(z)))) * (1.0 / GATE_NORM)
    row = lax.broadcasted_iota(I32, logg.shape, 0)
    b = logg
    d = 1
    while d < C:
        b = b + jnp.where(row >= d, pltpu.roll(b, d, 0), 0.0)
        d *= 2
    q_all = qk_ref[:, 0:GLA_H * GLA_DK] * (GLA_DK ** -0.5)
    k_all = qk_ref[:, GLA_H * GLA_DK:2 * GLA_H * GLA_DK]
    v_all = v_ref[...]
    n_sub = C // SB
    hk = lambda a, h: a[:, h * GLA_DK:(h + 1) * GLA_DK]
    hv = lambda a, h: a[:, h * GLA_DV:(h + 1) * GLA_DV]

    b_last = b[C - 1:C, :]
    q_in = (q_all * jnp.exp(b)).astype(BF16)
    k_dec = (k_all * jnp.exp(b_last - b)).astype(BF16)
    e_last = jnp.exp(b_last)
    q_off, k_off = [], []
    for i in range(1, n_sub):
        ref = b[i * SB:i * SB + 1, :]
        q_off.append((q_all[i * SB:(i + 1) * SB] * jnp.exp(b[i * SB:(i + 1) * SB] - ref)).astype(BF16))
        k_off.append((k_all * jnp.exp(jnp.minimum(ref - b, 0.0))).astype(BF16))
    col = lax.broadcasted_iota(I32, (SB, C), 1)

    outs = []
    for h in range(GLA_H):
        vb = hv(v_all, h).astype(BF16)
        st = st_sc[h]
        o = lax.dot_general(hk(q_in, h), st.astype(BF16), NT_DIMS, preferred_element_type=F32)
        if n_sub > 1:
            rows = [jnp.zeros((SB, C), F32)]
            for i in range(1, n_sub):
                a = lax.dot_general(hk(q_off[i - 1], h), hk(k_off[i - 1], h), NT_DIMS,
                                    preferred_element_type=F32)
                rows.append(jnp.where(col < i * SB, a, 0.0))
            a_off = jnp.concatenate(rows, axis=0)
            o = o + jnp.dot(a_off.astype(BF16), vb, preferred_element_type=F32)
        st_sc[h] = st * hk(e_last, h) + lax.dot_general(vb, hk(k_dec, h), TN_DIMS, preferred_element_type=F32)
        outs.append(o)

    rmod = row % SB
    xs = []
    for dlt in range(SB):
        kd, bd = (k_all, b) if dlt == 0 else (pltpu.roll(k_all, dlt, 0), pltpu.roll(b, dlt, 0))
        x = q_all * kd * jnp.exp(jnp.minimum(b - bd, 0.0))
        xs.append(jnp.where(rmod >= dlt, x, 0.0))
    w = jnp.dot(jnp.concatenate(xs, axis=0).astype(BF16), ex_ref[...], preferred_element_type=F32)
    o_diag = w[0:C] * v_all
    for dlt in range(1, SB):
        o_diag = o_diag + w[dlt * C:(dlt + 1) * C] * pltpu.roll(v_all, dlt, 0)

    for h in range(GLA_H):
        gate = _silu(hv(gg_ref[...], h))
        o = outs[h] + hv(o_diag, h)
        o_ref[:, h * GLA_DV:(h + 1) * GLA_DV] = (_rms(o, gn_ref[...], 1e-5) * gate).astype(o_ref.dtype)


def gla(proj, w_a2p, b_a, gla_g, s0, *, C, SB, G, layer, depth, stack=None):
    B, L, _ = proj.shape
    n_chunks = L // C
    W = GLA_W
    expand = (jnp.arange(GLA_H * GLA_DK, dtype=I32)[:, None] // GLA_DK
              == jnp.arange(W, dtype=I32)[None, :] // GLA_DV).astype(BF16)
    state = (GLA_H, GLA_DK, GLA_DV)
    if stack is None:
        stacks, state_spec = [], pl.BlockSpec((depth, G) + state, lambda b, c: (0, b, 0, 0, 0))
    else:
        stacks, state_spec = [stack], pl.BlockSpec((None, G) + state, lambda b, c: (layer, b, 0, 0, 0))
    return pl.pallas_call(
        functools.partial(_gla_kernel, C=C, SB=SB, n_chunks=n_chunks, G=G, layer=layer, depth=depth,
                          n_alias=len(stacks)),
        grid=(B // G, n_chunks),
        in_specs=[
            pl.BlockSpec((G, C, W), lambda b, c: (b, c, COL_QK_GLA)),
            pl.BlockSpec((G, C, W), lambda b, c: (b, c, COL_V_GLA)),
            pl.BlockSpec((G, C, W), lambda b, c: (b, c, COL_G_GLA)),
            pl.BlockSpec((G, C, LANES), lambda b, c: (b, c, COL_ALR)),
            pl.BlockSpec((LANES, GLA_H * GLA_DK), lambda b, c: (0, 0)),
            pl.BlockSpec((1, GLA_H * GLA_DK), lambda b, c: (0, 0)),
            pl.BlockSpec((1, GLA_DV), lambda b, c: (0, 0)),
            pl.BlockSpec((GLA_H * GLA_DK, W), lambda b, c: (0, 0)),
            pl.BlockSpec((G,) + state, lambda b, c: (b, 0, 0, 0)),
        ] + [pl.BlockSpec(memory_space=pl.ANY)] * len(stacks),
        out_specs=[pl.BlockSpec((G, C, W), lambda b, c: (b, c, 0)), state_spec],
        out_shape=[jax.ShapeDtypeStruct((B, L, W), BF16),
                   jax.ShapeDtypeStruct((depth, B) + state, F32)],
        input_output_aliases={9: 1} if stacks else {},
        scratch_shapes=[pltpu.VMEM((G, GLA_H, GLA_DV, GLA_DK), F32)],
        compiler_params=_params("parallel", "arbitrary"),
        name="gla",
    )(proj, proj, proj, proj, w_a2p, b_a.reshape(1, -1), gla_g.reshape(1, -1), expand, s0, *stacks)


def _xattn_kernel(q_ref, *refs):
    k_refs, v_refs, o_ref = refs[:MEM_H], refs[MEM_H:2 * MEM_H], refs[2 * MEM_H]
    rows = q_ref.shape[0]
    s = jnp.concatenate(
        [lax.dot_general(q_ref[:, h * MEM_DH:(h + 1) * MEM_DH], k_refs[h][...].astype(BF16), NT_DIMS,
                         preferred_element_type=F32) for h in range(MEM_H)], axis=0)
    p = jnp.exp(s - jnp.max(s, axis=-1, keepdims=True))
    l = jnp.sum(p, axis=-1, keepdims=True)
    for h in range(MEM_H):
        hr = slice(h * rows, (h + 1) * rows)
        o = jnp.dot(p[hr].astype(BF16), v_refs[h][...].astype(BF16), preferred_element_type=F32) / l[hr]
        o_ref[:, h * MEM_DH:(h + 1) * MEM_DH] = o.astype(o_ref.dtype)


def xattn_core(q, mk, mv, k_spec, v_spec, *, tm):
    Bx, Lx, W = q.shape
    return pl.pallas_call(
        _xattn_kernel,
        grid=(Bx, Lx // tm),
        in_specs=([pl.BlockSpec((None, tm, W), lambda b, i: (b, i, 0))]
                  + [k_spec(h) for h in range(MEM_H)] + [v_spec(h) for h in range(MEM_H)]),
        out_specs=pl.BlockSpec((None, tm, W), lambda b, i: (b, i, 0)),
        out_shape=jax.ShapeDtypeStruct((Bx, Lx, W), BF16),
        compiler_params=_params("parallel", "arbitrary"),
        name="xattn_core",
    )(q, *([mk] * MEM_H), *([mv] * MEM_H))


def _xattn_cached_kernel(q_ref, mk_hbm, mv_hbm, o_ref, k_buf, v_buf, sem, *, layer, n_steps, G):
    b = pl.program_id(0)
    slot = b % 2

    def copies(step, buf_slot):
        out = []
        for g in range(G):
            seq = step * G + g
            for h in range(MEM_H):
                out.append(pltpu.make_async_copy(mk_hbm.at[layer, seq, :, h, :], k_buf.at[buf_slot, g, h],
                                                 sem.at[0, buf_slot]))
                out.append(pltpu.make_async_copy(mv_hbm.at[layer, seq, :, h, :], v_buf.at[buf_slot, g, h],
                                                 sem.at[1, buf_slot]))
        return out

    @pl.when(b == 0)
    def _first():
        for c in copies(0, 0):
            c.start()

    @pl.when(b + 1 < n_steps)
    def _prefetch():
        for c in copies(b + 1, 1 - slot):
            c.start()

    for c in copies(b, slot):
        c.wait()
    for g in range(G):
        _xattn_kernel(q_ref.at[g], *[k_buf.at[slot, g, h] for h in range(MEM_H)],
                      *[v_buf.at[slot, g, h] for h in range(MEM_H)], o_ref.at[g])


def xattn_cached(q, mk, mv, *, layer, G):
    DB, LS, W = q.shape
    n_mem = mk.shape[2]
    buf = pltpu.VMEM((2, G, MEM_H, n_mem, MEM_DH), F32)
    return pl.pallas_call(
        functools.partial(_xattn_cached_kernel, layer=layer, n_steps=DB // G, G=G),
        grid=(DB // G,),
        in_specs=[pl.BlockSpec((G, LS, W), lambda b: (b, 0, 0)),
                  pl.BlockSpec(memory_space=pl.ANY), pl.BlockSpec(memory_space=pl.ANY)],
        out_specs=pl.BlockSpec((G, LS, W), lambda b: (b, 0, 0)),
        out_shape=jax.ShapeDtypeStruct((DB, LS, W), BF16),
        scratch_shapes=[buf, buf, pltpu.SemaphoreType.DMA((2, 2))],
        compiler_params=_params("arbitrary"),
        name="xattn_cached",
    )(q, mk, mv)


def _ffn_kernel(te_ref, tv_ref, x_ref, g_ref, wg_ref, wu_ref, wd_ref, sc_ref, o_ref, xn_sc, acc_sc,
                *, n_f, dense):
    i = pl.program_id(0)
    f = pl.program_id(1)

    @pl.when(f == 0)
    def _init():
        if dense:
            xn_sc[...] = _rms(x_ref[...], g_ref[...], 1e-6).astype(BF16)
        else:
            parts = _from_token_major(x_ref, xn_sc.shape[1])
            ms = sum(jnp.sum(p * p, axis=-1, keepdims=True) for p in parts) * (1.0 / xn_sc.shape[1])
            inv = lax.rsqrt(ms + 1e-6)
            for c, p in enumerate(parts):
                cols = slice(c * LANES, (c + 1) * LANES)
                xn_sc[:, cols] = ((p * inv) * g_ref[:, cols]).astype(BF16)
        acc_sc[...] = jnp.zeros(acc_sc.shape, F32)

    @pl.when(tv_ref[i] != 0)
    def _compute():
        xn = xn_sc[...]
        g = jnp.dot(xn, wg_ref[...], preferred_element_type=F32)
        u = jnp.dot(xn, wu_ref[...], preferred_element_type=F32)
        a = (_silu(g) * u).astype(BF16)
        acc_sc[...] += jnp.dot(a, wd_ref[...], preferred_element_type=F32)

    @pl.when(f == n_f - 1)
    def _fin():
        if dense:
            o_ref[...] = x_ref[...] + acc_sc[...]
        else:
            _to_token_major(o_ref, acc_sc[...] * sc_ref[...])


def ffn(x, g, w_gu, w_d, tile_expert, tile_valid, row_scale, *, tm, tf, dense):
    D = w_gu.shape[1]
    Mp = row_scale.shape[0]
    F = w_d.shape[1]
    n_f = F // tf
    last = n_f - 1

    def fblk(i, f, tv):
        return f * tv[i] + last * (1 - tv[i])

    if dense:
        x_spec = pl.BlockSpec((tm, D), lambda i, f, te, tv: (i, 0))
    else:
        x_spec = pl.BlockSpec((tm * D // LANES, LANES), lambda i, f, te, tv: (i, 0))
    grid_spec = pltpu.PrefetchScalarGridSpec(
        num_scalar_prefetch=2,
        grid=(Mp // tm, n_f),
        in_specs=[
            x_spec,
            pl.BlockSpec((1, D), lambda i, f, te, tv: (0, 0)),
            pl.BlockSpec((None, D, tf), lambda i, f, te, tv: (te[i], 0, fblk(i, f, tv))),
            pl.BlockSpec((None, D, tf), lambda i, f, te, tv: (te[i], 0, n_f + fblk(i, f, tv))),
            pl.BlockSpec((None, tf, D), lambda i, f, te, tv: (te[i], fblk(i, f, tv), 0)),
            pl.BlockSpec((tm, 1), lambda i, f, te, tv: (i, 0)),
        ],
        out_specs=x_spec,
        scratch_shapes=[pltpu.VMEM((tm, D), BF16), pltpu.VMEM((tm, D), F32)],
    )
    return pl.pallas_call(
        functools.partial(_ffn_kernel, n_f=n_f, dense=dense),
        grid_spec=grid_spec,
        out_shape=jax.ShapeDtypeStruct(x.shape, F32),
        compiler_params=_params("parallel", "arbitrary"),
        name="ffn_dense" if dense else "ffn_grouped",
    )(tile_expert, tile_valid, x, g.reshape(1, D), w_gu, w_gu, w_d, row_scale)


def _router_kernel(x_ref, g_ref, wr_ref, idx_ref, gate_ref):
    xn = _rms(x_ref[...], g_ref[...], 1e-6)
    logits = jnp.dot(xn, wr_ref[...], precision=lax.Precision.HIGHEST, preferred_element_type=F32)
    lane = lax.broadcasted_iota(I32, logits.shape, 1)
    real = lane < N_EXPERTS
    logits = jnp.where(real, logits, NEG_BIG)
    e = jnp.exp(logits - jnp.max(logits, axis=-1, keepdims=True))
    probs = jnp.where(real, e / jnp.sum(e, axis=-1, keepdims=True), -1.0)
    v1 = jnp.max(probs, axis=-1, keepdims=True)
    i1 = jnp.min(jnp.where(probs == v1, lane, LANES), axis=-1, keepdims=True)
    rest = jnp.where(lane == i1, -1.0, probs)
    v2 = jnp.max(rest, axis=-1, keepdims=True)
    i2 = jnp.min(jnp.where(rest == v2, lane, LANES), axis=-1, keepdims=True)
    den = v1 + v2
    idx_ref[...] = jnp.where(lane == 0, i1, jnp.where(lane == 1, i2, 0))
    gate_ref[...] = jnp.where(lane == 0, v1 / den, jnp.where(lane == 1, v2 / den, 0.0))


def router(x, g, w_router_pad, *, tm):
    M, D = x.shape
    return pl.pallas_call(
        _router_kernel,
        grid=(M // tm,),
        in_specs=[pl.BlockSpec((tm, D), lambda i: (i, 0)),
                  pl.BlockSpec((1, D), lambda i: (0, 0)),
                  pl.BlockSpec((D, LANES), lambda i: (0, 0))],
        out_specs=[pl.BlockSpec((tm, LANES), lambda i: (i, 0)),
                   pl.BlockSpec((tm, LANES), lambda i: (i, 0))],
        out_shape=[jax.ShapeDtypeStruct((M, LANES), I32), jax.ShapeDtypeStruct((M, LANES), F32)],
        compiler_params=_params("parallel"),
        name="router",
    )(x, g.reshape(1, D), w_router_pad)


def _row_copy(src_hbm, dst_ref, src_row, dst_row, sem):
    return pltpu.make_async_copy(src_hbm.at[pl.ds(pl.multiple_of(src_row * SUBLANES, SUBLANES), SUBLANES), :],
                                 dst_ref.at[pl.ds(pl.multiple_of(dst_row * SUBLANES, SUBLANES), SUBLANES), :], sem)


def _gather_kernel(idx_ref, src_hbm, o_ref, sem, *, R):
    def start(j, carry):
        r = 2 * j
        _row_copy(src_hbm, o_ref, idx_ref[0, r], r, sem).start(priority=0)
        _row_copy(src_hbm, o_ref, idx_ref[0, r + 1], r + 1, sem).start(priority=1)
        return carry

    lax.fori_loop(0, R // 2, start, 0, unroll=4)
    pltpu.make_async_copy(src_hbm.at[pl.ds(0, R * SUBLANES), :], o_ref, sem).wait()


def gather_rows(src, idx, *, R):
    Mp = idx.shape[0]
    return pl.pallas_call(
        functools.partial(_gather_kernel, R=R),
        grid=(Mp // R,),
        in_specs=[pl.BlockSpec((None, 1, R), lambda i: (i, 0, 0), memory_space=pltpu.SMEM),
                  pl.BlockSpec(memory_space=pl.ANY)],
        out_specs=pl.BlockSpec((R * SUBLANES, LANES), lambda i: (i, 0)),
        out_shape=jax.ShapeDtypeStruct((Mp * SUBLANES, LANES), src.dtype),
        scratch_shapes=[pltpu.SemaphoreType.DMA(())],
        compiler_params=_params("arbitrary"),
        name="gather_rows",
    )(idx.reshape(Mp // R, 1, R), src)


def _combine_kernel(pos_ref, h_ref, y_hbm, g_ref, o_ref, a_sc, b_sc, sem, *, R, final_norm):
    def start(r, carry):
        _row_copy(y_hbm, a_sc, pos_ref[0, 2 * r], r, sem.at[0]).start(priority=0)
        _row_copy(y_hbm, b_sc, pos_ref[0, 2 * r + 1], r, sem.at[1]).start(priority=1)
        return carry

    lax.fori_loop(0, R, start, 0, unroll=4)
    pltpu.make_async_copy(y_hbm.at[pl.ds(0, R * SUBLANES), :], a_sc, sem.at[0]).wait()
    pltpu.make_async_copy(y_hbm.at[pl.ds(0, R * SUBLANES), :], b_sc, sem.at[1]).wait()
    n = h_ref.shape[1]
    moe = jnp.concatenate([a + b for a, b in zip(_from_token_major(a_sc, n), _from_token_major(b_sc, n))], axis=1)
    out = h_ref[...] + moe
    if final_norm:
        out = _rms(out, g_ref[...], 1e-6)
    o_ref[...] = out


def combine(h, y, pos, g, *, R, final_norm):
    M, D = h.shape
    return pl.pallas_call(
        functools.partial(_combine_kernel, R=R, final_norm=final_norm),
        grid=(M // R,),
        in_specs=[pl.BlockSpec((None, 1, 2 * R), lambda i: (i, 0, 0), memory_space=pltpu.SMEM),
                  pl.BlockSpec((R, D), lambda i: (i, 0)),
                  pl.BlockSpec(memory_space=pl.ANY),
                  pl.BlockSpec((1, D), lambda i: (0, 0))],
        out_specs=pl.BlockSpec((R, D), lambda i: (i, 0)),
        out_shape=jax.ShapeDtypeStruct((M, D), F32),
        scratch_shapes=[pltpu.VMEM((R * SUBLANES, LANES), F32), pltpu.VMEM((R * SUBLANES, LANES), F32),
                        pltpu.SemaphoreType.DMA((2,))],
        compiler_params=_params("arbitrary"),
        name="combine",
    )(pos.reshape(M // R, 1, 2 * R), h, y, g.reshape(1, D))


def _rms_only_kernel(x_ref, g_ref, o_ref):
    o_ref[...] = _rms(x_ref[...], g_ref[...], 1e-6)


def rms_only(x, g, *, tm):
    M, D = x.shape
    return pl.pallas_call(
        _rms_only_kernel,
        grid=(M // tm,),
        in_specs=[pl.BlockSpec((tm, D), lambda i: (i, 0)), pl.BlockSpec((1, D), lambda i: (0, 0))],
        out_specs=pl.BlockSpec((tm, D), lambda i: (i, 0)),
        out_shape=jax.ShapeDtypeStruct((M, D), F32),
        compiler_params=_params("parallel"),
        name="rms_only",
    )(x, g.reshape(1, D))


def moe_ffn(h, h_tok, g_norm, w_router_pad, w_gu, w_d, g_final, *, tm, r_gather, r_combine, final_norm):
    M, D = h.shape
    idx_p, gate_p = router(h, g_norm, w_router_pad, tm=min(512, M))
    expert = idx_p[:, :TOP_K].reshape(-1)
    gate = gate_p[:, :TOP_K].reshape(-1)
    n_pairs = TOP_K * M
    n_tiles = n_pairs // tm + N_EXPERTS
    Mp = n_tiles * tm
    onehot = (expert[:, None] == jnp.arange(N_EXPERTS, dtype=I32)[None, :]).astype(I32)
    csum = jnp.cumsum(onehot, axis=0)
    counts = csum[-1]
    tiles_per = (counts + tm - 1) // tm
    tile_end = jnp.cumsum(tiles_per)
    tile_start = tile_end - tiles_per
    group_start = jnp.cumsum(counts) - counts
    rank = jnp.sum(onehot * (csum - 1), axis=1)
    slot_of_pair = jnp.sum(onehot * tile_start[None, :], axis=1) * tm + rank
    _, sorted_pair, sorted_gate = lax.sort((expert, jnp.arange(n_pairs, dtype=I32), gate), num_keys=1,
                                           is_stable=True)
    tile_ids = jnp.arange(n_tiles, dtype=I32)
    tile_valid = (tile_ids < tile_end[-1]).astype(I32)
    tile_expert = jnp.minimum(jnp.sum((tile_ids[:, None] >= tile_end[None, :]).astype(I32), axis=1),
                              N_EXPERTS - 1)
    within = (tile_ids - tile_start[tile_expert]) * tm
    pos = within[:, None] + jnp.arange(tm, dtype=I32)[None, :]
    live = (tile_valid[:, None] != 0) & (pos < counts[tile_expert][:, None])
    src = jnp.clip(group_start[tile_expert][:, None] + pos, 0, n_pairs - 1).reshape(Mp)
    live = live.reshape(Mp)
    packed = jnp.stack([sorted_pair, lax.bitcast_convert_type(sorted_gate, I32)], axis=1)[src]
    token_of_slot = jnp.where(live, packed[:, 0] // TOP_K, jnp.arange(Mp, dtype=I32) % M)
    scale_of_slot = jnp.where(live, lax.bitcast_convert_type(packed[:, 1], F32), 0.0)
    last_expert = tile_expert[jnp.maximum(tile_end[-1] - 1, 0)]
    tile_expert = jnp.where(tile_valid != 0, tile_expert, last_expert)

    x_sorted = gather_rows(h_tok, token_of_slot, R=r_gather)
    y_sorted = ffn(x_sorted, g_norm, w_gu, w_d, tile_expert, tile_valid, scale_of_slot.reshape(Mp, 1),
                   tm=tm, tf=w_d.shape[1] // 2, dense=False)
    return combine(h, y_sorted, slot_of_pair, g_final, R=r_combine, final_norm=final_norm)


def kernel(x_prompt, x_sample, mem_prompt, cache_attn_k, cache_attn_v, cache_mem_k, cache_mem_v, state_gla,
           page_table, rel_bias, norm_mix, w_in, w_gla_a2, b_gla_a, gla_norm, diff_subln, lambda_q1, lambda_k1,
           lambda_q2, lambda_k2, w_out, norm_mem, norm_memkv, w_mq, w_mkv, w_mo, norm_ffn, w_ffn_gu,
           w_ffn_down, w_router, w_exp_gu, w_exp_down, norm_final):
    B, L, D = x_prompt.shape
    DB, LS, _ = x_sample.shape
    depth = w_in.shape[0]
    n_mem = mem_prompt.shape[1]
    n_pages, page = page_table.shape[1], cache_attn_k.shape[2]
    past_len = n_pages * page
    M, MS = B * L, DB * LS
    TQ_ATT, TK_ATT = 1024, 512
    GLA_C = 64

    bias_p = bias_tiles(rel_bias, R=TQ_ATT, C=TK_ATT,
                        offsets=[TK_ATT - kind * TK_ATT for kind in range(TQ_ATT // TK_ATT + 1)])
    bias_s = bias_tiles(rel_bias, R=2 * LS, C=past_len + page, offsets=(past_len,),
                        rows_per_head=LS).reshape(DIFF_H, 2 * LS, past_len + page)
    cache_k = cache_attn_k.reshape(depth, -1, page * DIFF_H, DIFF_DV)
    cache_v = cache_attn_v.reshape(depth, -1, page * DIFF_H, DIFF_DV)
    kv_heads = [(DIFF_W, DIFF_H, DIFF_DV), (2 * DIFF_W, DIFF_H, DIFF_DV)]
    mem_heads = [(0, MEM_H, MEM_DH), (MEM_H * MEM_DH, MEM_H, MEM_DH)]
    w_router_pad = jnp.pad(w_router, ((0, 0), (0, 0), (0, LANES - N_EXPERTS)))

    hp = x_prompt.reshape(M, D)
    hs = x_sample.reshape(MS, D)
    pmk, pmv = [], []
    ps = ss = None
    kv_p = kv_s = None
    for l in range(depth):
        lambda_init = 0.8 - 0.6 * math.exp(-0.3 * l)
        w_in_p = jnp.pad(w_in[l], ((0, 0), (0, N_IN_PAD - N_IN))).astype(BF16)
        w_a2p = jnp.pad(w_gla_a2[l], ((0, LANES - GLA_LR), (0, 0)))
        lamp = jnp.stack([lambda_q1[l], lambda_k1[l], lambda_q2[l], lambda_k2[l]])
        w_out_a = w_out[l, :DIFF_W].astype(BF16)
        w_out_g = w_out[l, DIFF_W:].astype(BF16)
        w_mq_b, w_mo_b, w_mkv_b = w_mq[l].astype(BF16), w_mo[l].astype(BF16), w_mkv[l].astype(BF16)
        last = l == depth - 1

        proj_p, *kv_p = rms_matmul_heads(hp, norm_mix[l], w_in_p, tm=512, keep_full=True, head_outs=kv_heads,
                                         layer=l, depth=depth, stacks=kv_p)
        proj_s, *kv_s = rms_matmul_heads(hs, norm_mix[l], w_in_p, tm=512, keep_full=True, head_outs=kv_heads,
                                         layer=l, depth=depth, stacks=kv_s)
        proj_p = proj_p.reshape(B, L, N_IN_PAD)
        proj_s = proj_s.reshape(DB, LS, N_IN_PAD)
        att_p = diff_attention_prompt(proj_p, bias_p, lamp, diff_subln[l], TQ=TQ_ATT, TK=TK_ATT,
                                      lambda_init=lambda_init)
        att_s = diff_attention_decode(proj_s, cache_k, cache_v, page_table, bias_s, lamp, diff_subln[l],
                                      layer=l, lambda_init=lambda_init)
        zero_state = jnp.zeros((B, GLA_H, GLA_DK, GLA_DV), F32)
        gla_p, ps = gla(proj_p, w_a2p, b_gla_a[l], gla_norm[l], zero_state, C=GLA_C, SB=16, G=B,
                        layer=l, depth=depth, stack=ps)
        gla_s, ss = gla(proj_s, w_a2p, b_gla_a[l], gla_norm[l], state_gla[l], C=LS, SB=LS, G=8,
                        layer=l, depth=depth, stack=ss)
        hp = mm_res([att_p.reshape(M, DIFF_W), gla_p.reshape(M, GLA_W)], [w_out_a, w_out_g], hp, tm=512)
        hs = mm_res([att_s.reshape(MS, DIFF_W), gla_s.reshape(MS, GLA_W)], [w_out_a, w_out_g], hs, tm=512)

        mkv_p, mk_p, mv_p = rms_matmul_heads(mem_prompt.reshape(B * n_mem, D), norm_memkv[l], w_mkv_b, tm=512,
                                             keep_full=True, head_outs=mem_heads)
        mkv_p = mkv_p.reshape(B, n_mem, 2 * MEM_H * MEM_DH)
        mk_p = mk_p.reshape(B, n_mem, MEM_H, MEM_DH)
        mv_p = mv_p.reshape(B, n_mem, MEM_H, MEM_DH)
        q_p = rms_matmul(hp, norm_mem[l], w_mq_b, tm=512, out_dtype=BF16, scale=MEM_DH ** -0.5)
        q_s = rms_matmul(hs, norm_mem[l], w_mq_b, tm=512, out_dtype=BF16, scale=MEM_DH ** -0.5)
        xo_p = xattn_core(
            q_p.reshape(B, L, D), mkv_p, mkv_p,
            lambda h: pl.BlockSpec((None, n_mem, MEM_DH), lambda b, i: (b, 0, h)),
            lambda h: pl.BlockSpec((None, n_mem, MEM_DH), lambda b, i: (b, 0, MEM_H + h)), tm=512)
        xo_s = xattn_cached(q_s.reshape(DB, LS, D), cache_mem_k, cache_mem_v, layer=l, G=2)
        moe_layer = l % 2 == 1
        hp = mm_res([xo_p.reshape(M, D)], [w_mo_b], hp, tm=512, token_major=moe_layer)
        hs = mm_res([xo_s.reshape(MS, D)], [w_mo_b], hs, tm=512, token_major=moe_layer)
        if moe_layer:
            (hp, hp_tok), (hs, hs_tok) = hp, hs

        if l % 2 == 0:
            w_gu = w_ffn_gu[l // 2].astype(BF16)[None]
            w_d = w_ffn_down[l // 2].astype(BF16)[None]
            for_dense = lambda h, tm: ffn(
                h, norm_ffn[l], w_gu, w_d, jnp.zeros((h.shape[0] // tm,), I32),
                jnp.ones((h.shape[0] // tm,), I32), jnp.ones((h.shape[0], 1), F32),
                tm=tm, tf=w_d.shape[1] // 2, dense=True)
            hp, hs = for_dense(hp, 512), for_dense(hs, 512)
            if last:
                hp, hs = rms_only(hp, norm_final, tm=512), rms_only(hs, norm_final, tm=512)
        else:
            w_gu = w_exp_gu[l // 2].astype(BF16)
            w_d = w_exp_down[l // 2].astype(BF16)
            hp = moe_ffn(hp, hp_tok, norm_ffn[l], w_router_pad[l // 2], w_gu, w_d, norm_final,
                         tm=512, r_gather=1024, r_combine=512, final_norm=last)
            hs = moe_ffn(hs, hs_tok, norm_ffn[l], w_router_pad[l // 2], w_gu, w_d, norm_final,
                         tm=256, r_gather=1024, r_combine=512, final_norm=last)

        pmk.append(mk_p)
        pmv.append(mv_p)

    pk, pv = (a.reshape(depth, B, L, DIFF_H, DIFF_DV) for a in kv_p)
    sk, sv = (a.reshape(depth, DB, LS, DIFF_H, DIFF_DV) for a in kv_s)
    return (hp.reshape(B, L, D), hs.reshape(DB, LS, D), pk, pv, ps,
            jnp.stack(pmk), jnp.stack(pmv), sk, sv, ss)
```

```python
import functools
import math

import numpy as np
import jax
import jax.numpy as jnp
from jax import lax
from jax.experimental import pallas as pl
from jax.experimental.pallas import tpu as pltpu

F32 = jnp.float32
BF16 = jnp.bfloat16
I32 = jnp.int32

LANES = 128
SUBLANES = 8
VMEM_LIMIT_BYTES = 56 * 1024 * 1024

D_MODEL = 1024
DIFF_H = 4
DIFF_DV = 128
DIFF_DH = 64
DIFF_W = DIFF_H * DIFF_DV
GLA_H = 4
GLA_DK = 64
GLA_DV = 128
GLA_W = GLA_H * GLA_DV
GLA_LR = 16
GATE_NORM = 16.0
N_BUCKETS = 32
MAX_DISTANCE = 128
MEM_H = 4
MEM_DH = 256
N_EXPERTS = 8
TOP_K = 2
N_IN = 3 * DIFF_W + 2 * GLA_H * GLA_DK + 2 * GLA_W + GLA_LR
N_IN_PAD = 3200
NEG_BIG = -1e30
LOG2E = math.log2(math.e)

COL_Q = 0
COL_K = 4
COL_V = 8
COL_QK_GLA = 3
COL_V_GLA = 4
COL_G_GLA = 5
COL_ALR = 24

NT_DIMS = (((1,), (1,)), ((), ()))
TN_DIMS = (((0,), (0,)), ((), ()))


def _params(*sem):
    return pltpu.CompilerParams(dimension_semantics=sem, vmem_limit_bytes=VMEM_LIMIT_BYTES)


def _rms(x, g, eps):
    ms = jnp.mean(x * x, axis=-1, keepdims=True)
    return (x * lax.rsqrt(ms + eps)) * g


def _silu(x):
    return x / (1.0 + jnp.exp(-x))


def _rms_matmul_kernel(x_ref, g_ref, w_ref, o_ref, *, eps, scale):
    xn = _rms(x_ref[...], g_ref[...], eps).astype(BF16)
    y = jnp.dot(xn, w_ref[...], preferred_element_type=F32)
    if scale != 1.0:
        y = y * scale
    o_ref[...] = y.astype(o_ref.dtype)


def rms_matmul(x, g, w, *, tm, out_dtype=F32, scale=1.0, eps=1e-6):
    M, K = x.shape
    N = w.shape[1]
    return pl.pallas_call(
        functools.partial(_rms_matmul_kernel, eps=eps, scale=scale),
        grid=(M // tm,),
        in_specs=[pl.BlockSpec((tm, K), lambda i: (i, 0)),
                  pl.BlockSpec((1, K), lambda i: (0, 0)),
                  pl.BlockSpec((K, N), lambda i: (0, 0))],
        out_specs=pl.BlockSpec((tm, N), lambda i: (i, 0)),
        out_shape=jax.ShapeDtypeStruct((M, N), out_dtype),
        compiler_params=_params("parallel"),
        name="rms_matmul",
    )(x, g.reshape(1, K), w)


def _rms_matmul_heads_kernel(x_ref, g_ref, w_ref, *refs, eps, keep_full, head_outs, n_alias, layer, depth):
    o_refs = refs[n_alias:]
    xn = _rms(x_ref[...], g_ref[...], eps).astype(BF16)
    y = jnp.dot(xn, w_ref[...], preferred_element_type=F32)
    if keep_full:
        o_refs[0][...] = y
    for o_ref, (col0, n_heads, width) in zip(o_refs[1 if keep_full else 0:], head_outs):
        if n_alias == 0:
            for other in range(depth):
                if other != layer:
                    o_ref[other] = jnp.zeros(o_ref.shape[1:], F32)
            o_ref = o_ref.at[layer]
        for h in range(n_heads):
            o_ref[:, h, :] = y[:, col0 + h * width:col0 + (h + 1) * width]


def rms_matmul_heads(x, g, w, *, tm, keep_full, head_outs, layer=0, depth=1, stacks=None, eps=1e-6):
    M, K = x.shape
    N = w.shape[1]
    stacks = [] if stacks is None else list(stacks)
    if stacks:
        out_specs = [pl.BlockSpec((None, tm, nh, wd), lambda i: (layer, i, 0, 0)) for _, nh, wd in head_outs]
    else:
        out_specs = [pl.BlockSpec((depth, tm, nh, wd), lambda i: (0, i, 0, 0)) for _, nh, wd in head_outs]
    out_shape = [jax.ShapeDtypeStruct((depth, M, nh, wd), F32) for _, nh, wd in head_outs]
    if keep_full:
        out_specs.insert(0, pl.BlockSpec((tm, N), lambda i: (i, 0)))
        out_shape.insert(0, jax.ShapeDtypeStruct((M, N), F32))
    first_stack_out = 1 if keep_full else 0
    return pl.pallas_call(
        functools.partial(_rms_matmul_heads_kernel, eps=eps, keep_full=keep_full, head_outs=tuple(head_outs),
                          n_alias=len(stacks), layer=layer, depth=depth),
        grid=(M // tm,),
        in_specs=[pl.BlockSpec((tm, K), lambda i: (i, 0)),
                  pl.BlockSpec((1, K), lambda i: (0, 0)),
                  pl.BlockSpec((K, N), lambda i: (0, 0))] + [pl.BlockSpec(memory_space=pl.ANY)] * len(stacks),
        out_specs=out_specs,
        out_shape=out_shape,
        input_output_aliases={3 + j: first_stack_out + j for j in range(len(stacks))},
        compiler_params=_params("parallel"),
        name="rms_matmul_heads",
    )(x, g.reshape(1, K), w, *stacks)


def _to_token_major(o_ref, y):
    rows, n = y.shape
    for c in range(n // LANES):
        o_ref[pl.ds(c, rows, stride=n // LANES), :] = y[:, c * LANES:(c + 1) * LANES]


def _from_token_major(x_ref, n):
    rows = x_ref.shape[0] // (n // LANES)
    return [x_ref[pl.ds(c, rows, stride=n // LANES), :] for c in range(n // LANES)]


def _mm_res_kernel(*refs, n_lhs, token_major):
    a_refs, w_refs = refs[:n_lhs], refs[n_lhs:2 * n_lhs]
    res_ref, o_ref = refs[2 * n_lhs], refs[2 * n_lhs + 1]
    acc = res_ref[...]
    for a_ref, w_ref in zip(a_refs, w_refs):
        acc = acc + jnp.dot(a_ref[...], w_ref[...], preferred_element_type=F32)
    o_ref[...] = acc
    if token_major:
        _to_token_major(refs[2 * n_lhs + 2], acc)


def mm_res(lhs, ws, res, *, tm, token_major=False):
    M, N = res.shape
    n = len(lhs)
    in_specs = ([pl.BlockSpec((tm, a.shape[1]), lambda i: (i, 0)) for a in lhs]
                + [pl.BlockSpec(w.shape, lambda i: (0, 0)) for w in ws]
                + [pl.BlockSpec((tm, N), lambda i: (i, 0))])
    out_specs = [pl.BlockSpec((tm, N), lambda i: (i, 0))]
    out_shape = [jax.ShapeDtypeStruct((M, N), F32)]
    if token_major:
        out_specs.append(pl.BlockSpec((tm * N // LANES, LANES), lambda i: (i, 0)))
        out_shape.append(jax.ShapeDtypeStruct((M * N // LANES, LANES), F32))
    out = pl.pallas_call(
        functools.partial(_mm_res_kernel, n_lhs=n, token_major=token_major),
        grid=(M // tm,),
        in_specs=in_specs,
        out_specs=out_specs,
        out_shape=out_shape,
        compiler_params=_params("parallel"),
        name="mm_res",
    )(*lhs, *ws, res)
    return out if token_major else out[0]


def _bias_kernel(tab_ref, o_ref, *, offsets, rows_per_head):
    h = pl.program_id(0)
    R, C = o_ref.shape[-2], o_ref.shape[-1]
    max_exact = N_BUCKETS // 2
    far = tab_ref[N_BUCKETS - 1, h]
    rb = min(R, rows_per_head, LANES)
    for kind, off in enumerate(offsets):
        for r0 in range(0, R, rb):
            for c0 in range(0, C, LANES):
                lo = off + r0 % rows_per_head - (c0 + LANES - 1)
                hi = off + r0 % rows_per_head + rb - 1 - c0
                blk = (kind, slice(r0, r0 + rb), slice(c0, c0 + LANES))
                if lo >= MAX_DISTANCE:
                    o_ref[blk] = jnp.zeros((rb, LANES), F32)
                    continue
                if hi < 0:
                    o_ref[blk] = jnp.full((rb, LANES), NEG_BIG, F32)
                    continue
                r = lax.broadcasted_iota(I32, (rb, LANES), 0)
                c = lax.broadcasted_iota(I32, (rb, LANES), 1)
                rel = (off + r0 % rows_per_head - c0) + r - c
                n = jnp.maximum(rel, 0)
                nf = jnp.maximum(n, max_exact).astype(F32)
                large = max_exact + (jnp.log(nf / max_exact) / math.log(MAX_DISTANCE / max_exact)
                                     * (N_BUCKETS - max_exact)).astype(I32)
                bucket = jnp.where(n < max_exact, n, jnp.minimum(large, N_BUCKETS - 1))
                acc = jnp.zeros((rb, LANES), F32)
                for b in range(N_BUCKETS - 1):
                    acc = jnp.where(bucket == b, (tab_ref[b, h] - far) * LOG2E, acc)
                o_ref[blk] = jnp.where(rel >= 0, acc, NEG_BIG)


def bias_tiles(rel_bias, *, R, C, offsets, rows_per_head=None):
    rows_per_head = R if rows_per_head is None else rows_per_head
    return pl.pallas_call(
        functools.partial(_bias_kernel, offsets=tuple(offsets), rows_per_head=rows_per_head),
        grid=(DIFF_H,),
        in_specs=[pl.BlockSpec(memory_space=pltpu.SMEM)],
        out_specs=pl.BlockSpec((None, len(offsets), R, C), lambda h: (h, 0, 0, 0)),
        out_shape=jax.ShapeDtypeStruct((DIFF_H, len(offsets), R, C), F32),
        compiler_params=_params("arbitrary"),
        name="bias_tiles",
    )(rel_bias)


def _lambda_value(lamp, lambda_init):
    s1 = jnp.sum(lamp[0:1, :] * lamp[1:2, :], axis=-1, keepdims=True)
    s2 = jnp.sum(lamp[2:3, :] * lamp[3:4, :], axis=-1, keepdims=True)
    return jnp.exp(s1) - jnp.exp(s2) + lambda_init


def _diff_finish(o1, o2, lam, subln, lambda_init):
    o = o1 - lam * o2
    return _rms(o, subln, 1e-5) * (1.0 - lambda_init)


def _diff_attn_kernel(qi_tab, ki_tab, q_ref, k_ref, v_ref, bias_ref, lamp_ref, subln_ref, o_ref,
                      q2_sc, m_sc, acc_sc, *, TQ, TK, RB, lambda_init):
    t = pl.program_id(2)
    qi = qi_tab[t]
    ki = ki_tab[t]
    n = TQ // TK
    kind_now = ki - qi * n + 1

    @pl.when(ki == 0)
    def _init():
        q = q_ref[...] * (DIFF_DH ** -0.5 * LOG2E)
        lane = lax.broadcasted_iota(I32, q.shape, 1)
        q2_sc[0:TQ, :] = jnp.where(lane < DIFF_DH, q, 0.0).astype(BF16)
        q2_sc[TQ:2 * TQ, :] = jnp.where(lane >= DIFF_DH, q, 0.0).astype(BF16)
        m_sc[...] = jnp.full(m_sc.shape, NEG_BIG, F32)
        acc_sc[...] = jnp.zeros(acc_sc.shape, F32)

    def update(kind):
        kb = k_ref[...].astype(BF16)
        vb = jnp.concatenate([v_ref[...].astype(BF16), jnp.ones((TK, DIFF_DV), BF16)], axis=1)
        for r0 in range(0, 2 * TQ, RB):
            rows = slice(r0, r0 + RB)
            q_row = r0 % TQ
            n_k = TK if kind is None else max(0, min(TK, TK - kind * TK + q_row + RB))
            if n_k == 0:
                continue
            s = lax.dot_general(q2_sc[rows, :], kb[0:n_k], NT_DIMS, preferred_element_type=F32)
            if kind is not None:
                s = s + bias_ref[kind, q_row:q_row + RB, 0:n_k]
            cols = [s[:, c * LANES:(c + 1) * LANES] for c in range(n_k // LANES)]
            m_old = m_sc[rows, :]
            m_new = jnp.maximum(m_old, jnp.max(functools.reduce(jnp.maximum, cols), axis=-1, keepdims=True))
            alpha = jnp.exp2(m_old - m_new)
            p = jnp.concatenate([jnp.exp2(c - m_new) for c in cols], axis=1).astype(BF16)
            acc_sc[rows, :] = (jnp.concatenate([alpha, alpha], axis=1) * acc_sc[rows, :]
                               + jnp.dot(p, vb[0:n_k], preferred_element_type=F32))
            m_sc[rows, :] = m_new

    @pl.when(kind_now < 0)
    def _far():
        update(None)

    for kind in range(n + 1):
        @pl.when(kind_now == kind)
        def _near(kind=kind):
            update(kind)
            if kind == n:
                on = acc_sc[:, 0:DIFF_DV] / acc_sc[:, DIFF_DV:2 * DIFF_DV]
                lam = _lambda_value(lamp_ref[...], lambda_init)
                o_ref[...] = _diff_finish(on[0:TQ], on[TQ:2 * TQ], lam, subln_ref[...],
                                          lambda_init).astype(o_ref.dtype)


def diff_attention_prompt(proj, bias, lamp, subln, *, TQ, TK, lambda_init):
    B, L, _ = proj.shape
    n = TQ // TK
    pairs = [(qi, ki) for qi in range(L // TQ) for ki in range((qi + 1) * n)]
    qi_tab = jnp.asarray(np.array([p[0] for p in pairs], np.int32))
    ki_tab = jnp.asarray(np.array([p[1] for p in pairs], np.int32))
    grid_spec = pltpu.PrefetchScalarGridSpec(
        num_scalar_prefetch=2,
        grid=(B, DIFF_H, len(pairs)),
        in_specs=[
            pl.BlockSpec((None, TQ, DIFF_DV), lambda b, h, t, qt, kt: (b, qt[t], COL_Q + h)),
            pl.BlockSpec((None, TK, DIFF_DV), lambda b, h, t, qt, kt: (b, kt[t], COL_K + h)),
            pl.BlockSpec((None, TK, DIFF_DV), lambda b, h, t, qt, kt: (b, kt[t], COL_V + h)),
            pl.BlockSpec((None, n + 1, TQ, TK), lambda b, h, t, qt, kt: (h, 0, 0, 0)),
            pl.BlockSpec((4, DIFF_DH), lambda b, h, t, qt, kt: (0, 0)),
            pl.BlockSpec((1, DIFF_DV), lambda b, h, t, qt, kt: (0, 0)),
        ],
        out_specs=pl.BlockSpec((None, TQ, DIFF_DV), lambda b, h, t, qt, kt: (b, qt[t], h)),
        scratch_shapes=[pltpu.VMEM((2 * TQ, DIFF_DV), BF16),
                        pltpu.VMEM((2 * TQ, LANES), F32),
                        pltpu.VMEM((2 * TQ, 2 * DIFF_DV), F32)],
    )
    return pl.pallas_call(
        functools.partial(_diff_attn_kernel, TQ=TQ, TK=TK, RB=LANES, lambda_init=lambda_init),
        grid_spec=grid_spec,
        out_shape=jax.ShapeDtypeStruct((B, L, DIFF_W), BF16),
        compiler_params=_params("parallel", "parallel", "arbitrary"),
        name="diff_attn_prompt",
    )(qi_tab, ki_tab, proj, proj, proj, bias, lamp, subln.reshape(1, DIFF_DV))


def _diff_decode_kernel(pt_ref, proj_ref, bias_ref, lamp_ref, subln_ref, ck_hbm, cv_hbm, o_ref, k_buf, v_buf, sem,
                        *, layer, n_seq, n_pages, page, lq, lambda_init):
    b = pl.program_id(0)
    slot = b % 2

    def start_pages(seq, buf_slot):
        for j in range(n_pages):
            pg = pt_ref[seq, j]
            pltpu.make_async_copy(ck_hbm.at[layer, pg], k_buf.at[buf_slot, j], sem.at[0, buf_slot]).start()
            pltpu.make_async_copy(cv_hbm.at[layer, pg], v_buf.at[buf_slot, j], sem.at[1, buf_slot]).start()

    @pl.when(b == 0)
    def _first():
        start_pages(0, 0)

    @pl.when(b + 1 < n_seq)
    def _prefetch():
        start_pages(b + 1, 1 - slot)

    pltpu.make_async_copy(ck_hbm.at[layer, pl.ds(0, n_pages)], k_buf.at[slot], sem.at[0, slot]).wait()
    pltpu.make_async_copy(cv_hbm.at[layer, pl.ds(0, n_pages)], v_buf.at[slot], sem.at[1, slot]).wait()
    k_refs = [k_buf.at[slot, j] for j in range(n_pages)]
    v_refs = [v_buf.at[slot, j] for j in range(n_pages)]
    lam = _lambda_value(lamp_ref[...], lambda_init)
    lane = lax.broadcasted_iota(I32, (lq, DIFF_DV), 1)
    pad = jnp.zeros((page - lq, DIFF_DV), F32)
    head_rows = [pl.ds(h, page, stride=DIFF_H) for h in range(DIFF_H)]
    head_cols = [slice(h * DIFF_DV, (h + 1) * DIFF_DV) for h in range(DIFF_H)]
    scores = []
    for h in range(DIFF_H):
        q = proj_ref[:, head_cols[h]] * (DIFF_DH ** -0.5 * LOG2E)
        qbd = jnp.concatenate([jnp.where(lane < DIFF_DH, q, 0.0), jnp.where(lane >= DIFF_DH, q, 0.0)],
                              axis=0).astype(BF16)
        k_tail = jnp.concatenate([proj_ref[:, DIFF_W + h * DIFF_DV:DIFF_W + (h + 1) * DIFF_DV], pad], 0)
        parts = [lax.dot_general(qbd, k_refs[j][head_rows[h], :].astype(BF16), NT_DIMS,
                                 preferred_element_type=F32) for j in range(n_pages)]
        parts.append(lax.dot_general(qbd, k_tail.astype(BF16), NT_DIMS, preferred_element_type=F32))
        scores.append(jnp.concatenate(parts, axis=1) + bias_ref[h])
    s = jnp.concatenate(scores, axis=0)
    p = jnp.exp2(s - jnp.max(s, axis=-1, keepdims=True))
    l = jnp.sum(p, axis=-1, keepdims=True)
    pb = p.astype(BF16)
    for h in range(DIFF_H):
        hr = slice(2 * lq * h, 2 * lq * (h + 1))
        v_tail = jnp.concatenate([proj_ref[:, 2 * DIFF_W + h * DIFF_DV:2 * DIFF_W + (h + 1) * DIFF_DV], pad], 0)
        acc = jnp.dot(pb[hr, n_pages * page:], v_tail.astype(BF16), preferred_element_type=F32)
        for j in range(n_pages):
            acc = acc + jnp.dot(pb[hr, j * page:(j + 1) * page], v_refs[j][head_rows[h], :].astype(BF16),
                                preferred_element_type=F32)
        on = acc / l[hr]
        o = _diff_finish(on[0:lq], on[lq:2 * lq], lam, subln_ref[...], lambda_init)
        o_ref[:, head_cols[h]] = o.astype(o_ref.dtype)


def diff_attention_decode(proj, cache_k, cache_v, page_table, bias, lamp, subln, *, layer, lambda_init):
    DB, lq, _ = proj.shape
    n_pages = page_table.shape[1]
    page = cache_k.shape[2] // DIFF_H

    page_buf = pltpu.VMEM((2, n_pages, page * DIFF_H, DIFF_DV), F32)
    grid_spec = pltpu.PrefetchScalarGridSpec(
        num_scalar_prefetch=1,
        grid=(DB,),
        in_specs=[pl.BlockSpec((None, lq, 3 * DIFF_W), lambda b, pt: (b, 0, 0)),
                  pl.BlockSpec(bias.shape, lambda b, pt: (0, 0, 0)),
                  pl.BlockSpec((4, DIFF_DH), lambda b, pt: (0, 0)),
                  pl.BlockSpec((1, DIFF_DV), lambda b, pt: (0, 0)),
                  pl.BlockSpec(memory_space=pl.ANY), pl.BlockSpec(memory_space=pl.ANY)],
        out_specs=pl.BlockSpec((None, lq, DIFF_W), lambda b, pt: (b, 0, 0)),
        scratch_shapes=[page_buf, page_buf, pltpu.SemaphoreType.DMA((2, 2))],
    )
    return pl.pallas_call(
        functools.partial(_diff_decode_kernel, layer=layer, n_seq=DB, n_pages=n_pages, page=page, lq=lq,
                          lambda_init=lambda_init),
        grid_spec=grid_spec,
        out_shape=jax.ShapeDtypeStruct((DB, lq, DIFF_W), BF16),
        compiler_params=_params("arbitrary"),
        name="diff_attn_decode",
    )(page_table, proj, bias, lamp, subln.reshape(1, DIFF_DV), cache_k, cache_v)


def _gla_kernel(qk_ref, v_ref, gg_ref, alr_ref, wa_ref, ba_ref, gn_ref, ex_ref, s0_ref, *refs,
                C, SB, n_chunks, G, layer, depth, n_alias):
    o_ref, sout_ref, st_sc = refs[n_alias:]
    c = pl.program_id(1)

    @pl.when(c == 0)
    def _init():
        for g in range(G):
            for h in range(GLA_H):
                st_sc[g, h] = s0_ref[g, h].T

    for g in range(G):
        _gla_chunk(qk_ref.at[g], v_ref.at[g], gg_ref.at[g], alr_ref.at[g], wa_ref, ba_ref, gn_ref, ex_ref,
                   o_ref.at[g], st_sc.at[g], C=C, SB=SB)

    @pl.when(c == n_chunks - 1)
    def _fin():
        out = sout_ref
        if n_alias == 0:
            for other in range(depth):
                if other != layer:
                    sout_ref[other] = jnp.zeros(sout_ref.shape[1:], F32)
            out = sout_ref.at[layer]
        for g in range(G):
            for h in range(GLA_H):
                out[g, h] = st_sc[g, h].T


def _gla_chunk(qk_ref, v_ref, gg_ref, alr_ref, wa_ref, ba_ref, gn_ref, ex_ref, o_ref, st_sc, *, C, SB):
    z = jnp.dot(alr_ref[...], wa_ref[...], precision=lax.Precision.HIGHEST,
                preferred_element_type=F32) + ba_ref[...]
    logg = (jnp.minimum(z, 0.0) - jnp.log1p(jnp.exp(-jnp.abs(z)))) * (1.0 / GATE_NORM)
    row = lax.broadcasted_iota(I32, logg.shape, 0)
    b = logg
    d = 1
    while d < C:
        b = b + jnp.where(row >= d, pltpu.roll(b, d, 0), 0.0)
        d *= 2
    q_all = qk_ref[:, 0:GLA_H * GLA_DK] * (GLA_DK ** -0.5)
    k_all = qk_ref[:, GLA_H * GLA_DK:2 * GLA_H * GLA_DK]
    v_all = v_ref[...]
    n_sub = C // SB
    hk = lambda a, h: a[:, h * GLA_DK:(h + 1) * GLA_DK]
    hv = lambda a, h: a[:, h * GLA_DV:(h + 1) * GLA_DV]

    b_last = b[C - 1:C, :]
    q_in = (q_all * jnp.exp(b)).astype(BF16)
    k_dec = (k_all * jnp.exp(b_last - b)).astype(BF16)
    e_last = jnp.exp(b_last)
    q_off, k_off = [], []
    for i in range(1, n_sub):
        ref = b[i * SB:i * SB + 1, :]
        q_off.append((q_all[i * SB:(i + 1) * SB] * jnp.exp(b[i * SB:(i + 1) * SB] - ref)).astype(BF16))
        k_off.append((k_all * jnp.exp(jnp.minimum(ref - b, 0.0))).astype(BF16))
    col = lax.broadcasted_iota(I32, (SB, C), 1)

    outs = []
    for h in range(GLA_H):
        vb = hv(v_all, h).astype(BF16)
        st = st_sc[h]
        o = lax.dot_general(hk(q_in, h), st.astype(BF16), NT_DIMS, preferred_element_type=F32)
        if n_sub > 1:
            rows = [jnp.zeros((SB, C), F32)]
            for i in range(1, n_sub):
                a = lax.dot_general(hk(q_off[i - 1], h), hk(k_off[i - 1], h), NT_DIMS,
                                    preferred_element_type=F32)
                rows.append(jnp.where(col < i * SB, a, 0.0))
            a_off = jnp.concatenate(rows, axis=0)
            o = o + jnp.dot(a_off.astype(BF16), vb, preferred_element_type=F32)
        st_sc[h] = st * hk(e_last, h) + lax.dot_general(vb, hk(k_dec, h), TN_DIMS, preferred_element_type=F32)
        outs.append(o)

    rmod = row % SB
    xs = []
    for dlt in range(SB):
        kd, bd = (k_all, b) if dlt == 0 else (pltpu.roll(k_all, dlt, 0), pltpu.roll(b, dlt, 0))
        x = q_all * kd * jnp.exp(jnp.minimum(b - bd, 0.0))
        xs.append(jnp.where(rmod >= dlt, x, 0.0))
    w = jnp.dot(jnp.concatenate(xs, axis=0).astype(BF16), ex_ref[...], preferred_element_type=F32)
    o_diag = w[0:C] * v_all
    for dlt in range(1, SB):
        o_diag = o_diag + w[dlt * C:(dlt + 1) * C] * pltpu.roll(v_all, dlt, 0)

    for h in range(GLA_H):
        gate = _silu(hv(gg_ref[...], h))
        o = outs[h] + hv(o_diag, h)
        o_ref[:, h * GLA_DV:(h + 1) * GLA_DV] = (_rms(o, gn_ref[...], 1e-5) * gate).astype(o_ref.dtype)


def gla(proj, w_a2p, b_a, gla_g, s0, *, C, SB, G, layer, depth, stack=None):
    B, L, _ = proj.shape
    n_chunks = L // C
    W = GLA_W
    expand = (jnp.arange(GLA_H * GLA_DK, dtype=I32)[:, None] // GLA_DK
              == jnp.arange(W, dtype=I32)[None, :] // GLA_DV).astype(BF16)
    state = (GLA_H, GLA_DK, GLA_DV)
    if stack is None:
        stacks, state_spec = [], pl.BlockSpec((depth, G) + state, lambda b, c: (0, b, 0, 0, 0))
    else:
        stacks, state_spec = [stack], pl.BlockSpec((None, G) + state, lambda b, c: (layer, b, 0, 0, 0))
    return pl.pallas_call(
        functools.partial(_gla_kernel, C=C, SB=SB, n_chunks=n_chunks, G=G, layer=layer, depth=depth,
                          n_alias=len(stacks)),
        grid=(B // G, n_chunks),
        in_specs=[
            pl.BlockSpec((G, C, W), lambda b, c: (b, c, COL_QK_GLA)),
            pl.BlockSpec((G, C, W), lambda b, c: (b, c, COL_V_GLA)),
            pl.BlockSpec((G, C, W), lambda b, c: (b, c, COL_G_GLA)),
            pl.BlockSpec((G, C, LANES), lambda b, c: (b, c, COL_ALR)),
            pl.BlockSpec((LANES, GLA_H * GLA_DK), lambda b, c: (0, 0)),
            pl.BlockSpec((1, GLA_H * GLA_DK), lambda b, c: (0, 0)),
            pl.BlockSpec((1, GLA_DV), lambda b, c: (0, 0)),
            pl.BlockSpec((GLA_H * GLA_DK, W), lambda b, c: (0, 0)),
            pl.BlockSpec((G,) + state, lambda b, c: (b, 0, 0, 0)),
        ] + [pl.BlockSpec(memory_space=pl.ANY)] * len(stacks),
        out_specs=[pl.BlockSpec((G, C, W), lambda b, c: (b, c, 0)), state_spec],
        out_shape=[jax.ShapeDtypeStruct((B, L, W), BF16),
                   jax.ShapeDtypeStruct((depth, B) + state, F32)],
        input_output_aliases={9: 1} if stacks else {},
        scratch_shapes=[pltpu.VMEM((G, GLA_H, GLA_DV, GLA_DK), F32)],
        compiler_params=_params("parallel", "arbitrary"),
        name="gla",
    )(proj, proj, proj, proj, w_a2p, b_a.reshape(1, -1), gla_g.reshape(1, -1), expand, s0, *stacks)


def _xattn_kernel(q_ref, *refs):
    k_refs, v_refs, o_ref = refs[:MEM_H], refs[MEM_H:2 * MEM_H], refs[2 * MEM_H]
    rows = q_ref.shape[0]
    s = jnp.concatenate(
        [lax.dot_general(q_ref[:, h * MEM_DH:(h + 1) * MEM_DH], k_refs[h][...].astype(BF16), NT_DIMS,
                         preferred_element_type=F32) for h in range(MEM_H)], axis=0)
    p = jnp.exp(s - jnp.max(s, axis=-1, keepdims=True))
    l = jnp.sum(p, axis=-1, keepdims=True)
    for h in range(MEM_H):
        hr = slice(h * rows, (h + 1) * rows)
        o = jnp.dot(p[hr].astype(BF16), v_refs[h][...].astype(BF16), preferred_element_type=F32) / l[hr]
        o_ref[:, h * MEM_DH:(h + 1) * MEM_DH] = o.astype(o_ref.dtype)


def xattn_core(q, mk, mv, k_spec, v_spec, *, tm):
    Bx, Lx, W = q.shape
    return pl.pallas_call(
        _xattn_kernel,
        grid=(Bx, Lx // tm),
        in_specs=([pl.BlockSpec((None, tm, W), lambda b, i: (b, i, 0))]
                  + [k_spec(h) for h in range(MEM_H)] + [v_spec(h) for h in range(MEM_H)]),
        out_specs=pl.BlockSpec((None, tm, W), lambda b, i: (b, i, 0)),
        out_shape=jax.ShapeDtypeStruct((Bx, Lx, W), BF16),
        compiler_params=_params("parallel", "arbitrary"),
        name="xattn_core",
    )(q, *([mk] * MEM_H), *([mv] * MEM_H))


def _xattn_cached_kernel(q_ref, mk_hbm, mv_hbm, o_ref, k_buf, v_buf, sem, *, layer, n_steps, G):
    b = pl.program_id(0)
    slot = b % 2

    def copies(step, buf_slot):
        out = []
        for g in range(G):
            seq = step * G + g
            for h in range(MEM_H):
                out.append(pltpu.make_async_copy(mk_hbm.at[layer, seq, :, h, :], k_buf.at[buf_slot, g, h],
                                                 sem.at[0, buf_slot]))
                out.append(pltpu.make_async_copy(mv_hbm.at[layer, seq, :, h, :], v_buf.at[buf_slot, g, h],
                                                 sem.at[1, buf_slot]))
        return out

    @pl.when(b == 0)
    def _first():
        for c in copies(0, 0):
            c.start()

    @pl.when(b + 1 < n_steps)
    def _prefetch():
        for c in copies(b + 1, 1 - slot):
            c.start()

    for c in copies(b, slot):
        c.wait()
    for g in range(G):
        _xattn_kernel(q_ref.at[g], *[k_buf.at[slot, g, h] for h in range(MEM_H)],
                      *[v_buf.at[slot, g, h] for h in range(MEM_H)], o_ref.at[g])


def xattn_cached(q, mk, mv, *, layer, G):
    DB, LS, W = q.shape
    n_mem = mk.shape[2]
    buf = pltpu.VMEM((2, G, MEM_H, n_mem, MEM_DH), F32)
    return pl.pallas_call(
        functools.partial(_xattn_cached_kernel, layer=layer, n_steps=DB // G, G=G),
        grid=(DB // G,),
        in_specs=[pl.BlockSpec((G, LS, W), lambda b: (b, 0, 0)),
                  pl.BlockSpec(memory_space=pl.ANY), pl.BlockSpec(memory_space=pl.ANY)],
        out_specs=pl.BlockSpec((G, LS, W), lambda b: (b, 0, 0)),
        out_shape=jax.ShapeDtypeStruct((DB, LS, W), BF16),
        scratch_shapes=[buf, buf, pltpu.SemaphoreType.DMA((2, 2))],
        compiler_params=_params("arbitrary"),
        name="xattn_cached",
    )(q, mk, mv)


def _ffn_kernel(te_ref, tv_ref, x_ref, g_ref, wg_ref, wu_ref, wd_ref, sc_ref, o_ref, xn_sc, acc_sc,
                *, n_f, dense):
    i = pl.program_id(0)
    f = pl.program_id(1)

    @pl.when(f == 0)
    def _init():
        if dense:
            xn_sc[...] = _rms(x_ref[...], g_ref[...], 1e-6).astype(BF16)
        else:
            parts = _from_token_major(x_ref, xn_sc.shape[1])
            ms = sum(jnp.sum(p * p, axis=-1, keepdims=True) for p in parts) * (1.0 / xn_sc.shape[1])
            inv = lax.rsqrt(ms + 1e-6)
            for c, p in enumerate(parts):
                cols = slice(c * LANES, (c + 1) * LANES)
                xn_sc[:, cols] = ((p * inv) * g_ref[:, cols]).astype(BF16)
        acc_sc[...] = jnp.zeros(acc_sc.shape, F32)

    @pl.when(tv_ref[i] != 0)
    def _compute():
        xn = xn_sc[...]
        g = jnp.dot(xn, wg_ref[...], preferred_element_type=F32)
        u = jnp.dot(xn, wu_ref[...], preferred_element_type=F32)
        a = (_silu(g) * u).astype(BF16)
        acc_sc[...] += jnp.dot(a, wd_ref[...], preferred_element_type=F32)

    @pl.when(f == n_f - 1)
    def _fin():
        if dense:
            o_ref[...] = x_ref[...] + acc_sc[...]
        else:
            _to_token_major(o_ref, acc_sc[...] * sc_ref[...])


def ffn(x, g, w_gu, w_d, tile_expert, tile_valid, row_scale, *, tm, tf, dense):
    D = w_gu.shape[1]
    Mp = row_scale.shape[0]
    F = w_d.shape[1]
    n_f = F // tf
    last = n_f - 1

    def fblk(i, f, tv):
        return f * tv[i] + last * (1 - tv[i])

    if dense:
        x_spec = pl.BlockSpec((tm, D), lambda i, f, te, tv: (i, 0))
    else:
        x_spec = pl.BlockSpec((tm * D // LANES, LANES), lambda i, f, te, tv: (i, 0))
    grid_spec = pltpu.PrefetchScalarGridSpec(
        num_scalar_prefetch=2,
        grid=(Mp // tm, n_f),
        in_specs=[
            x_spec,
            pl.BlockSpec((1, D), lambda i, f, te, tv: (0, 0)),
            pl.BlockSpec((None, D, tf), lambda i, f, te, tv: (te[i], 0, fblk(i, f, tv))),
            pl.BlockSpec((None, D, tf), lambda i, f, te, tv: (te[i], 0, n_f + fblk(i, f, tv))),
            pl.BlockSpec((None, tf, D), lambda i, f, te, tv: (te[i], fblk(i, f, tv), 0)),
            pl.BlockSpec((tm, 1), lambda i, f, te, tv: (i, 0)),
        ],
        out_specs=x_spec,
        scratch_shapes=[pltpu.VMEM((tm, D), BF16), pltpu.VMEM((tm, D), F32)],
    )
    return pl.pallas_call(
        functools.partial(_ffn_kernel, n_f=n_f, dense=dense),
        grid_spec=grid_spec,
        out_shape=jax.ShapeDtypeStruct(x.shape, F32),
        compiler_params=_params("parallel", "arbitrary"),
        name="ffn_dense" if dense else "ffn_grouped",
    )(tile_expert, tile_valid, x, g.reshape(1, D), w_gu, w_gu, w_d, row_scale)


def _router_kernel(x_ref, g_ref, wr_ref, idx_ref, gate_ref):
    xn = _rms(x_ref[...], g_ref[...], 1e-6)
    logits = jnp.dot(xn, wr_ref[...], precision=lax.Precision.HIGHEST, preferred_element_type=F32)
    lane = lax.broadcasted_iota(I32, logits.shape, 1)
    real = lane < N_EXPERTS
    logits = jnp.where(real, logits, NEG_BIG)
    e = jnp.exp(logits - jnp.max(logits, axis=-1, keepdims=True))
    probs = jnp.where(real, e / jnp.sum(e, axis=-1, keepdims=True), -1.0)
    v1 = jnp.max(probs, axis=-1, keepdims=True)
    i1 = jnp.min(jnp.where(probs == v1, lane, LANES), axis=-1, keepdims=True)
    rest = jnp.where(lane == i1, -1.0, probs)
    v2 = jnp.max(rest, axis=-1, keepdims=True)
    i2 = jnp.min(jnp.where(rest == v2, lane, LANES), axis=-1, keepdims=True)
    den = v1 + v2
    idx_ref[...] = jnp.where(lane == 0, i1, jnp.where(lane == 1, i2, 0))
    gate_ref[...] = jnp.where(lane == 0, v1 / den, jnp.where(lane == 1, v2 / den, 0.0))


def router(x, g, w_router_pad, *, tm):
    M, D = x.shape
    return pl.pallas_call(
        _router_kernel,
        grid=(M // tm,),
        in_specs=[pl.BlockSpec((tm, D), lambda i: (i, 0)),
                  pl.BlockSpec((1, D), lambda i: (0, 0)),
                  pl.BlockSpec((D, LANES), lambda i: (0, 0))],
        out_specs=[pl.BlockSpec((tm, LANES), lambda i: (i, 0)),
                   pl.BlockSpec((tm, LANES), lambda i: (i, 0))],
        out_shape=[jax.ShapeDtypeStruct((M, LANES), I32), jax.ShapeDtypeStruct((M, LANES), F32)],
        compiler_params=_params("parallel"),
        name="router",
    )(x, g.reshape(1, D), w_router_pad)


def _row_copy(src_hbm, dst_ref, src_row, dst_row, sem):
    return pltpu.make_async_copy(src_hbm.at[pl.ds(pl.multiple_of(src_row * SUBLANES, SUBLANES), SUBLANES), :],
                                 dst_ref.at[pl.ds(pl.multiple_of(dst_row * SUBLANES, SUBLANES), SUBLANES), :], sem)


def _gather_kernel(idx_ref, src_hbm, o_ref, sem, *, R):
    def start(j, carry):
        r = 2 * j
        _row_copy(src_hbm, o_ref, idx_ref[0, r], r, sem).start(priority=0)
        _row_copy(src_hbm, o_ref, idx_ref[0, r + 1], r + 1, sem).start(priority=1)
        return carry

    lax.fori_loop(0, R // 2, start, 0, unroll=4)
    pltpu.make_async_copy(src_hbm.at[pl.ds(0, R * SUBLANES), :], o_ref, sem).wait()


def gather_rows(src, idx, *, R):
    Mp = idx.shape[0]
    return pl.pallas_call(
        functools.partial(_gather_kernel, R=R),
        grid=(Mp // R,),
        in_specs=[pl.BlockSpec((None, 1, R), lambda i: (i, 0, 0), memory_space=pltpu.SMEM),
                  pl.BlockSpec(memory_space=pl.ANY)],
        out_specs=pl.BlockSpec((R * SUBLANES, LANES), lambda i: (i, 0)),
        out_shape=jax.ShapeDtypeStruct((Mp * SUBLANES, LANES), src.dtype),
        scratch_shapes=[pltpu.SemaphoreType.DMA(())],
        compiler_params=_params("arbitrary"),
        name="gather_rows",
    )(idx.reshape(Mp // R, 1, R), src)


def _combine_kernel(pos_ref, h_ref, y_hbm, g_ref, o_ref, a_sc, b_sc, sem, *, R, final_norm):
    def start(r, carry):
        _row_copy(y_hbm, a_sc, pos_ref[0, 2 * r], r, sem.at[0]).start(priority=0)
        _row_copy(y_hbm, b_sc, pos_ref[0, 2 * r + 1], r, sem.at[1]).start(priority=1)
        return carry

    lax.fori_loop(0, R, start, 0, unroll=4)
    pltpu.make_async_copy(y_hbm.at[pl.ds(0, R * SUBLANES), :], a_sc, sem.at[0]).wait()
    pltpu.make_async_copy(y_hbm.at[pl.ds(0, R * SUBLANES), :], b_sc, sem.at[1]).wait()
    n = h_ref.shape[1]
    moe = jnp.concatenate([a + b for a, b in zip(_from_token_major(a_sc, n), _from_token_major(b_sc, n))], axis=1)
    out = h_ref[...] + moe
    if final_norm:
        out = _rms(out, g_ref[...], 1e-6)
    o_ref[...] = out


def combine(h, y, pos, g, *, R, final_norm):
    M, D = h.shape
    return pl.pallas_call(
        functools.partial(_combine_kernel, R=R, final_norm=final_norm),
        grid=(M // R,),
        in_specs=[pl.BlockSpec((None, 1, 2 * R), lambda i: (i, 0, 0), memory_space=pltpu.SMEM),
                  pl.BlockSpec((R, D), lambda i: (i, 0)),
                  pl.BlockSpec(memory_space=pl.ANY),
                  pl.BlockSpec((1, D), lambda i: (0, 0))],
        out_specs=pl.BlockSpec((R, D), lambda i: (i, 0)),
        out_shape=jax.ShapeDtypeStruct((M, D), F32),
        scratch_shapes=[pltpu.VMEM((R * SUBLANES, LANES), F32), pltpu.VMEM((R * SUBLANES, LANES), F32),
                        pltpu.SemaphoreType.DMA((2,))],
        compiler_params=_params("arbitrary"),
        name="combine",
    )(pos.reshape(M // R, 1, 2 * R), h, y, g.reshape(1, D))


def _rms_only_kernel(x_ref, g_ref, o_ref):
    o_ref[...] = _rms(x_ref[...], g_ref[...], 1e-6)


def rms_only(x, g, *, tm):
    M, D = x.shape
    return pl.pallas_call(
        _rms_only_kernel,
        grid=(M // tm,),
        in_specs=[pl.BlockSpec((tm, D), lambda i: (i, 0)), pl.BlockSpec((1, D), lambda i: (0, 0))],
        out_specs=pl.BlockSpec((tm, D), lambda i: (i, 0)),
        out_shape=jax.ShapeDtypeStruct((M, D), F32),
        compiler_params=_params("parallel"),
        name="rms_only",
    )(x, g.reshape(1, D))


def moe_ffn(h, h_tok, g_norm, w_router_pad, w_gu, w_d, g_final, *, tm, r_gather, r_combine, final_norm):
    M, D = h.shape
    idx_p, gate_p = router(h, g_norm, w_router_pad, tm=min(512, M))
    expert = idx_p[:, :TOP_K].reshape(-1)
    gate = gate_p[:, :TOP_K].reshape(-1)
    n_pairs = TOP_K * M
    n_tiles = n_pairs // tm + N_EXPERTS
    Mp = n_tiles * tm
    onehot = (expert[:, None] == jnp.arange(N_EXPERTS, dtype=I32)[None, :]).astype(I32)
    csum = jnp.cumsum(onehot, axis=0)
    counts = csum[-1]
    tiles_per = (counts + tm - 1) // tm
    tile_end = jnp.cumsum(tiles_per)
    tile_start = tile_end - tiles_per
    group_start = jnp.cumsum(counts) - counts
    rank = jnp.sum(onehot * (csum - 1), axis=1)
    slot_of_pair = jnp.sum(onehot * tile_start[None, :], axis=1) * tm + rank
    _, sorted_pair, sorted_gate = lax.sort((expert, jnp.arange(n_pairs, dtype=I32), gate), num_keys=1,
                                           is_stable=True)
    tile_ids = jnp.arange(n_tiles, dtype=I32)
    tile_valid = (tile_ids < tile_end[-1]).astype(I32)
    tile_expert = jnp.minimum(jnp.sum((tile_ids[:, None] >= tile_end[None, :]).astype(I32), axis=1),
                              N_EXPERTS - 1)
    within = (tile_ids - tile_start[tile_expert]) * tm
    pos = within[:, None] + jnp.arange(tm, dtype=I32)[None, :]
    live = (tile_valid[:, None] != 0) & (pos < counts[tile_expert][:, None])
    src = jnp.clip(group_start[tile_expert][:, None] + pos, 0, n_pairs - 1).reshape(Mp)
    live = live.reshape(Mp)
    packed = jnp.stack([sorted_pair, lax.bitcast_convert_type(sorted_gate, I32)], axis=1)[src]
    token_of_slot = jnp.where(live, packed[:, 0] // TOP_K, jnp.arange(Mp, dtype=I32) % M)
    scale_of_slot = jnp.where(live, lax.bitcast_convert_type(packed[:, 1], F32), 0.0)
    last_expert = tile_expert[jnp.maximum(tile_end[-1] - 1, 0)]
    tile_expert = jnp.where(tile_valid != 0, tile_expert, last_expert)

    x_sorted = gather_rows(h_tok, token_of_slot, R=r_gather)
    y_sorted = ffn(x_sorted, g_norm, w_gu, w_d, tile_expert, tile_valid, scale_of_slot.reshape(Mp, 1),
                   tm=tm, tf=w_d.shape[1] // 2, dense=False)
    return combine(h, y_sorted, slot_of_pair, g_final, R=r_combine, final_norm=final_norm)


def kernel(x_prompt, x_sample, mem_prompt, cache_attn_k, cache_attn_v, cache_mem_k, cache_mem_v, state_gla,
           page_table, rel_bias, norm_mix, w_in, w_gla_a2, b_gla_a, gla_norm, diff_subln, lambda_q1, lambda_k1,
           lambda_q2, lambda_k2, w_out, norm_mem, norm_memkv, w_mq, w_mkv, w_mo, norm_ffn, w_ffn_gu,
           w_ffn_down, w_router, w_exp_gu, w_exp_down, norm_final):
    B, L, D = x_prompt.shape
    DB, LS, _ = x_sample.shape
    depth = w_in.shape[0]
    n_mem = mem_prompt.shape[1]
    n_pages, page = page_table.shape[1], cache_attn_k.shape[2]
    past_len = n_pages * page
    M, MS = B * L, DB * LS
    TQ_ATT, TK_ATT = 1024, 512
    GLA_C = 64

    bias_p = bias_tiles(rel_bias, R=TQ_ATT, C=TK_ATT,
                        offsets=[TK_ATT - kind * TK_ATT for kind in range(TQ_ATT // TK_ATT + 1)])
    bias_s = bias_tiles(rel_bias, R=2 * LS, C=past_len + page, offsets=(past_len,),
                        rows_per_head=LS).reshape(DIFF_H, 2 * LS, past_len + page)
    cache_k = cache_attn_k.reshape(depth, -1, page * DIFF_H, DIFF_DV)
    cache_v = cache_attn_v.reshape(depth, -1, page * DIFF_H, DIFF_DV)
    kv_heads = [(DIFF_W, DIFF_H, DIFF_DV), (2 * DIFF_W, DIFF_H, DIFF_DV)]
    mem_heads = [(0, MEM_H, MEM_DH), (MEM_H * MEM_DH, MEM_H, MEM_DH)]
    w_router_pad = jnp.pad(w_router, ((0, 0), (0, 0), (0, LANES - N_EXPERTS)))

    hp = x_prompt.reshape(M, D)
    hs = x_sample.reshape(MS, D)
    pmk, pmv = [], []
    ps = ss = None
    kv_p = kv_s = None
    for l in range(depth):
        lambda_init = 0.8 - 0.6 * math.exp(-0.3 * l)
        w_in_p = jnp.pad(w_in[l], ((0, 0), (0, N_IN_PAD - N_IN))).astype(BF16)
        w_a2p = jnp.pad(w_gla_a2[l], ((0, LANES - GLA_LR), (0, 0)))
        lamp = jnp.stack([lambda_q1[l], lambda_k1[l], lambda_q2[l], lambda_k2[l]])
        w_out_a = w_out[l, :DIFF_W].astype(BF16)
        w_out_g = w_out[l, DIFF_W:].astype(BF16)
        w_mq_b, w_mo_b, w_mkv_b = w_mq[l].astype(BF16), w_mo[l].astype(BF16), w_mkv[l].astype(BF16)
        last = l == depth - 1

        proj_p, *kv_p = rms_matmul_heads(hp, norm_mix[l], w_in_p, tm=512, keep_full=True, head_outs=kv_heads,
                                         layer=l, depth=depth, stacks=kv_p)
        proj_s, *kv_s = rms_matmul_heads(hs, norm_mix[l], w_in_p, tm=512, keep_full=True, head_outs=kv_heads,
                                         layer=l, depth=depth, stacks=kv_s)
        proj_p = proj_p.reshape(B, L, N_IN_PAD)
        proj_s = proj_s.reshape(DB, LS, N_IN_PAD)
        att_p = diff_attention_prompt(proj_p, bias_p, lamp, diff_subln[l], TQ=TQ_ATT, TK=TK_ATT,
                                      lambda_init=lambda_init)
        att_s = diff_attention_decode(proj_s, cache_k, cache_v, page_table, bias_s, lamp, diff_subln[l],
                                      layer=l, lambda_init=lambda_init)
        zero_state = jnp.zeros((B, GLA_H, GLA_DK, GLA_DV), F32)
        gla_p, ps = gla(proj_p, w_a2p, b_gla_a[l], gla_norm[l], zero_state, C=GLA_C, SB=16, G=B,
                        layer=l, depth=depth, stack=ps)
        gla_s, ss = gla(proj_s, w_a2p, b_gla_a[l], gla_norm[l], state_gla[l], C=LS, SB=LS, G=8,
                        layer=l, depth=depth, stack=ss)
        hp = mm_res([att_p.reshape(M, DIFF_W), gla_p.reshape(M, GLA_W)], [w_out_a, w_out_g], hp, tm=512)
        hs = mm_res([att_s.reshape(MS, DIFF_W), gla_s.reshape(MS, GLA_W)], [w_out_a, w_out_g], hs, tm=512)

        mkv_p, mk_p, mv_p = rms_matmul_heads(mem_prompt.reshape(B * n_mem, D), norm_memkv[l], w_mkv_b, tm=512,
                                             keep_full=True, head_outs=mem_heads)
        mkv_p = mkv_p.reshape(B, n_mem, 2 * MEM_H * MEM_DH)
        mk_p = mk_p.reshape(B, n_mem, MEM_H, MEM_DH)
        mv_p = mv_p.reshape(B, n_mem, MEM_H, MEM_DH)
        q_p = rms_matmul(hp, norm_mem[l], w_mq_b, tm=512, out_dtype=BF16, scale=MEM_DH ** -0.5)
        q_s = rms_matmul(hs, norm_mem[l], w_mq_b, tm=512, out_dtype=BF16, scale=MEM_DH ** -0.5)
        xo_p = xattn_core(
            q_p.reshape(B, L, D), mkv_p, mkv_p,
            lambda h: pl.BlockSpec((None, n_mem, MEM_DH), lambda b, i: (b, 0, h)),
            lambda h: pl.BlockSpec((None, n_mem, MEM_DH), lambda b, i: (b, 0, MEM_H + h)), tm=512)
        xo_s = xattn_cached(q_s.reshape(DB, LS, D), cache_mem_k, cache_mem_v, layer=l, G=2)
        moe_layer = l % 2 == 1
        hp = mm_res([xo_p.reshape(M, D)], [w_mo_b], hp, tm=512, token_major=moe_layer)
        hs = mm_res([xo_s.reshape(MS, D)], [w_mo_b], hs, tm=512, token_major=moe_layer)
        if moe_layer:
            (hp, hp_tok), (hs, hs_tok) = hp, hs

        if l % 2 == 0:
            w_gu = w_ffn_gu[l // 2].astype(BF16)[None]
            w_d = w_ffn_down[l // 2].astype(BF16)[None]
            for_dense = lambda h, tm: ffn(
                h, norm_ffn[l], w_gu, w_d, jnp.zeros((h.shape[0] // tm,), I32),
                jnp.ones((h.shape[0] // tm,), I32), jnp.ones((h.shape[0], 1), F32),
                tm=tm, tf=w_d.shape[1] // 2, dense=True)
            hp, hs = for_dense(hp, 512), for_dense(hs, 512)
            if last:
                hp, hs = rms_only(hp, norm_final, tm=512), rms_only(hs, norm_final, tm=512)
        else:
            w_gu = w_exp_gu[l // 2].astype(BF16)
            w_d = w_exp_down[l // 2].astype(BF16)
            hp = moe_ffn(hp, hp_tok, norm_ffn[l], w_router_pad[l // 2], w_gu, w_d, norm_final,
                         tm=512, r_gather=1024, r_combine=512, final_norm=last)
            hs = moe_ffn(hs, hs_tok, norm_ffn[l], w_router_pad[l // 2], w_gu, w_d, norm_final,
                         tm=512, r_gather=1024, r_combine=512, final_norm=last)

        pmk.append(mk_p)
        pmv.append(mv_p)

    pk, pv = (a.reshape(depth, B, L, DIFF_H, DIFF_DV) for a in kv_p)
    sk, sv = (a.reshape(depth, DB, LS, DIFF_H, DIFF_DV) for a in kv_s)
    return (hp.reshape(B, L, D), hs.reshape(DB, LS, D), pk, pv, ps,
            jnp.stack(pmk), jnp.stack(pmv), sk, sv, ss)
```
